```python
import math
import jax, jax.numpy as jnp
from jax import lax
import numpy as np

D_MODEL = 1024
BATCH = 2
SEQ = 8192
DEPTH = 1

CHUNK = 64
D_SSM = D_MODEL // 2
SSM_GROUP = 16
N_SSM_GROUPS = D_SSM // SSM_GROUP
SSM_STATE = 64
D_RWKV = D_MODEL // 2
RWKV_HEAD = 64
N_RWKV_HEADS = D_RWKV // RWKV_HEAD
DECAY_LORA = 64
AAA_LORA = 64
GATE_LORA = 128
D_RWKV_IN = 3 * D_RWKV + GATE_LORA + DECAY_LORA + AAA_LORA
D_IN = D_SSM + D_RWKV_IN + 2 * D_MODEL
D_FF = 4 * D_MODEL
RMS_EPS = 1e-6
GN_EPS = 64e-5
L2_EPS = 1e-12
MIN_NEG_REAL = -1e-4

kernel_name = "hybrid_s5_rwkv7_gated_block"


def rmsnorm(x, g):
    xf = x.astype(jnp.float32)
    y = xf * lax.rsqrt(jnp.mean(xf * xf, axis=-1, keepdims=True) + RMS_EPS)
    return (y * g.astype(jnp.float32)).astype(x.dtype)


def s5_branch(u, lam_re, lam_im, log_dt, b_re, b_im, c_re, c_im, d_skip, w_glu, b_glu):
    bsz, seq, _ = u.shape
    uf = u.astype(jnp.float32).reshape(bsz, seq, N_SSM_GROUPS, SSM_GROUP)
    lr = jnp.minimum(lam_re.astype(jnp.float32), MIN_NEG_REAL)
    li = lam_im.astype(jnp.float32)
    dt = jnp.exp(log_dt.astype(jnp.float32))[:, None]
    mag = jnp.exp(lr * dt)
    ab_re = mag * jnp.cos(li * dt)
    ab_im = mag * jnp.sin(li * dt)
    den = lr * lr + li * li
    xm1 = ab_re - 1.0
    q_re = (xm1 * lr + ab_im * li) / den
    q_im = (ab_im * lr - xm1 * li) / den
    br = b_re.astype(jnp.float32)
    bi = b_im.astype(jnp.float32)
    bb_re = q_re[..., None] * br - q_im[..., None] * bi
    bb_im = q_re[..., None] * bi + q_im[..., None] * br
    bu_re = jnp.einsum('gpm,bsgm->bsgp', bb_re, uf)
    bu_im = jnp.einsum('gpm,bsgm->bsgp', bb_im, uf)
    a_re = jnp.broadcast_to(ab_re, bu_re.shape)
    a_im = jnp.broadcast_to(ab_im, bu_im.shape)

    def combine(e1, e2):
        a1r, a1i, b1r, b1i = e1
        a2r, a2i, b2r, b2i = e2
        return (a2r * a1r - a2i * a1i,
                a2r * a1i + a2i * a1r,
                a2r * b1r - a2i * b1i + b2r,
                a2r * b1i + a2i * b1r + b2i)

    _, _, xs_re, xs_im = lax.associative_scan(combine, (a_re, a_im, bu_re, bu_im), axis=1)
    y = (jnp.einsum('gmp,bsgp->bsgm', c_re.astype(jnp.float32), xs_re)
         - jnp.einsum('gmp,bsgp->bsgm', c_im.astype(jnp.float32), xs_im)
         + d_skip.astype(jnp.float32).reshape(N_SSM_GROUPS, SSM_GROUP) * uf)
    y = y.reshape(bsz, seq, D_SSM)
    z = jax.nn.gelu(y)
    out = z * jax.nn.sigmoid(z @ w_glu.astype(jnp.float32) + b_glu.astype(jnp.float32))
    return out.astype(u.dtype)


def rwkv7_branch(p, mu, w0, w2, a0, a2, g2, k_k, k_a, r_k, lnx_g, lnx_b):
    bsz, seq, _ = p.shape
    pf = p.astype(jnp.float32)
    prev = jnp.pad(pf[:, :-1], ((0, 0), (1, 0), (0, 0)))
    pm = pf + (prev - pf) * mu.astype(jnp.float32)
    o1 = D_RWKV
    o2 = 2 * D_RWKV
    o3 = 3 * D_RWKV
    o4 = o3 + GATE_LORA
    o5 = o4 + DECAY_LORA
    r = pm[..., :o1]
    k = pm[..., o1:o2]
    v = pm[..., o2:o3]
    xg = pm[..., o3:o4]
    xw = pm[..., o4:o5]
    xa = pm[..., o5:]
    w = -jax.nn.softplus(-(w0 + jnp.tanh(xw) @ w2.astype(jnp.float32))) - 0.5
    decay = jnp.exp(-jnp.exp(w))
    a = jax.nn.sigmoid(a0 + xa @ a2.astype(jnp.float32))
    g = jax.nn.sigmoid(xg) @ g2.astype(jnp.float32)

    hs = (bsz, seq, N_RWKV_HEADS, RWKV_HEAD)
    r = r.reshape(hs)
    k = k.reshape(hs)
    v = v.reshape(hs)
    decay = decay.reshape(hs)
    a = a.reshape(hs)
    kk = k * k_k.astype(jnp.float32).reshape(N_RWKV_HEADS, RWKV_HEAD)
    kk = kk / jnp.maximum(jnp.linalg.norm(kk, axis=-1, keepdims=True), L2_EPS)
    k = k * (1.0 + (a - 1.0) * k_a.astype(jnp.float32).reshape(N_RWKV_HEADS, RWKV_HEAD))
    a_vec = -kk
    b_vec = kk * a

    n_chunks = seq // CHUNK

    def to_chunks(t):
        return jnp.transpose(t, (1, 0, 2, 3)).reshape(n_chunks, CHUNK, bsz, N_RWKV_HEADS, RWKV_HEAD)

    def step(state, inp):
        rt, wt, kt, vt, at, bt = inp
        sa = jnp.einsum('bhvk,bhk->bhv', state, at)
        state = (state * wt[:, :, None, :] + sa[..., None] * bt[:, :, None, :]
                 + vt[..., None] * kt[:, :, None, :])
        yt = jnp.einsum('bhvk,bhk->bhv', state, rt)
        return state, yt

    def chunk_step(state, chunk_inp):
        return lax.scan(step, state, chunk_inp)

    state0 = jnp.zeros((bsz, N_RWKV_HEADS, RWKV_HEAD, RWKV_HEAD), jnp.float32)
    xs = tuple(to_chunks(t) for t in (r, decay, k, v, a_vec, b_vec))
    _, ys = lax.scan(chunk_step, state0, xs)
    y = jnp.transpose(ys.reshape(seq, bsz, N_RWKV_HEADS, RWKV_HEAD), (1, 0, 2, 3))

    mean = jnp.mean(y, axis=-1, keepdims=True)
    var = jnp.mean(jnp.square(y - mean), axis=-1, keepdims=True)
    y = (y - mean) * lax.rsqrt(var + GN_EPS)
    y = (y * lnx_g.astype(jnp.float32).reshape(N_RWKV_HEADS, RWKV_HEAD)
         + lnx_b.astype(jnp.float32).reshape(N_RWKV_HEADS, RWKV_HEAD))
    bonus = jnp.sum(r * k * r_k.astype(jnp.float32), axis=-1, keepdims=True) * v
    y = (y + bonus).reshape(bsz, seq, D_RWKV) * g
    return y.astype(p.dtype)


def setup_inputs(seed: int = 0) -> dict:
    key = jax.random.key(seed)
    ks = jax.random.split(key, 32)
    L = DEPTH
    G, P, M = N_SSM_GROUPS, SSM_STATE, SSM_GROUP
    f32 = jnp.float32
    nrm = lambda k, s: jax.random.normal(k, s, f32)
    x = nrm(ks[0], (BATCH, SEQ, D_MODEL))
    norm1_g = 1.0 + 0.02 * nrm(ks[1], (L, D_MODEL))
    w_in = nrm(ks[2], (L, D_MODEL, D_IN)) * D_MODEL ** -0.5
    lam_re = -0.5 + 0.01 * nrm(ks[3], (L, G, P))
    lam_im = math.pi * jnp.arange(P, dtype=f32)[None, None, :] + 0.01 * nrm(ks[4], (L, G, P))
    log_dt = jax.random.uniform(ks[5], (L, G), f32, math.log(1e-3), math.log(1e-1))
    b_re = nrm(ks[6], (L, G, P, M)) * (2.0 * M) ** -0.5
    b_im = nrm(ks[7], (L, G, P, M)) * (2.0 * M) ** -0.5
    c_re = 0.5 * nrm(ks[8], (L, G, M, P))
    c_im = 0.5 * nrm(ks[9], (L, G, M, P))
    d_skip = nrm(ks[10], (L, D_SSM))
    w_glu = nrm(ks[11], (L, D_SSM, D_SSM)) * D_SSM ** -0.5
    b_glu = 0.01 * nrm(ks[12], (L, D_SSM))
    mu_rwkv = jax.random.uniform(ks[13], (L, D_RWKV_IN), f32)
    w0 = jax.random.uniform(ks[14], (L, D_RWKV), f32, -6.5, -1.5)
    w2 = 0.1 * nrm(ks[15], (L, DECAY_LORA, D_RWKV))
    a0 = 0.1 * nrm(ks[16], (L, D_RWKV))
    a2 = 0.1 * nrm(ks[17], (L, AAA_LORA, D_RWKV))
    g2 = nrm(ks[18], (L, GATE_LORA, D_RWKV)) * GATE_LORA ** -0.5
    k_k = 0.85 + 0.02 * nrm(ks[19], (L, D_RWKV))
    k_a = 1.0 + 0.02 * nrm(ks[20], (L, D_RWKV))
    r_k = 0.1 * nrm(ks[21], (L, N_RWKV_HEADS, RWKV_HEAD))
    lnx_g = 1.0 + 0.02 * nrm(ks[22], (L, D_RWKV))
    lnx_b = 0.01 * nrm(ks[23], (L, D_RWKV))
    w_branch = nrm(ks[24], (L, D_SSM + D_RWKV, D_MODEL)) * (D_SSM ** -0.5)
    w_out = nrm(ks[25], (L, D_MODEL, D_MODEL)) * D_MODEL ** -0.5
    norm2_g = 1.0 + 0.02 * nrm(ks[26], (L, D_MODEL))
    w_ff1 = nrm(ks[27], (L, D_MODEL, D_FF)) * D_MODEL ** -0.5
    w_ff2 = nrm(ks[28], (L, D_FF, D_MODEL)) * D_FF ** -0.5
    norm_f_g = 1.0 + 0.02 * nrm(ks[29], (D_MODEL,))
    return {"x": x, "norm1_g": norm1_g, "w_in": w_in,
            "lam_re": lam_re, "lam_im": lam_im, "log_dt": log_dt,
            "b_re": b_re, "b_im": b_im, "c_re": c_re, "c_im": c_im,
            "d_skip": d_skip, "w_glu": w_glu, "b_glu": b_glu,
            "mu_rwkv": mu_rwkv, "w0": w0, "w2": w2, "a0": a0, "a2": a2, "g2": g2,
            "k_k": k_k, "k_a": k_a, "r_k": r_k, "lnx_g": lnx_g, "lnx_b": lnx_b,
            "w_branch": w_branch, "w_out": w_out, "norm2_g": norm2_g,
            "w_ff1": w_ff1, "w_ff2": w_ff2, "norm_f_g": norm_f_g}


def reference(x, norm1_g, w_in, lam_re, lam_im, log_dt, b_re, b_im, c_re, c_im,
              d_skip, w_glu, b_glu, mu_rwkv, w0, w2, a0, a2, g2, k_k, k_a, r_k,
              lnx_g, lnx_b, w_branch, w_out, norm2_g, w_ff1, w_ff2, norm_f_g):
    s1 = D_SSM
    s2 = D_SSM + D_RWKV_IN
    s3 = s2 + D_MODEL
    for l in range(DEPTH):
        h = rmsnorm(x, norm1_g[l])
        proj = h @ w_in[l]
        u = proj[..., :s1]
        p_rwkv = proj[..., s1:s2]
        gate_a = jax.nn.sigmoid(proj[..., s2:s3])
        gate_b = jax.nn.sigmoid(proj[..., s3:])
        o_a = s5_branch(u, lam_re[l], lam_im[l], log_dt[l], b_re[l], b_im[l],
                        c_re[l], c_im[l], d_skip[l], w_glu[l], b_glu[l])
        o_b = rwkv7_branch(p_rwkv, mu_rwkv[l], w0[l], w2[l], a0[l], a2[l], g2[l],
                           k_k[l], k_a[l], r_k[l], lnx_g[l], lnx_b[l])
        wb = w_branch[l]
        merged = gate_a * (o_a @ wb[:D_SSM]) + gate_b * (o_b @ wb[D_SSM:])
        x = x + merged @ w_out[l]
        h2 = rmsnorm(x, norm2_g[l])
        x = x + jnp.square(jax.nn.relu(h2 @ w_ff1[l])) @ w_ff2[l]
    return rmsnorm(x, norm_f_g)
```

```python
import functools

import jax
import jax.numpy as jnp
from jax import lax
from jax.experimental import pallas as pl
from jax.experimental.pallas import tpu as pltpu

F32 = jnp.float32
BF16 = jnp.bfloat16

RMS_EPS = 1e-6
GN_EPS = 64e-5
L2_EPS = 1e-12
MIN_NEG_REAL = -1e-4

SSM_GROUP = 16
SSM_STATE = 64
RWKV_HEAD = 64
GATE_LORA = 128
DECAY_LORA = 64
AAA_LORA = 64
CHUNK = 64

SUBLANES = 8
LANES = 128
VMEM_LIMIT = 56 * 1024 * 1024

TM_PROJ = 512
TT_S5 = 256
TT_RWKV = 128
S5_LANE_CHUNK = 512


def _dot(a, b):
    return jnp.dot(a, b, preferred_element_type=F32)


def _dot_nt(a, b):
    return lax.dot_general(a, b, (((1,), (1,)), ((), ())), preferred_element_type=F32)


def _split2(x):
    hi = x.astype(BF16)
    lo = (x - hi.astype(F32)).astype(BF16)
    return hi, lo


def _split3(x):
    hi = x.astype(BF16)
    r1 = x - hi.astype(F32)
    lo = r1.astype(BF16)
    lo2 = (r1 - lo.astype(F32)).astype(BF16)
    return hi, lo, lo2


def _dot3(a, b_hi, b_lo):
    a_hi, a_lo = _split2(a)
    return _dot(a_hi, b_hi) + _dot(a_lo, b_hi) + _dot(a_hi, b_lo)


def _rmsnorm(x, g):
    ms = jnp.mean(x * x, axis=-1, keepdims=True)
    return x * lax.rsqrt(ms + RMS_EPS) * g


def _s5_prep_kernel(lre_ref, lim_ref, ldt_ref, lre_w_ref, lim_w_ref, ldt_w_ref, bre_ref, bim_ref,
                    apow_re_ref, apow_im_ref, sh_re_ref, sh_im_ref, bbre_ref, bbim_ref):
    def discretise(lre, lim, ldt):
        lr = jnp.minimum(lre, MIN_NEG_REAL)
        dt = jnp.exp(ldt)
        mag = jnp.exp(lr * dt)
        ab_re = mag * jnp.cos(lim * dt)
        ab_im = mag * jnp.sin(lim * dt)
        den = lr * lr + lim * lim
        xm1 = ab_re - 1.0
        q_re = (xm1 * lr + ab_im * lim) / den
        q_im = (ab_im * lr - xm1 * lim) / den
        return ab_re, ab_im, q_re, q_im

    a_re, a_im, _, _ = discretise(lre_ref[...], lim_ref[...], ldt_ref[...])
    pows = [(a_re, a_im)]
    for _ in range(SUBLANES - 1):
        pr, pi = pows[-1]
        pows.append((pr * a_re - pi * a_im, pr * a_im + pi * a_re))
    shape = apow_re_ref.shape
    row = lax.broadcasted_iota(jnp.int32, shape, 0)
    apow_re = jnp.zeros(shape, F32)
    apow_im = jnp.zeros(shape, F32)
    for i, (pr, pi) in enumerate(pows):
        apow_re = jnp.where(row == i, jnp.broadcast_to(pr, shape), apow_re)
        apow_im = jnp.where(row == i, jnp.broadcast_to(pi, shape), apow_im)
    apow_re_ref[...] = apow_re
    apow_im_ref[...] = apow_im
    for i, s in enumerate((1, 2, 4)):
        pr, pi = pows[s - 1]
        sh_re_ref[i] = jnp.where(row >= s, jnp.broadcast_to(pr, shape), 0.0)
        sh_im_ref[i] = jnp.where(row >= s, jnp.broadcast_to(pi, shape), 0.0)

    _, _, q_re, q_im = discretise(lre_w_ref[...], lim_w_ref[...], ldt_w_ref[...])
    br = bre_ref[...]
    bi = bim_ref[...]
    bbre_ref[...] = q_re * br - q_im * bi
    bbim_ref[...] = q_re * bi + q_im * br


def _s5_prep(lam_re, lam_im, log_dt, b_re, b_im):
    g, p = lam_re.shape
    m = b_re.shape[-1]
    n = g * p
    row = lambda a: a.reshape(1, n)
    wide = lambda a: jnp.repeat(a, m, axis=-1)
    ldt_gp = jnp.broadcast_to(log_dt[:, None], (g, p))
    out_shape = (
        jax.ShapeDtypeStruct((SUBLANES, n), F32), jax.ShapeDtypeStruct((SUBLANES, n), F32),
        jax.ShapeDtypeStruct((3, SUBLANES, n), F32), jax.ShapeDtypeStruct((3, SUBLANES, n), F32),
        jax.ShapeDtypeStruct((g, p * m), F32), jax.ShapeDtypeStruct((g, p * m), F32),
    )
    return pl.pallas_call(_s5_prep_kernel, out_shape=out_shape, name="s5_prep")(
        row(lam_re), row(lam_im), row(ldt_gp), wide(lam_re), wide(lam_im), wide(ldt_gp),
        b_re.reshape(g, p * m), b_im.reshape(g, p * m))


def _block_diag(blocks):
    g, r, c = blocks.shape
    eye = jnp.eye(g, dtype=bool)
    return jnp.where(eye[:, None, :, None], blocks[:, :, None, :], 0.0).reshape(g * r, g * c)


def _in_proj_kernel(x_ref, g_ref, w_ref, u_ref, p_ref, *, d_ssm):
    h = _rmsnorm(x_ref[...], g_ref[...]).astype(BF16)
    out = _dot(h, w_ref[...])
    u_ref[...] = out[:, :d_ssm]
    p_ref[...] = out[:, d_ssm:]


def _in_proj(x2, norm_g, w_mix, d_ssm):
    t, d = x2.shape
    n = w_mix.shape[1]
    tm = min(TM_PROJ, t)
    const = lambda i: (0, 0)
    return pl.pallas_call(
        functools.partial(_in_proj_kernel, d_ssm=d_ssm),
        grid=(t // tm,),
        in_specs=[pl.BlockSpec((tm, d), lambda i: (i, 0)),
                  pl.BlockSpec((1, d), const),
                  pl.BlockSpec((d, n), const)],
        out_specs=[pl.BlockSpec((tm, d_ssm), lambda i: (i, 0)),
                   pl.BlockSpec((tm, n - d_ssm), lambda i: (i, 0))],
        out_shape=[jax.ShapeDtypeStruct((t, d_ssm), F32), jax.ShapeDtypeStruct((t, n - d_ssm), F32)],
        compiler_params=pltpu.CompilerParams(dimension_semantics=("arbitrary",),
                                             vmem_limit_bytes=VMEM_LIMIT),
        name="in_proj",
    )(x2, norm_g.reshape(1, d), w_mix)


def _s5_kernel(u_ref, bmat_ref, cmat_ref, sh_re_ref, sh_im_ref, apow_re_ref, apow_im_ref,
               d_ref, wglu_ref, bglu_ref, o_ref, xs_ref, carry_ref, *, n_state):
    tt = u_ref.shape[0]

    @pl.when(pl.program_id(1) == 0)
    def _():
        carry_ref[...] = jnp.zeros_like(carry_ref)

    u = u_ref[...]
    xs_ref[...] = _dot(u.astype(BF16), bmat_ref[...])

    lw = S5_LANE_CHUNK
    for c in range(n_state // lw):
        re_sl = pl.ds(c * lw, lw)
        im_sl = pl.ds(n_state + c * lw, lw)

        def block(i, carry, re_sl=re_sl, im_sl=im_sl):
            cr, ci = carry
            rows = pl.ds(pl.multiple_of(i * SUBLANES, SUBLANES), SUBLANES)
            xr = xs_ref[rows, re_sl]
            xi = xs_ref[rows, im_sl]
            for k, s in enumerate((1, 2, 4)):
                ar = sh_re_ref[k, :, re_sl]
                ai = sh_im_ref[k, :, re_sl]
                sr = pltpu.roll(xr, s, 0)
                si = pltpu.roll(xi, s, 0)
                xr, xi = xr + (ar * sr - ai * si), xi + (ar * si + ai * sr)
            pr = apow_re_ref[:, re_sl]
            pi = apow_im_ref[:, re_sl]
            xr = xr + (pr * cr - pi * ci)
            xi = xi + (pr * ci + pi * cr)
            xs_ref[rows, re_sl] = xr
            xs_ref[rows, im_sl] = xi
            last = SUBLANES - 1
            return (jnp.broadcast_to(xr[last:last + 1], xr.shape),
                    jnp.broadcast_to(xi[last:last + 1], xi.shape))

        cr, ci = lax.fori_loop(0, tt // SUBLANES, block,
                               (carry_ref[:, re_sl], carry_ref[:, im_sl]), unroll=2)
        carry_ref[:, re_sl] = cr
        carry_ref[:, im_sl] = ci

    y = _dot(xs_ref[...].astype(BF16), cmat_ref[...]) + d_ref[...] * u
    z = jax.nn.gelu(y)
    gate = jax.nn.sigmoid(_dot(z.astype(BF16), wglu_ref[...]) + bglu_ref[...])
    o_ref[...] = z * gate


def _s5_branch(u2, bsz, bmat, cmat, sh_re, sh_im, apow_re, apow_im, d_skip, w_glu, b_glu):
    t, d_ssm = u2.shape
    seq = t // bsz
    n_state = apow_re.shape[1]
    tt = min(TT_S5, seq)
    nt = seq // tt
    tok = lambda b, i: (b * nt + i, 0)
    c2 = lambda b, i: (0, 0)
    c3 = lambda b, i: (0, 0, 0)
    return pl.pallas_call(
        functools.partial(_s5_kernel, n_state=n_state),
        grid=(bsz, nt),
        in_specs=[pl.BlockSpec((tt, d_ssm), tok),
                  pl.BlockSpec(bmat.shape, c2), pl.BlockSpec(cmat.shape, c2),
                  pl.BlockSpec(sh_re.shape, c3), pl.BlockSpec(sh_im.shape, c3),
                  pl.BlockSpec(apow_re.shape, c2), pl.BlockSpec(apow_im.shape, c2),
                  pl.BlockSpec((1, d_ssm), c2), pl.BlockSpec(w_glu.shape, c2),
                  pl.BlockSpec((1, d_ssm), c2)],
        out_specs=pl.BlockSpec((tt, d_ssm), tok),
        out_shape=jax.ShapeDtypeStruct((t, d_ssm), F32),
        scratch_shapes=[pltpu.VMEM((tt, 2 * n_state), F32),
                        pltpu.VMEM((SUBLANES, 2 * n_state), F32)],
        compiler_params=pltpu.CompilerParams(dimension_semantics=("arbitrary", "arbitrary"),
                                             vmem_limit_bytes=VMEM_LIMIT),
        name="s5",
    )(u2, bmat, cmat, sh_re, sh_im, apow_re, apow_im, d_skip.reshape(1, d_ssm), w_glu,
      b_glu.reshape(1, d_ssm))


def _pair_blocks(x):
    lane = lax.broadcasted_iota(jnp.int32, x.shape, 1)
    first = lane < RWKV_HEAD
    return jnp.concatenate([jnp.where(first, x, 0.0), jnp.where(first, 0.0, x)], axis=0)


def _rwkv_pair_chunk(at, rt, bt, kt, bh, kh, v, decay_c, h):
    n2 = 2 * CHUNK
    bd_at = _pair_blocks(at)
    bd_rt = _pair_blocks(rt)
    lhs = jnp.concatenate([bd_at, bd_rt], axis=0).astype(BF16)
    rhs = jnp.concatenate([_pair_blocks(bt), _pair_blocks(kt)], axis=0).astype(BF16)
    sc = _dot_nt(lhs, rhs)
    row = lax.broadcasted_iota(jnp.int32, (n2, n2), 0)
    col = lax.broadcasted_iota(jnp.int32, (n2, n2), 1)
    strict = row > col
    incl = row >= col
    eye = row == col
    s_ab = jnp.where(strict, sc[:n2, :n2], 0.0)
    s_ak = jnp.where(strict, sc[:n2, n2:], 0.0)
    s_rb = jnp.where(incl, sc[n2:, :n2], 0.0)
    s_rk = jnp.where(incl, sc[n2:, n2:], 0.0)

    npow = s_ab.astype(BF16)
    inv = jnp.where(eye, 1.0, 0.0) + s_ab
    for _ in range(5):
        npow = _dot(npow, npow).astype(BF16)
        inv = inv + _dot(inv.astype(BF16), npow)

    bd_v = _pair_blocks(v).astype(BF16)
    av = _dot(s_ak.astype(BF16), bd_v)
    wu = _dot(inv.astype(BF16), jnp.concatenate([bd_at.astype(BF16), av.astype(BF16)], axis=1))
    z = jnp.concatenate(
        [wu.astype(BF16), jnp.concatenate([jnp.zeros((n2, n2), BF16), bd_v], axis=1)], axis=0)
    qy = _dot(jnp.concatenate([s_rb, s_rk], axis=1).astype(BF16), z)
    bk_t = jnp.concatenate([_pair_blocks(bh).T, _pair_blocks(kh).T], axis=1).astype(BF16)
    mg = _dot(bk_t, z)
    qh = bd_rt + qy[:, :n2]
    m = jnp.where(eye, jnp.broadcast_to(decay_c, (n2, n2)), 0.0) + mg[:, :n2]
    hb = h.astype(BF16)
    y_bd = _dot(qh.astype(BF16), hb) + qy[:, n2:]
    h_new = _dot(m.astype(BF16), hb) + mg[:, n2:]
    return y_bd[:CHUNK] + y_bd[CHUNK:], h_new


def _rwkv_kernel(p_ref, mu_ref, wwa_hi_ref, wwa_lo_ref, g2_hi_ref, g2_lo_ref, w0_ref, a0_ref,
                 kk_ref, ka_ref, rk_ref, lng_ref, lnb_ref, ones_ref, ltri_ref,
                 o_ref, h_ref, prev_ref, *, d):
    tt = p_ref.shape[0]

    @pl.when(pl.program_id(1) == 0)
    def _():
        h_ref[...] = jnp.zeros_like(h_ref)
        prev_ref[...] = jnp.zeros_like(prev_ref)

    p = p_ref[...]
    row = lax.broadcasted_iota(jnp.int32, p.shape, 0)
    last = prev_ref[SUBLANES - 1:SUBLANES, :]
    prev = jnp.where(row == 0, jnp.broadcast_to(last, p.shape), pltpu.roll(p, 1, 0))
    prev_ref[...] = p[tt - SUBLANES:, :]
    pm = p + (prev - p) * mu_ref[...]

    r = pm[:, :d]
    k = pm[:, d:2 * d]
    v = pm[:, 2 * d:3 * d]
    xg = pm[:, 3 * d:3 * d + GATE_LORA]
    xwa = pm[:, 3 * d + GATE_LORA:]
    lane = lax.broadcasted_iota(jnp.int32, xwa.shape, 1)
    zwa = jnp.where(lane < DECAY_LORA, jnp.tanh(xwa), xwa)
    lora = _dot3(zwa, wwa_hi_ref[...], wwa_lo_ref[...])
    w = -jax.nn.softplus(-(w0_ref[...] + lora[:, :d])) - 0.5
    logw = -jnp.exp(w)
    a = jax.nn.sigmoid(a0_ref[...] + lora[:, d:])
    g = _dot3(jax.nn.sigmoid(xg), g2_hi_ref[...], g2_lo_ref[...])

    ones = ones_ref[...]

    def head_sum(x):
        hi, lo = _split2(x)
        return _dot(hi, ones) + _dot(lo, ones)

    kk = k * kk_ref[...]
    kk = kk / jnp.maximum(jnp.sqrt(head_sum(kk * kk)), L2_EPS)
    k2 = k * (1.0 + (a - 1.0) * ka_ref[...])
    av = -kk
    bv = kk * a

    ltri = ltri_ref[...]
    l_hi, l_lo, l_lo2 = _split3(logw)
    lc = _dot(ltri, l_hi) + _dot(ltri, l_lo) + _dot(ltri, l_lo2)

    ys = []
    for c in range(tt // CHUNK):
        rows = slice(c * CHUNK, (c + 1) * CHUNK)
        lc_c = lc[rows]
        lw_c = logw[rows]
        lc_end = lc_c[CHUNK - 1:CHUNK]
        e_in = jnp.exp(lc_c - lw_c)
        e_inc = jnp.exp(lc_c)
        e_neg = jnp.exp(-lc_c)
        e_out = jnp.exp(lc_end - lc_c)
        at = av[rows] * e_in
        rt = r[rows] * e_inc
        bt = bv[rows] * e_neg
        kt = k2[rows] * e_neg
        bh = bv[rows] * e_out
        kh = k2[rows] * e_out
        decay_c = jnp.exp(lc_end)
        y_pairs = []
        for j in range(d // LANES):
            ln = slice(j * LANES, (j + 1) * LANES)
            y_j, h_new = _rwkv_pair_chunk(at[:, ln], rt[:, ln], bt[:, ln], kt[:, ln], bh[:, ln],
                                          kh[:, ln], v[rows, ln], decay_c[:, ln], h_ref[j])
            h_ref[j] = h_new
            y_pairs.append(y_j)
        ys.append(jnp.concatenate(y_pairs, axis=1))
    y = jnp.concatenate(ys, axis=0)

    inv_n = 1.0 / RWKV_HEAD
    mean = head_sum(y) * inv_n
    yc = y - mean
    var = head_sum(yc * yc) * inv_n
    yn = yc * lax.rsqrt(var + GN_EPS) * lng_ref[...] + lnb_ref[...]
    bonus = head_sum(r * k2 * rk_ref[...]) * v
    o_ref[...] = (yn + bonus) * g


def _rwkv_branch(p2, bsz, mu, w0, w2, a0, a2, g2, k_k, k_a, r_k, lnx_g, lnx_b):
    t, d_in = p2.shape
    d = w0.shape[0]
    seq = t // bsz
    tt = min(TT_RWKV, seq)
    nt = seq // tt
    wwa = jnp.concatenate([jnp.concatenate([w2, jnp.zeros_like(w2)], axis=1),
                           jnp.concatenate([jnp.zeros_like(a2), a2], axis=1)], axis=0)
    wwa_hi, wwa_lo = _split2(wwa)
    g2_hi, g2_lo = _split2(g2)
    head = jnp.arange(d) // RWKV_HEAD
    ones = (head[:, None] == head[None, :]).astype(BF16)
    tok_i = jnp.arange(tt)
    ltri = ((tok_i[:, None] // CHUNK == tok_i[None, :] // CHUNK)
            & (tok_i[:, None] >= tok_i[None, :])).astype(BF16)
    vec = lambda a: a.reshape(1, -1)
    tok = lambda b, i: (b * nt + i, 0)
    c2 = lambda b, i: (0, 0)
    consts = [vec(mu), wwa_hi, wwa_lo, g2_hi, g2_lo, vec(w0), vec(a0), vec(k_k), vec(k_a), vec(r_k),
              vec(lnx_g), vec(lnx_b), ones, ltri]
    return pl.pallas_call(
        functools.partial(_rwkv_kernel, d=d),
        grid=(bsz, nt),
        in_specs=[pl.BlockSpec((tt, d_in), tok)] + [pl.BlockSpec(c.shape, c2) for c in consts],
        out_specs=pl.BlockSpec((tt, d), tok),
        out_shape=jax.ShapeDtypeStruct((t, d), F32),
        scratch_shapes=[pltpu.VMEM((d // LANES, LANES, LANES), F32),
                        pltpu.VMEM((SUBLANES, d_in), F32)],
        compiler_params=pltpu.CompilerParams(dimension_semantics=("arbitrary", "arbitrary"),
                                             vmem_limit_bytes=VMEM_LIMIT),
        name="rwkv",
    )(p2, *consts)


def _tail_kernel(x_ref, oa_ref, ob_ref, g1_ref, wg_ref, wba_ref, wbb_ref, wout_ref, g2_ref,
                 w1_ref, w2_ref, gf_ref, o_ref, *, ff_block):
    x = x_ref[...]
    d = x.shape[1]
    h = _rmsnorm(x, g1_ref[...]).astype(BF16)
    gates = jax.nn.sigmoid(_dot(h, wg_ref[...]))
    ma = _dot(oa_ref[...].astype(BF16), wba_ref[...])
    mb = _dot(ob_ref[...].astype(BF16), wbb_ref[...])
    merged = gates[:, :d] * ma + gates[:, d:] * mb
    x1 = x + _dot(merged.astype(BF16), wout_ref[...])
    h2 = _rmsnorm(x1, g2_ref[...]).astype(BF16)
    acc = x1
    for j in range(w1_ref.shape[1] // ff_block):
        cols = slice(j * ff_block, (j + 1) * ff_block)
        hid = jnp.square(jnp.maximum(_dot(h2, w1_ref[:, cols]), 0.0))
        acc = acc + _dot(hid.astype(BF16), w2_ref[cols, :])
    o_ref[...] = _rmsnorm(acc, gf_ref[...])


def _tail(x2, o_a, o_b, norm1_g, w_gate, wb_a, wb_b, w_out, norm2_g, w_ff1, w_ff2, norm_f_g):
    t, d = x2.shape
    d_mix = o_a.shape[1]
    tm = min(TM_PROJ, t)
    tok = lambda i: (i, 0)
    const = lambda i: (0, 0)
    resident = lambda a: pl.BlockSpec(a.shape, const, pipeline_mode=pl.Buffered(1))
    vec = lambda a: a.reshape(1, d)
    return pl.pallas_call(
        functools.partial(_tail_kernel, ff_block=d),
        grid=(t // tm,),
        in_specs=[pl.BlockSpec((tm, d), tok), pl.BlockSpec((tm, d_mix), tok),
                  pl.BlockSpec((tm, d_mix), tok), pl.BlockSpec((1, d), const),
                  resident(w_gate), resident(wb_a), resident(wb_b), resident(w_out),
                  pl.BlockSpec((1, d), const), resident(w_ff1), resident(w_ff2),
                  pl.BlockSpec((1, d), const)],
        out_specs=pl.BlockSpec((tm, d), tok),
        out_shape=jax.ShapeDtypeStruct((t, d), F32),
        compiler_params=pltpu.CompilerParams(dimension_semantics=("arbitrary",),
                                             vmem_limit_bytes=VMEM_LIMIT),
        name="tail",
    )(x2, o_a, o_b, vec(norm1_g), w_gate, wb_a, wb_b, w_out, vec(norm2_g), w_ff1, w_ff2,
      vec(norm_f_g))


def _layer(x2, bsz, norm1_g, w_in, lam_re, lam_im, log_dt, b_re, b_im, c_re, c_im, d_skip, w_glu,
           b_glu, mu_rwkv, w0, w2, a0, a2, g2, k_k, k_a, r_k, lnx_g, lnx_b, w_branch, w_out,
           norm2_g, w_ff1, w_ff2, out_g):
    d_ssm = d_skip.shape[0]
    d_rwkv = w0.shape[0]
    d_mix_in = d_ssm + mu_rwkv.shape[0]
    n_grp, n_state, grp = b_re.shape

    apow_re, apow_im, sh_re, sh_im, bb_re, bb_im = _s5_prep(lam_re, lam_im, log_dt, b_re, b_im)
    to_blocks = lambda bb: jnp.swapaxes(bb.reshape(n_grp, n_state, grp), 1, 2)
    bmat = jnp.concatenate([_block_diag(to_blocks(bb_re)), _block_diag(to_blocks(bb_im))],
                           axis=1).astype(BF16)
    c_blocks = lambda c: _block_diag(jnp.swapaxes(c, 1, 2))
    cmat = jnp.concatenate([c_blocks(c_re), -c_blocks(c_im)], axis=0).astype(BF16)

    u2, p2 = _in_proj(x2, norm1_g, w_in[:, :d_mix_in].astype(BF16), d_ssm)
    o_a = _s5_branch(u2, bsz, bmat, cmat, sh_re, sh_im, apow_re, apow_im, d_skip,
                     w_glu.astype(BF16), b_glu)
    o_b = _rwkv_branch(p2, bsz, mu_rwkv, w0, w2, a0, a2, g2, k_k, k_a, r_k.reshape(-1), lnx_g, lnx_b)
    del d_rwkv
    return _tail(x2, o_a, o_b, norm1_g, w_in[:, d_mix_in:].astype(BF16),
                 w_branch[:d_ssm].astype(BF16), w_branch[d_ssm:].astype(BF16),
                 w_out.astype(BF16), norm2_g, w_ff1.astype(BF16), w_ff2.astype(BF16), out_g)


def kernel(x, norm1_g, w_in, lam_re, lam_im, log_dt, b_re, b_im, c_re, c_im, d_skip, w_glu, b_glu,
           mu_rwkv, w0, w2, a0, a2, g2, k_k, k_a, r_k, lnx_g, lnx_b, w_branch, w_out, norm2_g,
           w_ff1, w_ff2, norm_f_g):
    bsz, seq, d = x.shape
    depth = norm1_g.shape[0]
    assert depth == 1, "the fused tail applies the final norm, so exactly one layer is supported"
    x2 = x.reshape(bsz * seq, d)
    l = 0
    out = _layer(x2, bsz, norm1_g[l], w_in[l], lam_re[l], lam_im[l], log_dt[l], b_re[l], b_im[l],
                 c_re[l], c_im[l], d_skip[l], w_glu[l], b_glu[l], mu_rwkv[l], w0[l], w2[l], a0[l],
                 a2[l], g2[l], k_k[l], k_a[l], r_k[l], lnx_g[l], lnx_b[l], w_branch[l], w_out[l],
                 norm2_g[l], w_ff1[l], w_ff2[l], norm_f_g)
    return out.reshape(bsz, seq, d)
```

```python
import functools

import jax
import jax.numpy as jnp
from jax import lax
from jax.experimental import pallas as pl
from jax.experimental.pallas import tpu as pltpu

F32 = jnp.float32
BF16 = jnp.bfloat16

RMS_EPS = 1e-6
GN_EPS = 64e-5
L2_EPS = 1e-12
MIN_NEG_REAL = -1e-4

SSM_GROUP = 16
SSM_STATE = 64
RWKV_HEAD = 64
GATE_LORA = 128
DECAY_LORA = 64
AAA_LORA = 64
CHUNK = 64

SUBLANES = 8
LANES = 128
VMEM_LIMIT = 56 * 1024 * 1024

TM_PROJ = 512
TT_S5 = 256
TT_RWKV = 128
S5_LANE_CHUNK = 512


def _dot(a, b):
    return jnp.dot(a, b, preferred_element_type=F32)


def _dot_nt(a, b):
    return lax.dot_general(a, b, (((1,), (1,)), ((), ())), preferred_element_type=F32)


def _split2(x):
    hi = x.astype(BF16)
    lo = (x - hi.astype(F32)).astype(BF16)
    return hi, lo


def _split3(x):
    hi = x.astype(BF16)
    r1 = x - hi.astype(F32)
    lo = r1.astype(BF16)
    lo2 = (r1 - lo.astype(F32)).astype(BF16)
    return hi, lo, lo2


def _dot3(a, b_hi, b_lo):
    a_hi, a_lo = _split2(a)
    return _dot(a_hi, b_hi) + _dot(a_lo, b_hi) + _dot(a_hi, b_lo)


def _rmsnorm(x, g):
    ms = jnp.mean(x * x, axis=-1, keepdims=True)
    return x * lax.rsqrt(ms + RMS_EPS) * g


def _s5_prep_kernel(lre_ref, lim_ref, ldt_ref, lre_w_ref, lim_w_ref, ldt_w_ref, bre_ref, bim_ref,
                    apow_re_ref, apow_im_ref, sh_re_ref, sh_im_ref, bbre_ref, bbim_ref):
    def discretise(lre, lim, ldt):
        lr = jnp.minimum(lre, MIN_NEG_REAL)
        dt = jnp.exp(ldt)
        mag = jnp.exp(lr * dt)
        ab_re = mag * jnp.cos(lim * dt)
        ab_im = mag * jnp.sin(lim * dt)
        den = lr * lr + lim * lim
        xm1 = ab_re - 1.0
        q_re = (xm1 * lr + ab_im * lim) / den
        q_im = (ab_im * lr - xm1 * lim) / den
        return ab_re, ab_im, q_re, q_im

    a_re, a_im, _, _ = discretise(lre_ref[...], lim_ref[...], ldt_ref[...])
    pows = [(a_re, a_im)]
    for _ in range(SUBLANES - 1):
        pr, pi = pows[-1]
        pows.append((pr * a_re - pi * a_im, pr * a_im + pi * a_re))
    shape = apow_re_ref.shape
    row = lax.broadcasted_iota(jnp.int32, shape, 0)
    apow_re = jnp.zeros(shape, F32)
    apow_im = jnp.zeros(shape, F32)
    for i, (pr, pi) in enumerate(pows):
        apow_re = jnp.where(row == i, jnp.broadcast_to(pr, shape), apow_re)
        apow_im = jnp.where(row == i, jnp.broadcast_to(pi, shape), apow_im)
    apow_re_ref[...] = apow_re
    apow_im_ref[...] = apow_im
    for i, s in enumerate((1, 2, 4)):
        pr, pi = pows[s - 1]
        sh_re_ref[i] = jnp.where(row >= s, jnp.broadcast_to(pr, shape), 0.0)
        sh_im_ref[i] = jnp.where(row >= s, jnp.broadcast_to(pi, shape), 0.0)

    _, _, q_re, q_im = discretise(lre_w_ref[...], lim_w_ref[...], ldt_w_ref[...])
    br = bre_ref[...]
    bi = bim_ref[...]
    bbre_ref[...] = q_re * br - q_im * bi
    bbim_ref[...] = q_re * bi + q_im * br


def _s5_prep(lam_re, lam_im, log_dt, b_re, b_im):
    g, p = lam_re.shape
    m = b_re.shape[-1]
    n = g * p
    row = lambda a: a.reshape(1, n)
    wide = lambda a: jnp.repeat(a, m, axis=-1)
    ldt_gp = jnp.broadcast_to(log_dt[:, None], (g, p))
    out_shape = (
        jax.ShapeDtypeStruct((SUBLANES, n), F32), jax.ShapeDtypeStruct((SUBLANES, n), F32),
        jax.ShapeDtypeStruct((3, SUBLANES, n), F32), jax.ShapeDtypeStruct((3, SUBLANES, n), F32),
        jax.ShapeDtypeStruct((g, p * m), F32), jax.ShapeDtypeStruct((g, p * m), F32),
    )
    return pl.pallas_call(_s5_prep_kernel, out_shape=out_shape, name="s5_prep")(
        row(lam_re), row(lam_im), row(ldt_gp), wide(lam_re), wide(lam_im), wide(ldt_gp),
        b_re.reshape(g, p * m), b_im.reshape(g, p * m))


def _block_diag(blocks):
    g, r, c = blocks.shape
    eye = jnp.eye(g, dtype=bool)
    return jnp.where(eye[:, None, :, None], blocks[:, :, None, :], 0.0).reshape(g * r, g * c)


def _in_proj_kernel(x_ref, g_ref, w_ref, u_ref, p_ref, *, d_ssm):
    h = _rmsnorm(x_ref[...], g_ref[...]).astype(BF16)
    out = _dot(h, w_ref[...])
    u_ref[...] = out[:, :d_ssm]
    p_ref[...] = out[:, d_ssm:]


def _in_proj(x2, norm_g, w_mix, d_ssm):
    t, d = x2.shape
    n = w_mix.shape[1]
    tm = min(TM_PROJ, t)
    const = lambda i: (0, 0)
    return pl.pallas_call(
        functools.partial(_in_proj_kernel, d_ssm=d_ssm),
        grid=(t // tm,),
        in_specs=[pl.BlockSpec((tm, d), lambda i: (i, 0)),
                  pl.BlockSpec((1, d), const),
                  pl.BlockSpec((d, n), const)],
        out_specs=[pl.BlockSpec((tm, d_ssm), lambda i: (i, 0)),
                   pl.BlockSpec((tm, n - d_ssm), lambda i: (i, 0))],
        out_shape=[jax.ShapeDtypeStruct((t, d_ssm), F32), jax.ShapeDtypeStruct((t, n - d_ssm), F32)],
        compiler_params=pltpu.CompilerParams(dimension_semantics=("arbitrary",),
                                             vmem_limit_bytes=VMEM_LIMIT),
        name="in_proj",
    )(x2, norm_g.reshape(1, d), w_mix)


def _s5_kernel(u_ref, bmat_ref, cmat_ref, sh_re_ref, sh_im_ref, apow_re_ref, apow_im_ref,
               d_ref, wglu_ref, bglu_ref, o_ref, xs_ref, carry_ref, *, n_state):
    tt = u_ref.shape[0]

    @pl.when(pl.program_id(1) == 0)
    def _():
        carry_ref[...] = jnp.zeros_like(carry_ref)

    u = u_ref[...]
    xs_ref[...] = _dot(u.astype(BF16), bmat_ref[...])

    lw = S5_LANE_CHUNK
    for c in range(n_state // lw):
        re_sl = pl.ds(c * lw, lw)
        im_sl = pl.ds(n_state + c * lw, lw)

        def block(i, carry, re_sl=re_sl, im_sl=im_sl):
            cr, ci = carry
            rows = pl.ds(pl.multiple_of(i * SUBLANES, SUBLANES), SUBLANES)
            xr = xs_ref[rows, re_sl]
            xi = xs_ref[rows, im_sl]
            for k, s in enumerate((1, 2, 4)):
                ar = sh_re_ref[k, :, re_sl]
                ai = sh_im_ref[k, :, re_sl]
                sr = pltpu.roll(xr, s, 0)
                si = pltpu.roll(xi, s, 0)
                xr, xi = xr + (ar * sr - ai * si), xi + (ar * si + ai * sr)
            pr = apow_re_ref[:, re_sl]
            pi = apow_im_ref[:, re_sl]
            xr = xr + (pr * cr - pi * ci)
            xi = xi + (pr * ci + pi * cr)
            xs_ref[rows, re_sl] = xr
            xs_ref[rows, im_sl] = xi
            last = SUBLANES - 1
            return (jnp.broadcast_to(xr[last:last + 1], xr.shape),
                    jnp.broadcast_to(xi[last:last + 1], xi.shape))

        cr, ci = lax.fori_loop(0, tt // SUBLANES, block,
                               (carry_ref[:, re_sl], carry_ref[:, im_sl]), unroll=2)
        carry_ref[:, re_sl] = cr
        carry_ref[:, im_sl] = ci

    y = _dot(xs_ref[...].astype(BF16), cmat_ref[...]) + d_ref[...] * u
    z = jax.nn.gelu(y)
    gate = jax.nn.sigmoid(_dot(z.astype(BF16), wglu_ref[...]) + bglu_ref[...])
    o_ref[...] = z * gate


def _s5_branch(u2, bsz, bmat, cmat, sh_re, sh_im, apow_re, apow_im, d_skip, w_glu, b_glu):
    t, d_ssm = u2.shape
    seq = t // bsz
    n_state = apow_re.shape[1]
    tt = min(TT_S5, seq)
    nt = seq // tt
    tok = lambda b, i: (b * nt + i, 0)
    c2 = lambda b, i: (0, 0)
    c3 = lambda b, i: (0, 0, 0)
    return pl.pallas_call(
        functools.partial(_s5_kernel, n_state=n_state),
        grid=(bsz, nt),
        in_specs=[pl.BlockSpec((tt, d_ssm), tok),
                  pl.BlockSpec(bmat.shape, c2), pl.BlockSpec(cmat.shape, c2),
                  pl.BlockSpec(sh_re.shape, c3), pl.BlockSpec(sh_im.shape, c3),
                  pl.BlockSpec(apow_re.shape, c2), pl.BlockSpec(apow_im.shape, c2),
                  pl.BlockSpec((1, d_ssm), c2), pl.BlockSpec(w_glu.shape, c2),
                  pl.BlockSpec((1, d_ssm), c2)],
        out_specs=pl.BlockSpec((tt, d_ssm), tok),
        out_shape=jax.ShapeDtypeStruct((t, d_ssm), F32),
        scratch_shapes=[pltpu.VMEM((tt, 2 * n_state), F32),
                        pltpu.VMEM((SUBLANES, 2 * n_state), F32)],
        compiler_params=pltpu.CompilerParams(dimension_semantics=("arbitrary", "arbitrary"),
                                             vmem_limit_bytes=VMEM_LIMIT),
        name="s5",
    )(u2, bmat, cmat, sh_re, sh_im, apow_re, apow_im, d_skip.reshape(1, d_ssm), w_glu,
      b_glu.reshape(1, d_ssm))


def _pair_blocks(x):
    lane = lax.broadcasted_iota(jnp.int32, x.shape, 1)
    first = lane < RWKV_HEAD
    return jnp.concatenate([jnp.where(first, x, 0.0), jnp.where(first, 0.0, x)], axis=0)


def _rwkv_chunks(items, h_ref):
    n2 = 2 * CHUNK
    row = lax.broadcasted_iota(jnp.int32, (n2, n2), 0)
    col = lax.broadcasted_iota(jnp.int32, (n2, n2), 1)
    strict = row > col
    incl = row >= col
    eye = row == col

    for it in items:
        it["bd_at"] = _pair_blocks(it["at"])
        it["bd_rt"] = _pair_blocks(it["rt"])
        lhs = jnp.concatenate([it["bd_at"], it["bd_rt"]], axis=0).astype(BF16)
        rhs = jnp.concatenate([_pair_blocks(it["bt"]), _pair_blocks(it["kt"])], axis=0).astype(BF16)
        sc = _dot_nt(lhs, rhs)
        it["s_ab"] = jnp.where(strict, sc[:n2, :n2], 0.0)
        it["s_ak"] = jnp.where(strict, sc[:n2, n2:], 0.0).astype(BF16)
        it["s_r"] = jnp.concatenate([jnp.where(incl, sc[n2:, :n2], 0.0),
                                     jnp.where(incl, sc[n2:, n2:], 0.0)], axis=1).astype(BF16)
        it["bd_v"] = _pair_blocks(it["v"]).astype(BF16)
        it["bk_t"] = jnp.concatenate([_pair_blocks(it["bh"]).T, _pair_blocks(it["kh"]).T],
                                     axis=1).astype(BF16)

    for it in items:
        it["npow"] = it["s_ab"].astype(BF16)
        it["inv"] = jnp.where(eye, 1.0, 0.0) + it["s_ab"]
    for _ in range(5):
        for it in items:
            it["npow"] = _dot(it["npow"], it["npow"]).astype(BF16)
        for it in items:
            it["inv"] = it["inv"] + _dot(it["inv"].astype(BF16), it["npow"])

    for it in items:
        it["av"] = _dot(it["s_ak"], it["bd_v"]).astype(BF16)
    for it in items:
        wu = _dot(it["inv"].astype(BF16),
                  jnp.concatenate([it["bd_at"].astype(BF16), it["av"]], axis=1))
        it["z"] = jnp.concatenate(
            [wu.astype(BF16), jnp.concatenate([jnp.zeros((n2, n2), BF16), it["bd_v"]], axis=1)],
            axis=0)
    for it in items:
        qy = _dot(it["s_r"], it["z"])
        mg = _dot(it["bk_t"], it["z"])
        it["qh"] = (it["bd_rt"] + qy[:, :n2]).astype(BF16)
        it["yp"] = qy[:, n2:]
        it["m"] = (jnp.where(eye, jnp.broadcast_to(it["decay"], (n2, n2)), 0.0)
                   + mg[:, :n2]).astype(BF16)
        it["g"] = mg[:, n2:]

    for it in items:
        hb = h_ref[it["pair"]].astype(BF16)
        y_bd = _dot(it["qh"], hb) + it["yp"]
        h_ref[it["pair"]] = _dot(it["m"], hb) + it["g"]
        it["y"] = y_bd[:CHUNK] + y_bd[CHUNK:]


def _rwkv_kernel(p_ref, mu_ref, wwa_hi_ref, wwa_lo_ref, g2_hi_ref, g2_lo_ref, w0_ref, a0_ref,
                 kk_ref, ka_ref, rk_ref, lng_ref, lnb_ref, ones_ref, ltri_ref,
                 o_ref, h_ref, prev_ref, *, d):
    tt = p_ref.shape[0]

    @pl.when(pl.program_id(1) == 0)
    def _():
        h_ref[...] = jnp.zeros_like(h_ref)
        prev_ref[...] = jnp.zeros_like(prev_ref)

    p = p_ref[...]
    row = lax.broadcasted_iota(jnp.int32, p.shape, 0)
    last = prev_ref[SUBLANES - 1:SUBLANES, :]
    prev = jnp.where(row == 0, jnp.broadcast_to(last, p.shape), pltpu.roll(p, 1, 0))
    prev_ref[...] = p[tt - SUBLANES:, :]
    pm = p + (prev - p) * mu_ref[...]

    r = pm[:, :d]
    k = pm[:, d:2 * d]
    v = pm[:, 2 * d:3 * d]
    xg = pm[:, 3 * d:3 * d + GATE_LORA]
    xwa = pm[:, 3 * d + GATE_LORA:]
    lane = lax.broadcasted_iota(jnp.int32, xwa.shape, 1)
    zwa = jnp.where(lane < DECAY_LORA, jnp.tanh(xwa), xwa)
    lora = _dot3(zwa, wwa_hi_ref[...], wwa_lo_ref[...])
    w = -jax.nn.softplus(-(w0_ref[...] + lora[:, :d])) - 0.5
    logw = -jnp.exp(w)
    a = jax.nn.sigmoid(a0_ref[...] + lora[:, d:])
    g = _dot3(jax.nn.sigmoid(xg), g2_hi_ref[...], g2_lo_ref[...])

    ones = ones_ref[...]

    def head_sum(x):
        hi, lo = _split2(x)
        return _dot(hi, ones) + _dot(lo, ones)

    kk = k * kk_ref[...]
    kk = kk / jnp.maximum(jnp.sqrt(head_sum(kk * kk)), L2_EPS)
    k2 = k * (1.0 + (a - 1.0) * ka_ref[...])
    av = -kk
    bv = kk * a

    ltri = ltri_ref[...]
    l_hi, l_lo, l_lo2 = _split3(logw)
    lc = _dot(ltri, l_hi) + _dot(ltri, l_lo) + _dot(ltri, l_lo2)

    items = []
    for c in range(tt // CHUNK):
        rows = slice(c * CHUNK, (c + 1) * CHUNK)
        lc_c = lc[rows]
        lw_c = logw[rows]
        lc_end = lc_c[CHUNK - 1:CHUNK]
        e_in = jnp.exp(lc_c - lw_c)
        e_inc = jnp.exp(lc_c)
        e_neg = jnp.exp(-lc_c)
        e_out = jnp.exp(lc_end - lc_c)
        ops = dict(at=av[rows] * e_in, rt=r[rows] * e_inc, bt=bv[rows] * e_neg,
                   kt=k2[rows] * e_neg, bh=bv[rows] * e_out, kh=k2[rows] * e_out, v=v[rows],
                   decay=jnp.exp(lc_end))
        for j in range(d // LANES):
            ln = slice(j * LANES, (j + 1) * LANES)
            item = {name: val[:, ln] for name, val in ops.items()}
            item["pair"] = j
            items.append(item)
    _rwkv_chunks(items, h_ref)
    n_pairs = d // LANES
    ys = [jnp.concatenate([it["y"] for it in items[c * n_pairs:(c + 1) * n_pairs]], axis=1)
          for c in range(tt // CHUNK)]
    y = jnp.concatenate(ys, axis=0)

    inv_n = 1.0 / RWKV_HEAD
    mean = head_sum(y) * inv_n
    yc = y - mean
    var = head_sum(yc * yc) * inv_n
    yn = yc * lax.rsqrt(var + GN_EPS) * lng_ref[...] + lnb_ref[...]
    bonus = head_sum(r * k2 * rk_ref[...]) * v
    o_ref[...] = (yn + bonus) * g


def _rwkv_branch(p2, bsz, mu, w0, w2, a0, a2, g2, k_k, k_a, r_k, lnx_g, lnx_b):
    t, d_in = p2.shape
    d = w0.shape[0]
    seq = t // bsz
    tt = min(TT_RWKV, seq)
    nt = seq // tt
    wwa = jnp.concatenate([jnp.concatenate([w2, jnp.zeros_like(w2)], axis=1),
                           jnp.concatenate([jnp.zeros_like(a2), a2], axis=1)], axis=0)
    wwa_hi, wwa_lo = _split2(wwa)
    g2_hi, g2_lo = _split2(g2)
    head = jnp.arange(d) // RWKV_HEAD
    ones = (head[:, None] == head[None, :]).astype(BF16)
    tok_i = jnp.arange(tt)
    ltri = ((tok_i[:, None] // CHUNK == tok_i[None, :] // CHUNK)
            & (tok_i[:, None] >= tok_i[None, :])).astype(BF16)
    vec = lambda a: a.reshape(1, -1)
    tok = lambda b, i: (b * nt + i, 0)
    c2 = lambda b, i: (0, 0)
    consts = [vec(mu), wwa_hi, wwa_lo, g2_hi, g2_lo, vec(w0), vec(a0), vec(k_k), vec(k_a), vec(r_k),
              vec(lnx_g), vec(lnx_b), ones, ltri]
    return pl.pallas_call(
        functools.partial(_rwkv_kernel, d=d),
        grid=(bsz, nt),
        in_specs=[pl.BlockSpec((tt, d_in), tok)] + [pl.BlockSpec(c.shape, c2) for c in consts],
        out_specs=pl.BlockSpec((tt, d), tok),
        out_shape=jax.ShapeDtypeStruct((t, d), F32),
        scratch_shapes=[pltpu.VMEM((d // LANES, LANES, LANES), F32),
                        pltpu.VMEM((SUBLANES, d_in), F32)],
        compiler_params=pltpu.CompilerParams(dimension_semantics=("arbitrary", "arbitrary"),
                                             vmem_limit_bytes=VMEM_LIMIT),
        name="rwkv",
    )(p2, *consts)


def _tail_kernel(x_ref, oa_ref, ob_ref, g1_ref, wg_ref, wba_ref, wbb_ref, wout_ref, g2_ref,
                 w1_ref, w2_ref, gf_ref, o_ref, *, ff_block):
    x = x_ref[...]
    d = x.shape[1]
    h = _rmsnorm(x, g1_ref[...]).astype(BF16)
    gates = jax.nn.sigmoid(_dot(h, wg_ref[...]))
    ma = _dot(oa_ref[...].astype(BF16), wba_ref[...])
    mb = _dot(ob_ref[...].astype(BF16), wbb_ref[...])
    merged = gates[:, :d] * ma + gates[:, d:] * mb
    x1 = x + _dot(merged.astype(BF16), wout_ref[...])
    h2 = _rmsnorm(x1, g2_ref[...]).astype(BF16)
    acc = x1
    for j in range(w1_ref.shape[1] // ff_block):
        cols = slice(j * ff_block, (j + 1) * ff_block)
        hid = jnp.square(jnp.maximum(_dot(h2, w1_ref[:, cols]), 0.0))
        acc = acc + _dot(hid.astype(BF16), w2_ref[cols, :])
    o_ref[...] = _rmsnorm(acc, gf_ref[...])


def _tail(x2, o_a, o_b, norm1_g, w_gate, wb_a, wb_b, w_out, norm2_g, w_ff1, w_ff2, norm_f_g):
    t, d = x2.shape
    d_mix = o_a.shape[1]
    tm = min(TM_PROJ, t)
    tok = lambda i: (i, 0)
    const = lambda i: (0, 0)
    resident = lambda a: pl.BlockSpec(a.shape, const, pipeline_mode=pl.Buffered(1))
    vec = lambda a: a.reshape(1, d)
    return pl.pallas_call(
        functools.partial(_tail_kernel, ff_block=d),
        grid=(t // tm,),
        in_specs=[pl.BlockSpec((tm, d), tok), pl.BlockSpec((tm, d_mix), tok),
                  pl.BlockSpec((tm, d_mix), tok), pl.BlockSpec((1, d), const),
                  resident(w_gate), resident(wb_a), resident(wb_b), resident(w_out),
                  pl.BlockSpec((1, d), const), resident(w_ff1), resident(w_ff2),
                  pl.BlockSpec((1, d), const)],
        out_specs=pl.BlockSpec((tm, d), tok),
        out_shape=jax.ShapeDtypeStruct((t, d), F32),
        compiler_params=pltpu.CompilerParams(dimension_semantics=("arbitrary",),
                                             vmem_limit_bytes=VMEM_LIMIT),
        name="tail",
    )(x2, o_a, o_b, vec(norm1_g), w_gate, wb_a, wb_b, w_out, vec(norm2_g), w_ff1, w_ff2,
      vec(norm_f_g))


def _layer(x2, bsz, norm1_g, w_in, lam_re, lam_im, log_dt, b_re, b_im, c_re, c_im, d_skip, w_glu,
           b_glu, mu_rwkv, w0, w2, a0, a2, g2, k_k, k_a, r_k, lnx_g, lnx_b, w_branch, w_out,
           norm2_g, w_ff1, w_ff2, out_g):
    d_ssm = d_skip.shape[0]
    d_rwkv = w0.shape[0]
    d_mix_in = d_ssm + mu_rwkv.shape[0]
    n_grp, n_state, grp = b_re.shape

    apow_re, apow_im, sh_re, sh_im, bb_re, bb_im = _s5_prep(lam_re, lam_im, log_dt, b_re, b_im)
    to_blocks = lambda bb: jnp.swapaxes(bb.reshape(n_grp, n_state, grp), 1, 2)
    bmat = jnp.concatenate([_block_diag(to_blocks(bb_re)), _block_diag(to_blocks(bb_im))],
                           axis=1).astype(BF16)
    c_blocks = lambda c: _block_diag(jnp.swapaxes(c, 1, 2))
    cmat = jnp.concatenate([c_blocks(c_re), -c_blocks(c_im)], axis=0).astype(BF16)

    u2, p2 = _in_proj(x2, norm1_g, w_in[:, :d_mix_in].astype(BF16), d_ssm)
    o_a = _s5_branch(u2, bsz, bmat, cmat, sh_re, sh_im, apow_re, apow_im, d_skip,
                     w_glu.astype(BF16), b_glu)
    o_b = _rwkv_branch(p2, bsz, mu_rwkv, w0, w2, a0, a2, g2, k_k, k_a, r_k.reshape(-1), lnx_g, lnx_b)
    del d_rwkv
    return _tail(x2, o_a, o_b, norm1_g, w_in[:, d_mix_in:].astype(BF16),
                 w_branch[:d_ssm].astype(BF16), w_branch[d_ssm:].astype(BF16),
                 w_out.astype(BF16), norm2_g, w_ff1.astype(BF16), w_ff2.astype(BF16), out_g)


def kernel(x, norm1_g, w_in, lam_re, lam_im, log_dt, b_re, b_im, c_re, c_im, d_skip, w_glu, b_glu,
           mu_rwkv, w0, w2, a0, a2, g2, k_k, k_a, r_k, lnx_g, lnx_b, w_branch, w_out, norm2_g,
           w_ff1, w_ff2, norm_f_g):
    bsz, seq, d = x.shape
    depth = norm1_g.shape[0]
    assert depth == 1, "the fused tail applies the final norm, so exactly one layer is supported"
    x2 = x.reshape(bsz * seq, d)
    l = 0
    out = _layer(x2, bsz, norm1_g[l], w_in[l], lam_re[l], lam_im[l], log_dt[l], b_re[l], b_im[l],
                 c_re[l], c_im[l], d_skip[l], w_glu[l], b_glu[l], mu_rwkv[l], w0[l], w2[l], a0[l],
                 a2[l], g2[l], k_k[l], k_a[l], r_k[l], lnx_g[l], lnx_b[l], w_branch[l], w_out[l],
                 norm2_g[l], w_ff1[l], w_ff2[l], norm_f_g)
    return out.reshape(bsz, seq, d)
```

```python
import functools

import jax
import jax.numpy as jnp
from jax import lax
from jax.experimental import pallas as pl
from jax.experimental.pallas import tpu as pltpu

F32 = jnp.float32
BF16 = jnp.bfloat16

RMS_EPS = 1e-6
GN_EPS = 64e-5
L2_EPS = 1e-12
MIN_NEG_REAL = -1e-4

SSM_GROUP = 16
SSM_STATE = 64
RWKV_HEAD = 64
GATE_LORA = 128
DECAY_LORA = 64
AAA_LORA = 64
CHUNK = 64

SUBLANES = 8
LANES = 128
VMEM_LIMIT = 56 * 1024 * 1024

TM_PROJ = 512
TT_S5 = 256
TT_RWKV = 256
S5_LANE_CHUNK = 512


def _dot(a, b):
    return jnp.dot(a, b, preferred_element_type=F32)


def _dot_nt(a, b):
    return lax.dot_general(a, b, (((1,), (1,)), ((), ())), preferred_element_type=F32)


def _split2(x):
    hi = x.astype(BF16)
    lo = (x - hi.astype(F32)).astype(BF16)
    return hi, lo


def _split3(x):
    hi = x.astype(BF16)
    r1 = x - hi.astype(F32)
    lo = r1.astype(BF16)
    lo2 = (r1 - lo.astype(F32)).astype(BF16)
    return hi, lo, lo2


def _dot3(a, b_hi, b_lo):
    a_hi, a_lo = _split2(a)
    return _dot(a_hi, b_hi) + _dot(a_lo, b_hi) + _dot(a_hi, b_lo)


def _rmsnorm(x, g):
    ms = jnp.mean(x * x, axis=-1, keepdims=True)
    return x * lax.rsqrt(ms + RMS_EPS) * g


def _s5_prep_kernel(lre_ref, lim_ref, ldt_ref, lre_w_ref, lim_w_ref, ldt_w_ref, bre_ref, bim_ref,
                    apow_re_ref, apow_im_ref, sh_re_ref, sh_im_ref, bbre_ref, bbim_ref):
    def discretise(lre, lim, ldt):
        lr = jnp.minimum(lre, MIN_NEG_REAL)
        dt = jnp.exp(ldt)
        mag = jnp.exp(lr * dt)
        ab_re = mag * jnp.cos(lim * dt)
        ab_im = mag * jnp.sin(lim * dt)
        den = lr * lr + lim * lim
        xm1 = ab_re - 1.0
        q_re = (xm1 * lr + ab_im * lim) / den
        q_im = (ab_im * lr - xm1 * lim) / den
        return ab_re, ab_im, q_re, q_im

    a_re, a_im, _, _ = discretise(lre_ref[...], lim_ref[...], ldt_ref[...])
    pows = [(a_re, a_im)]
    for _ in range(SUBLANES - 1):
        pr, pi = pows[-1]
        pows.append((pr * a_re - pi * a_im, pr * a_im + pi * a_re))
    shape = apow_re_ref.shape
    row = lax.broadcasted_iota(jnp.int32, shape, 0)
    apow_re = jnp.zeros(shape, F32)
    apow_im = jnp.zeros(shape, F32)
    for i, (pr, pi) in enumerate(pows):
        apow_re = jnp.where(row == i, jnp.broadcast_to(pr, shape), apow_re)
        apow_im = jnp.where(row == i, jnp.broadcast_to(pi, shape), apow_im)
    apow_re_ref[...] = apow_re
    apow_im_ref[...] = apow_im
    for i, s in enumerate((1, 2, 4)):
        pr, pi = pows[s - 1]
        sh_re_ref[i] = jnp.where(row >= s, jnp.broadcast_to(pr, shape), 0.0)
        sh_im_ref[i] = jnp.where(row >= s, jnp.broadcast_to(pi, shape), 0.0)

    _, _, q_re, q_im = discretise(lre_w_ref[...], lim_w_ref[...], ldt_w_ref[...])
    br = bre_ref[...]
    bi = bim_ref[...]
    bbre_ref[...] = q_re * br - q_im * bi
    bbim_ref[...] = q_re * bi + q_im * br


def _s5_prep(lam_re, lam_im, log_dt, b_re, b_im):
    g, p = lam_re.shape
    m = b_re.shape[-1]
    n = g * p
    row = lambda a: a.reshape(1, n)
    wide = lambda a: jnp.repeat(a, m, axis=-1)
    ldt_gp = jnp.broadcast_to(log_dt[:, None], (g, p))
    out_shape = (
        jax.ShapeDtypeStruct((SUBLANES, n), F32), jax.ShapeDtypeStruct((SUBLANES, n), F32),
        jax.ShapeDtypeStruct((3, SUBLANES, n), F32), jax.ShapeDtypeStruct((3, SUBLANES, n), F32),
        jax.ShapeDtypeStruct((g, p * m), F32), jax.ShapeDtypeStruct((g, p * m), F32),
    )
    return pl.pallas_call(_s5_prep_kernel, out_shape=out_shape, name="s5_prep")(
        row(lam_re), row(lam_im), row(ldt_gp), wide(lam_re), wide(lam_im), wide(ldt_gp),
        b_re.reshape(g, p * m), b_im.reshape(g, p * m))


def _block_diag(blocks):
    g, r, c = blocks.shape
    eye = jnp.eye(g, dtype=bool)
    return jnp.where(eye[:, None, :, None], blocks[:, :, None, :], 0.0).reshape(g * r, g * c)


def _in_proj_kernel(x_ref, g_ref, w_ref, u_ref, p_ref, *, d_ssm):
    h = _rmsnorm(x_ref[...], g_ref[...]).astype(BF16)
    out = _dot(h, w_ref[...])
    u_ref[...] = out[:, :d_ssm]
    p_ref[...] = out[:, d_ssm:]


def _in_proj(x2, norm_g, w_mix, d_ssm):
    t, d = x2.shape
    n = w_mix.shape[1]
    tm = min(TM_PROJ, t)
    const = lambda i: (0, 0)
    return pl.pallas_call(
        functools.partial(_in_proj_kernel, d_ssm=d_ssm),
        grid=(t // tm,),
        in_specs=[pl.BlockSpec((tm, d), lambda i: (i, 0)),
                  pl.BlockSpec((1, d), const),
                  pl.BlockSpec((d, n), const)],
        out_specs=[pl.BlockSpec((tm, d_ssm), lambda i: (i, 0)),
                   pl.BlockSpec((tm, n - d_ssm), lambda i: (i, 0))],
        out_shape=[jax.ShapeDtypeStruct((t, d_ssm), F32), jax.ShapeDtypeStruct((t, n - d_ssm), F32)],
        compiler_params=pltpu.CompilerParams(dimension_semantics=("arbitrary",),
                                             vmem_limit_bytes=VMEM_LIMIT),
        name="in_proj",
    )(x2, norm_g.reshape(1, d), w_mix)


def _s5_kernel(u_ref, bmat_ref, cmat_ref, sh_re_ref, sh_im_ref, apow_re_ref, apow_im_ref,
               d_ref, wglu_ref, bglu_ref, o_ref, xs_ref, carry_ref, *, n_state):
    tt = u_ref.shape[0]

    @pl.when(pl.program_id(1) == 0)
    def _():
        carry_ref[...] = jnp.zeros_like(carry_ref)

    u = u_ref[...]
    xs_ref[...] = _dot(u.astype(BF16), bmat_ref[...])

    lw = S5_LANE_CHUNK
    for c in range(n_state // lw):
        re_sl = pl.ds(c * lw, lw)
        im_sl = pl.ds(n_state + c * lw, lw)

        def block(i, carry, re_sl=re_sl, im_sl=im_sl):
            cr, ci = carry
            rows = pl.ds(pl.multiple_of(i * SUBLANES, SUBLANES), SUBLANES)
            xr = xs_ref[rows, re_sl]
            xi = xs_ref[rows, im_sl]
            for k, s in enumerate((1, 2, 4)):
                ar = sh_re_ref[k, :, re_sl]
                ai = sh_im_ref[k, :, re_sl]
                sr = pltpu.roll(xr, s, 0)
                si = pltpu.roll(xi, s, 0)
                xr, xi = xr + (ar * sr - ai * si), xi + (ar * si + ai * sr)
            pr = apow_re_ref[:, re_sl]
            pi = apow_im_ref[:, re_sl]
            xr = xr + (pr * cr - pi * ci)
            xi = xi + (pr * ci + pi * cr)
            xs_ref[rows, re_sl] = xr
            xs_ref[rows, im_sl] = xi
            last = SUBLANES - 1
            return (jnp.broadcast_to(xr[last:last + 1], xr.shape),
                    jnp.broadcast_to(xi[last:last + 1], xi.shape))

        cr, ci = lax.fori_loop(0, tt // SUBLANES, block,
                               (carry_ref[:, re_sl], carry_ref[:, im_sl]), unroll=2)
        carry_ref[:, re_sl] = cr
        carry_ref[:, im_sl] = ci

    y = _dot(xs_ref[...].astype(BF16), cmat_ref[...]) + d_ref[...] * u
    z = jax.nn.gelu(y)
    gate = jax.nn.sigmoid(_dot(z.astype(BF16), wglu_ref[...]) + bglu_ref[...])
    o_ref[...] = z * gate


def _s5_branch(u2, bsz, bmat, cmat, sh_re, sh_im, apow_re, apow_im, d_skip, w_glu, b_glu):
    t, d_ssm = u2.shape
    seq = t // bsz
    n_state = apow_re.shape[1]
    tt = min(TT_S5, seq)
    nt = seq // tt
    tok = lambda b, i: (b * nt + i, 0)
    c2 = lambda b, i: (0, 0)
    c3 = lambda b, i: (0, 0, 0)
    return pl.pallas_call(
        functools.partial(_s5_kernel, n_state=n_state),
        grid=(bsz, nt),
        in_specs=[pl.BlockSpec((tt, d_ssm), tok),
                  pl.BlockSpec(bmat.shape, c2), pl.BlockSpec(cmat.shape, c2),
                  pl.BlockSpec(sh_re.shape, c3), pl.BlockSpec(sh_im.shape, c3),
                  pl.BlockSpec(apow_re.shape, c2), pl.BlockSpec(apow_im.shape, c2),
                  pl.BlockSpec((1, d_ssm), c2), pl.BlockSpec(w_glu.shape, c2),
                  pl.BlockSpec((1, d_ssm), c2)],
        out_specs=pl.BlockSpec((tt, d_ssm), tok),
        out_shape=jax.ShapeDtypeStruct((t, d_ssm), F32),
        scratch_shapes=[pltpu.VMEM((tt, 2 * n_state), F32),
                        pltpu.VMEM((SUBLANES, 2 * n_state), F32)],
        compiler_params=pltpu.CompilerParams(dimension_semantics=("arbitrary", "arbitrary"),
                                             vmem_limit_bytes=VMEM_LIMIT),
        name="s5",
    )(u2, bmat, cmat, sh_re, sh_im, apow_re, apow_im, d_skip.reshape(1, d_ssm), w_glu,
      b_glu.reshape(1, d_ssm))


def _pair_blocks(x):
    lane = lax.broadcasted_iota(jnp.int32, x.shape, 1)
    first = lane < RWKV_HEAD
    return jnp.concatenate([jnp.where(first, x, 0.0), jnp.where(first, 0.0, x)], axis=0)


def _pair_transpose(x):
    t = _pair_blocks(x).T
    return t[:CHUNK] + t[CHUNK:]


def _rwkv_chunks(items, h_ref):
    shape = (CHUNK, LANES)
    row = lax.broadcasted_iota(jnp.int32, shape, 0)
    col = lax.broadcasted_iota(jnp.int32, shape, 1) % RWKV_HEAD
    strict = row > col
    incl = row >= col
    eye = row == col
    blocks = lambda x: _pair_blocks(x).astype(BF16)

    for it in items:
        lhs = jnp.concatenate([it["at"], it["rt"]], axis=0).astype(BF16)
        rhs = jnp.concatenate([blocks(it["bt"]), blocks(it["kt"])], axis=0)
        sc = _dot_nt(lhs, rhs)
        it["s_ab"] = jnp.where(strict, sc[:CHUNK, :LANES], 0.0)
        it["s_ak"] = jnp.where(strict, sc[:CHUNK, LANES:], 0.0).astype(BF16)
        it["s_r"] = jnp.concatenate([jnp.where(incl, sc[CHUNK:, :LANES], 0.0),
                                     jnp.where(incl, sc[CHUNK:, LANES:], 0.0)], axis=1).astype(BF16)
        it["bd_v"] = blocks(it["v"])
        it["bk_t"] = jnp.concatenate([_pair_transpose(it["bh"]), _pair_transpose(it["kh"])],
                                     axis=1).astype(BF16)

    for it in items:
        it["npow"] = it["s_ab"]
        it["inv"] = jnp.where(eye, 1.0, 0.0) + it["s_ab"]
    for _ in range(5):
        for it in items:
            it["npow"] = _dot(it["npow"].astype(BF16), blocks(it["npow"]))
        for it in items:
            it["inv"] = it["inv"] + _dot(it["inv"].astype(BF16), blocks(it["npow"]))

    for it in items:
        it["av"] = _dot(it["s_ak"], it["bd_v"])
    for it in items:
        wu = _dot(it["inv"].astype(BF16),
                  jnp.concatenate([blocks(it["at"]), blocks(it["av"])], axis=1))
        it["z"] = jnp.concatenate(
            [jnp.concatenate([blocks(wu[:, :LANES]), blocks(wu[:, LANES:])], axis=1),
             jnp.concatenate([jnp.zeros((LANES, LANES), BF16), it["bd_v"]], axis=1)], axis=0)
    for it in items:
        qy = _dot(it["s_r"], it["z"])
        mg = _dot(it["bk_t"], it["z"])
        it["qh"] = (it["rt"] + qy[:, :LANES]).astype(BF16)
        it["yp"] = qy[:, LANES:]
        it["m"] = (jnp.where(eye, jnp.broadcast_to(it["decay"], shape), 0.0)
                   + mg[:, :LANES]).astype(BF16)
        it["g"] = mg[:, LANES:]

    for it in items:
        hb = blocks(h_ref[it["pair"]])
        it["y"] = _dot(it["qh"], hb) + it["yp"]
        h_ref[it["pair"]] = _dot(it["m"], hb) + it["g"]


def _rwkv_kernel(p_ref, mu_ref, wwa_ref, g2_ref, w0_ref, a0_ref,
                 kk_ref, ka_ref, rk_ref, lng_ref, lnb_ref, ones_ref, ltri_ref,
                 o_ref, h_ref, prev_ref, *, d):
    tt = p_ref.shape[0]

    @pl.when(pl.program_id(1) == 0)
    def _():
        h_ref[...] = jnp.zeros_like(h_ref)
        prev_ref[...] = jnp.zeros_like(prev_ref)

    p = p_ref[...]
    row = lax.broadcasted_iota(jnp.int32, p.shape, 0)
    last = prev_ref[SUBLANES - 1:SUBLANES, :]
    prev = jnp.where(row == 0, jnp.broadcast_to(last, p.shape), pltpu.roll(p, 1, 0))
    prev_ref[...] = p[tt - SUBLANES:, :]
    pm = p + (prev - p) * mu_ref[...]

    r = pm[:, :d]
    k = pm[:, d:2 * d]
    v = pm[:, 2 * d:3 * d]
    xg = pm[:, 3 * d:3 * d + GATE_LORA]
    xwa = pm[:, 3 * d + GATE_LORA:]
    lane = lax.broadcasted_iota(jnp.int32, xwa.shape, 1)
    zwa = jnp.where(lane < DECAY_LORA, jnp.tanh(xwa), xwa)
    lora = _dot(zwa.astype(BF16), wwa_ref[...])
    w = -jax.nn.softplus(-(w0_ref[...] + lora[:, :d])) - 0.5
    logw = -jnp.exp(w)
    a = jax.nn.sigmoid(a0_ref[...] + lora[:, d:])
    g = _dot(jax.nn.sigmoid(xg).astype(BF16), g2_ref[...])

    ones = ones_ref[...]
    n_pairs = d // LANES

    def head_sum(x):
        stacked = jnp.concatenate([x[:, j * LANES:(j + 1) * LANES] for j in range(n_pairs)], axis=0)
        sums = _dot(stacked.astype(BF16), ones)
        return jnp.concatenate([sums[j * tt:(j + 1) * tt] for j in range(n_pairs)], axis=1)

    kk = k * kk_ref[...]
    kk = kk / jnp.maximum(jnp.sqrt(head_sum(kk * kk)), L2_EPS)
    k2 = k * (1.0 + (a - 1.0) * ka_ref[...])
    av = -kk
    bv = kk * a

    ltri = ltri_ref[...]
    l_hi, l_lo = _split2(logw)
    lc = _dot(ltri, l_hi) + _dot(ltri, l_lo)

    items = []
    for c in range(tt // CHUNK):
        rows = slice(c * CHUNK, (c + 1) * CHUNK)
        lc_c = lc[rows]
        lw_c = logw[rows]
        lc_end = lc_c[CHUNK - 1:CHUNK]
        e_in = jnp.exp(lc_c - lw_c)
        e_inc = jnp.exp(lc_c)
        e_neg = jnp.exp(-lc_c)
        e_out = jnp.exp(lc_end - lc_c)
        ops = dict(at=av[rows] * e_in, rt=r[rows] * e_inc, bt=bv[rows] * e_neg,
                   kt=k2[rows] * e_neg, bh=bv[rows] * e_out, kh=k2[rows] * e_out, v=v[rows],
                   decay=jnp.exp(lc_end))
        for j in range(d // LANES):
            ln = slice(j * LANES, (j + 1) * LANES)
            item = {name: val[:, ln] for name, val in ops.items()}
            item["pair"] = j
            items.append(item)
    _rwkv_chunks(items, h_ref)
    ys = [jnp.concatenate([it["y"] for it in items[c * n_pairs:(c + 1) * n_pairs]], axis=1)
          for c in range(tt // CHUNK)]
    y = jnp.concatenate(ys, axis=0)

    inv_n = 1.0 / RWKV_HEAD
    mean = head_sum(y) * inv_n
    yc = y - mean
    var = head_sum(yc * yc) * inv_n
    yn = yc * lax.rsqrt(var + GN_EPS) * lng_ref[...] + lnb_ref[...]
    bonus = head_sum(r * k2 * rk_ref[...]) * v
    o_ref[...] = (yn + bonus) * g


def _rwkv_branch(p2, bsz, mu, w0, w2, a0, a2, g2, k_k, k_a, r_k, lnx_g, lnx_b):
    t, d_in = p2.shape
    d = w0.shape[0]
    seq = t // bsz
    tt = min(TT_RWKV, seq)
    nt = seq // tt
    wwa = jnp.concatenate([jnp.concatenate([w2, jnp.zeros_like(w2)], axis=1),
                           jnp.concatenate([jnp.zeros_like(a2), a2], axis=1)], axis=0)
    head = jnp.arange(LANES) // RWKV_HEAD
    ones = (head[:, None] == head[None, :]).astype(BF16)
    tok_i = jnp.arange(tt)
    ltri = ((tok_i[:, None] // CHUNK == tok_i[None, :] // CHUNK)
            & (tok_i[:, None] >= tok_i[None, :])).astype(BF16)
    vec = lambda a: a.reshape(1, -1)
    tok = lambda b, i: (b * nt + i, 0)
    c2 = lambda b, i: (0, 0)
    consts = [vec(mu), wwa.astype(BF16), g2.astype(BF16), vec(w0), vec(a0), vec(k_k), vec(k_a), vec(r_k),
              vec(lnx_g), vec(lnx_b), ones, ltri]
    return pl.pallas_call(
        functools.partial(_rwkv_kernel, d=d),
        grid=(bsz, nt),
        in_specs=[pl.BlockSpec((tt, d_in), tok)] + [pl.BlockSpec(c.shape, c2) for c in consts],
        out_specs=pl.BlockSpec((tt, d), tok),
        out_shape=jax.ShapeDtypeStruct((t, d), F32),
        scratch_shapes=[pltpu.VMEM((d // LANES, RWKV_HEAD, LANES), F32),
                        pltpu.VMEM((SUBLANES, d_in), F32)],
        compiler_params=pltpu.CompilerParams(dimension_semantics=("arbitrary", "arbitrary"),
                                             vmem_limit_bytes=VMEM_LIMIT),
        name="rwkv",
    )(p2, *consts)


def _tail_kernel(x_ref, oa_ref, ob_ref, g1_ref, wg_ref, wba_ref, wbb_ref, wout_ref, g2_ref,
                 w1_ref, w2_ref, gf_ref, o_ref, *, ff_block):
    x = x_ref[...]
    d = x.shape[1]
    h = _rmsnorm(x, g1_ref[...]).astype(BF16)
    gates = jax.nn.sigmoid(_dot(h, wg_ref[...]))
    ma = _dot(oa_ref[...].astype(BF16), wba_ref[...])
    mb = _dot(ob_ref[...].astype(BF16), wbb_ref[...])
    merged = gates[:, :d] * ma + gates[:, d:] * mb
    x1 = x + _dot(merged.astype(BF16), wout_ref[...])
    h2 = _rmsnorm(x1, g2_ref[...]).astype(BF16)
    acc = x1
    for j in range(w1_ref.shape[1] // ff_block):
        cols = slice(j * ff_block, (j + 1) * ff_block)
        hid = jnp.square(jnp.maximum(_dot(h2, w1_ref[:, cols]), 0.0))
        acc = acc + _dot(hid.astype(BF16), w2_ref[cols, :])
    o_ref[...] = _rmsnorm(acc, gf_ref[...])


def _tail(x2, o_a, o_b, norm1_g, w_gate, wb_a, wb_b, w_out, norm2_g, w_ff1, w_ff2, norm_f_g):
    t, d = x2.shape
    d_mix = o_a.shape[1]
    tm = min(TM_PROJ, t)
    tok = lambda i: (i, 0)
    const = lambda i: (0, 0)
    resident = lambda a: pl.BlockSpec(a.shape, const, pipeline_mode=pl.Buffered(1))
    vec = lambda a: a.reshape(1, d)
    return pl.pallas_call(
        functools.partial(_tail_kernel, ff_block=d),
        grid=(t // tm,),
        in_specs=[pl.BlockSpec((tm, d), tok), pl.BlockSpec((tm, d_mix), tok),
                  pl.BlockSpec((tm, d_mix), tok), pl.BlockSpec((1, d), const),
                  resident(w_gate), resident(wb_a), resident(wb_b), resident(w_out),
                  pl.BlockSpec((1, d), const), resident(w_ff1), resident(w_ff2),
                  pl.BlockSpec((1, d), const)],
        out_specs=pl.BlockSpec((tm, d), tok),
        out_shape=jax.ShapeDtypeStruct((t, d), F32),
        compiler_params=pltpu.CompilerParams(dimension_semantics=("arbitrary",),
                                             vmem_limit_bytes=VMEM_LIMIT),
        name="tail",
    )(x2, o_a, o_b, vec(norm1_g), w_gate, wb_a, wb_b, w_out, vec(norm2_g), w_ff1, w_ff2,
      vec(norm_f_g))


def _layer(x2, bsz, norm1_g, w_in, lam_re, lam_im, log_dt, b_re, b_im, c_re, c_im, d_skip, w_glu,
           b_glu, mu_rwkv, w0, w2, a0, a2, g2, k_k, k_a, r_k, lnx_g, lnx_b, w_branch, w_out,
           norm2_g, w_ff1, w_ff2, out_g):
    d_ssm = d_skip.shape[0]
    d_rwkv = w0.shape[0]
    d_mix_in = d_ssm + mu_rwkv.shape[0]
    n_grp, n_state, grp = b_re.shape

    apow_re, apow_im, sh_re, sh_im, bb_re, bb_im = _s5_prep(lam_re, lam_im, log_dt, b_re, b_im)
    to_blocks = lambda bb: jnp.swapaxes(bb.reshape(n_grp, n_state, grp), 1, 2)
    bmat = jnp.concatenate([_block_diag(to_blocks(bb_re)), _block_diag(to_blocks(bb_im))],
                           axis=1).astype(BF16)
    c_blocks = lambda c: _block_diag(jnp.swapaxes(c, 1, 2))
    cmat = jnp.concatenate([c_blocks(c_re), -c_blocks(c_im)], axis=0).astype(BF16)

    u2, p2 = _in_proj(x2, norm1_g, w_in[:, :d_mix_in].astype(BF16), d_ssm)
    o_a = _s5_branch(u2, bsz, bmat, cmat, sh_re, sh_im, apow_re, apow_im, d_skip,
                     w_glu.astype(BF16), b_glu)
    o_b = _rwkv_branch(p2, bsz, mu_rwkv, w0, w2, a0, a2, g2, k_k, k_a, r_k.reshape(-1), lnx_g, lnx_b)
    del d_rwkv
    return _tail(x2, o_a, o_b, norm1_g, w_in[:, d_mix_in:].astype(BF16),
                 w_branch[:d_ssm].astype(BF16), w_branch[d_ssm:].astype(BF16),
                 w_out.astype(BF16), norm2_g, w_ff1.astype(BF16), w_ff2.astype(BF16), out_g)


def kernel(x, norm1_g, w_in, lam_re, lam_im, log_dt, b_re, b_im, c_re, c_im, d_skip, w_glu, b_glu,
           mu_rwkv, w0, w2, a0, a2, g2, k_k, k_a, r_k, lnx_g, lnx_b, w_branch, w_out, norm2_g,
           w_ff1, w_ff2, norm_f_g):
    bsz, seq, d = x.shape
    depth = norm1_g.shape[0]
    assert depth == 1, "the fused tail applies the final norm, so exactly one layer is supported"
    x2 = x.reshape(bsz * seq, d)
    l = 0
    out = _layer(x2, bsz, norm1_g[l], w_in[l], lam_re[l], lam_im[l], log_dt[l], b_re[l], b_im[l],
                 c_re[l], c_im[l], d_skip[l], w_glu[l], b_glu[l], mu_rwkv[l], w0[l], w2[l], a0[l],
                 a2[l], g2[l], k_k[l], k_a[l], r_k[l], lnx_g[l], lnx_b[l], w_branch[l], w_out[l],
                 norm2_g[l], w_ff1[l], w_ff2[l], norm_f_g)
    return out.reshape(bsz, seq, d)
```

```python
import functools

import jax
import jax.numpy as jnp
from jax import lax
from jax.experimental import pallas as pl
from jax.experimental.pallas import tpu as pltpu

F32 = jnp.float32
BF16 = jnp.bfloat16

RMS_EPS = 1e-6
GN_EPS = 64e-5
L2_EPS = 1e-12
MIN_NEG_REAL = -1e-4
EXP_NEG_HALF = 0.6065306597126334

SSM_GROUP = 16
SSM_STATE = 64
RWKV_HEAD = 64
GATE_LORA = 128
DECAY_LORA = 64
AAA_LORA = 64
CHUNK = 64

SUBLANES = 8
LANES = 128
VMEM_LIMIT = 56 * 1024 * 1024

TM_PROJ = 512
TT_S5 = 256
TT_RWKV = 256
S5_LANE_CHUNK = 512


def _dot(a, b):
    return jnp.dot(a, b, preferred_element_type=F32)


def _dot_nt(a, b):
    return lax.dot_general(a, b, (((1,), (1,)), ((), ())), preferred_element_type=F32)


def _split2(x):
    hi = x.astype(BF16)
    lo = (x - hi.astype(F32)).astype(BF16)
    return hi, lo


def _split3(x):
    hi = x.astype(BF16)
    r1 = x - hi.astype(F32)
    lo = r1.astype(BF16)
    lo2 = (r1 - lo.astype(F32)).astype(BF16)
    return hi, lo, lo2


def _rmsnorm(x, g):
    ms = jnp.mean(x * x, axis=-1, keepdims=True)
    return x * lax.rsqrt(ms + RMS_EPS) * g


def _cmul(ar, ai, br, bi):
    return ar * br - ai * bi, ar * bi + ai * br


def _s5_prep_kernel(lre_ref, lim_ref, ldt_ref, lre_w_ref, lim_w_ref, ldt_w_ref, bre_ref, bim_ref,
                    a1_re_ref, a1_im_ref, a8_re_ref, a8_im_ref, apb_re_ref, apb_im_ref,
                    alp_re_ref, alp_im_ref, shl_re_ref, shl_im_ref, bbre_ref, bbim_ref, *, seg_len):
    def discretise(lre, lim, ldt):
        lr = jnp.minimum(lre, MIN_NEG_REAL)
        dt = jnp.exp(ldt)
        mag = jnp.exp(lr * dt)
        ab_re = mag * jnp.cos(lim * dt)
        ab_im = mag * jnp.sin(lim * dt)
        den = lr * lr + lim * lim
        xm1 = ab_re - 1.0
        q_re = (xm1 * lr + ab_im * lim) / den
        q_im = (ab_im * lr - xm1 * lim) / den
        return ab_re, ab_im, q_re, q_im

    shape = a1_re_ref.shape
    a_re, a_im, _, _ = discretise(lre_ref[...], lim_ref[...], ldt_ref[...])
    bcast = lambda x: jnp.broadcast_to(x, shape)
    pows = [(a_re, a_im)]
    for _ in range(SUBLANES - 1):
        pows.append(_cmul(*pows[-1], a_re, a_im))
    a1_re_ref[...] = bcast(a_re)
    a1_im_ref[...] = bcast(a_im)
    a8_re_ref[...] = bcast(pows[-1][0])
    a8_im_ref[...] = bcast(pows[-1][1])
    for b, (pr, pi) in enumerate(pows):
        apb_re_ref[b] = bcast(pr)
        apb_im_ref[b] = bcast(pi)
    lr_, li_ = pows[-1]
    n = SUBLANES
    while n < seg_len:
        lr_, li_ = _cmul(lr_, li_, lr_, li_)
        n *= 2
    lpows = [(lr_, li_)]
    for _ in range(SUBLANES - 1):
        lpows.append(_cmul(*lpows[-1], lr_, li_))
    row = lax.broadcasted_iota(jnp.int32, shape, 0)
    alp_re = jnp.zeros(shape, F32)
    alp_im = jnp.zeros(shape, F32)
    for i, (pr, pi) in enumerate(lpows):
        alp_re = jnp.where(row == i, bcast(pr), alp_re)
        alp_im = jnp.where(row == i, bcast(pi), alp_im)
    alp_re_ref[...] = alp_re
    alp_im_ref[...] = alp_im
    for i, s in enumerate((1, 2, 4)):
        pr, pi = lpows[s - 1]
        shl_re_ref[i] = jnp.where(row >= s, bcast(pr), 0.0)
        shl_im_ref[i] = jnp.where(row >= s, bcast(pi), 0.0)

    _, _, q_re, q_im = discretise(lre_w_ref[...], lim_w_ref[...], ldt_w_ref[...])
    br = bre_ref[...]
    bi = bim_ref[...]
    bbre_ref[...] = q_re * br - q_im * bi
    bbim_ref[...] = q_re * bi + q_im * br


def _s5_prep(lam_re, lam_im, log_dt, b_re, b_im, seg_len):
    g, p = lam_re.shape
    m = b_re.shape[-1]
    n = g * p
    row = lambda a: a.reshape(1, n)
    wide = lambda a: jnp.repeat(a, m, axis=-1)
    ldt_gp = jnp.broadcast_to(log_dt[:, None], (g, p))
    s8 = jax.ShapeDtypeStruct((SUBLANES, n), F32)
    s88 = jax.ShapeDtypeStruct((SUBLANES, SUBLANES, n), F32)
    s38 = jax.ShapeDtypeStruct((3, SUBLANES, n), F32)
    sw = jax.ShapeDtypeStruct((g, p * m), F32)
    out_shape = (s8, s8, s8, s8, s88, s88, s8, s8, s38, s38, sw, sw)
    return pl.pallas_call(functools.partial(_s5_prep_kernel, seg_len=seg_len),
                          out_shape=out_shape, name="s5_prep")(
        row(lam_re), row(lam_im), row(ldt_gp), wide(lam_re), wide(lam_im), wide(ldt_gp),
        b_re.reshape(g, p * m), b_im.reshape(g, p * m))


def _block_diag(blocks):
    g, r, c = blocks.shape
    eye = jnp.eye(g, dtype=bool)
    return jnp.where(eye[:, None, :, None], blocks[:, :, None, :], 0.0).reshape(g * r, g * c)


def _head_sum(x, ones):
    rows, width = x.shape
    groups = width // LANES
    stacked = jnp.concatenate([x[:, j * LANES:(j + 1) * LANES] for j in range(groups)], axis=0)
    sums = _dot(stacked.astype(BF16), ones)
    return jnp.concatenate([sums[j * rows:(j + 1) * rows] for j in range(groups)], axis=1)


def _in_proj_kernel(x_ref, g_ref, w_ref, mu_ref, wwa_ref, g2_ref, w0_ref, a0_ref, kk_ref, ka_ref,
                    rk_ref, ones_ref, u_ref, r_ref, k2_ref, v_ref, av_ref, bv_ref, logw_ref,
                    gate_ref, bonus_ref, prev_ref, *, d_ssm, d, tiles_per_seq):
    tm = x_ref.shape[0]

    @pl.when(pl.program_id(0) % tiles_per_seq == 0)
    def _():
        prev_ref[...] = jnp.zeros_like(prev_ref)

    h = _rmsnorm(x_ref[...], g_ref[...]).astype(BF16)

    def mixed(c0, c1):
        p = _dot(h, w_ref[:, d_ssm + c0:d_ssm + c1])
        row = lax.broadcasted_iota(jnp.int32, p.shape, 0)
        last = prev_ref[SUBLANES - 1:SUBLANES, c0:c1]
        prev = jnp.where(row == 0, jnp.broadcast_to(last, p.shape), pltpu.roll(p, 1, 0))
        prev_ref[:, c0:c1] = p[tm - SUBLANES:, :]
        return p + (prev - p) * mu_ref[:, c0:c1]

    lora_in = mixed(3 * d, 3 * d + GATE_LORA + DECAY_LORA + AAA_LORA)
    u_ref[...] = _dot(h, w_ref[:, :d_ssm])
    k = mixed(d, 2 * d)
    xg = lora_in[:, :GATE_LORA]
    xwa = lora_in[:, GATE_LORA:]
    lane = lax.broadcasted_iota(jnp.int32, xwa.shape, 1)
    zwa = jnp.where(lane < DECAY_LORA, jnp.tanh(xwa), xwa)
    lora = _dot(zwa.astype(BF16), wwa_ref[...])
    gate_ref[...] = _dot(jax.nn.sigmoid(xg).astype(BF16), g2_ref[...])
    r = mixed(0, d)
    logw_ref[...] = -EXP_NEG_HALF * jax.nn.sigmoid(w0_ref[...] + lora[:, :d])
    a = jax.nn.sigmoid(a0_ref[...] + lora[:, d:])

    ones = ones_ref[...]
    kk = k * kk_ref[...]
    kk = kk * lax.rsqrt(jnp.maximum(_head_sum(kk * kk, ones), L2_EPS * L2_EPS))
    v = mixed(2 * d, 3 * d)
    k2 = k * (1.0 + (a - 1.0) * ka_ref[...])
    k2_ref[...] = k2.astype(BF16)
    av_ref[...] = (-kk).astype(BF16)
    bv_ref[...] = (kk * a).astype(BF16)
    r_ref[...] = r.astype(BF16)
    v_ref[...] = v.astype(BF16)
    bonus_ref[...] = _head_sum(r * k2 * rk_ref[...], ones) * v


def _in_proj(x2, seq, norm_g, w_mix, d_ssm, mu, w0, w2, a0, a2, g2, k_k, k_a, r_k):
    t, dm = x2.shape
    n = w_mix.shape[1]
    d = w0.shape[0]
    tm = min(TM_PROJ, seq)
    wwa = jnp.concatenate([jnp.concatenate([w2, jnp.zeros_like(w2)], axis=1),
                           jnp.concatenate([jnp.zeros_like(a2), a2], axis=1)], axis=0)
    head = jnp.arange(LANES) // RWKV_HEAD
    ones = (head[:, None] == head[None, :]).astype(BF16)
    vec = lambda a: a.reshape(1, -1)
    consts = [vec(norm_g), w_mix, vec(mu), wwa.astype(BF16), g2.astype(BF16), vec(w0), vec(a0),
              vec(k_k), vec(k_a), vec(r_k), ones]
    const = lambda i: (0, 0)
    tok = lambda i: (i, 0)
    widths = [(d_ssm, F32)] + [(d, BF16)] * 5 + [(d, F32)] * 3
    return pl.pallas_call(
        functools.partial(_in_proj_kernel, d_ssm=d_ssm, d=d, tiles_per_seq=seq // tm),
        grid=(t // tm,),
        in_specs=[pl.BlockSpec((tm, dm), tok)] + [pl.BlockSpec(c.shape, const) for c in consts],
        out_specs=[pl.BlockSpec((tm, w), tok) for w, _ in widths],
        out_shape=[jax.ShapeDtypeStruct((t, w), dt) for w, dt in widths],
        scratch_shapes=[pltpu.VMEM((SUBLANES, n - d_ssm), F32)],
        compiler_params=pltpu.CompilerParams(dimension_semantics=("arbitrary",),
                                             vmem_limit_bytes=VMEM_LIMIT),
        name="in_proj",
    )(x2, *consts)


def _s5_kernel(u_ref, perm_ref, permt_ref, bmat_ref, cmat_ref, a1_re_ref, a1_im_ref, a8_re_ref,
               a8_im_ref, apb_re_ref, apb_im_ref, alp_re_ref, alp_im_ref, shl_re_ref, shl_im_ref,
               d_ref, wglu_ref, bglu_ref, o_ref, xs_ref, carry_ref, *, n_state):
    tt = u_ref.shape[0]
    seg_len = tt // SUBLANES
    half = n_state // 2
    lw = S5_LANE_CHUNK
    blk = lambda j: pl.ds(j * SUBLANES, SUBLANES)

    @pl.when(pl.program_id(1) == 0)
    def _():
        carry_ref[...] = jnp.zeros_like(carry_ref)

    perm = perm_ref[...]
    u_hi, u_lo, u_lo2 = _split3(u_ref[...])
    up_hi = _dot(perm, u_hi)
    up = up_hi + _dot(perm, u_lo) + _dot(perm, u_lo2)
    ub = up_hi.astype(BF16)
    d_half = ub.shape[1] // 2
    for h in range(2):
        bu = _dot(ub[:, h * d_half:(h + 1) * d_half], bmat_ref[h])
        xs_ref[:, pl.ds(h * half, half)] = bu[:, :half]
        xs_ref[:, pl.ds(n_state + h * half, half)] = bu[:, half:]

    for c in range(n_state // lw):
        re_sl, im_sl = pl.ds(c * lw, lw), pl.ds(n_state + c * lw, lw)
        ar, ai = a1_re_ref[:, re_sl], a1_im_ref[:, re_sl]
        xr = jnp.zeros((SUBLANES, lw), F32)
        xi = jnp.zeros((SUBLANES, lw), F32)
        for j in range(seg_len):
            br = xs_ref[blk(j), re_sl]
            bi = xs_ref[blk(j), im_sl]
            xr, xi = ar * xr - ai * xi + br, ar * xi + ai * xr + bi
            xs_ref[blk(j), re_sl] = xr
            xs_ref[blk(j), im_sl] = xi
        er, ei = xr, xi
        for k, s in enumerate((1, 2, 4)):
            pr, pi = shl_re_ref[k, :, re_sl], shl_im_ref[k, :, re_sl]
            sr, si = pltpu.roll(er, s, 0), pltpu.roll(ei, s, 0)
            er, ei = er + (pr * sr - pi * si), ei + (pr * si + pi * sr)
        tr, ti = carry_ref[:, re_sl], carry_ref[:, im_sl]
        pr, pi = alp_re_ref[:, re_sl], alp_im_ref[:, re_sl]
        fr = er + (pr * tr - pi * ti)
        fi = ei + (pr * ti + pi * tr)
        row = lax.broadcasted_iota(jnp.int32, fr.shape, 0)
        dr = jnp.where(row == 0, tr, pltpu.roll(fr, 1, 0))
        di = jnp.where(row == 0, ti, pltpu.roll(fi, 1, 0))
        last = SUBLANES - 1
        carry_ref[:, re_sl] = jnp.broadcast_to(fr[last:last + 1], fr.shape)
        carry_ref[:, im_sl] = jnp.broadcast_to(fi[last:last + 1], fi.shape)
        a8r, a8i = a8_re_ref[:, re_sl], a8_im_ref[:, re_sl]
        for a in range(seg_len // SUBLANES):
            for b in range(SUBLANES):
                j = a * SUBLANES + b
                pr, pi = apb_re_ref[b, :, re_sl], apb_im_ref[b, :, re_sl]
                xs_ref[blk(j), re_sl] += pr * dr - pi * di
                xs_ref[blk(j), im_sl] += pr * di + pi * dr
            dr, di = a8r * dr - a8i * di, a8r * di + a8i * dr

    ys = []
    for h in range(2):
        lhs = jnp.concatenate([xs_ref[:, pl.ds(h * half, half)],
                               xs_ref[:, pl.ds(n_state + h * half, half)]], axis=1)
        ys.append(_dot(lhs.astype(BF16), cmat_ref[h]))
    y = jnp.concatenate(ys, axis=1) + d_ref[...] * up
    z = jax.nn.gelu(y)
    gate = jax.nn.sigmoid(_dot(z.astype(BF16), wglu_ref[...]) + bglu_ref[...])
    o_hi, o_lo, o_lo2 = _split3(z * gate)
    permt = permt_ref[...]
    o_ref[...] = _dot(permt, o_hi) + _dot(permt, o_lo) + _dot(permt, o_lo2)


def _s5_branch(u2, bsz, tt, bmat, cmat, scan_consts, d_skip, w_glu, b_glu):
    t, d_ssm = u2.shape
    seq = t // bsz
    n_state = scan_consts[0].shape[1]
    nt = seq // tt
    seg_len = tt // SUBLANES
    rho = jnp.arange(tt)
    src = (rho % SUBLANES) * seg_len + rho // SUBLANES
    perm = (src[:, None] == jnp.arange(tt)[None, :]).astype(BF16)
    tok = lambda b, i: (b * nt + i, 0)
    whole = lambda a: pl.BlockSpec(a.shape, lambda b, i: (0,) * a.ndim)
    consts = [perm, perm.T, bmat, cmat, *scan_consts, d_skip.reshape(1, d_ssm), w_glu,
              b_glu.reshape(1, d_ssm)]
    return pl.pallas_call(
        functools.partial(_s5_kernel, n_state=n_state),
        grid=(bsz, nt),
        in_specs=[pl.BlockSpec((tt, d_ssm), tok)] + [whole(c) for c in consts],
        out_specs=pl.BlockSpec((tt, d_ssm), tok),
        out_shape=jax.ShapeDtypeStruct((t, d_ssm), F32),
        scratch_shapes=[pltpu.VMEM((tt, 2 * n_state), F32),
                        pltpu.VMEM((SUBLANES, 2 * n_state), F32)],
        compiler_params=pltpu.CompilerParams(dimension_semantics=("arbitrary", "arbitrary"),
                                             vmem_limit_bytes=VMEM_LIMIT),
        name="s5",
    )(u2, *consts)


def _s5_mixer(u2, bsz, lam_re, lam_im, log_dt, b_re, b_im, c_re, c_im, d_skip, w_glu, b_glu):
    d_ssm = d_skip.shape[0]
    n_grp, n_state, grp = b_re.shape
    seq = u2.shape[0] // bsz
    tt_s5 = min(TT_S5, seq)
    *scan_consts, bb_re, bb_im = _s5_prep(lam_re, lam_im, log_dt, b_re, b_im, tt_s5 // SUBLANES)
    to_blocks = lambda bb: jnp.swapaxes(bb.reshape(n_grp, n_state, grp), 1, 2)
    b_full = [_block_diag(to_blocks(bb)) for bb in (bb_re, bb_im)]
    c_full = [_block_diag(jnp.swapaxes(c, 1, 2)) for c in (c_re, -c_im)]
    hc, hs = d_ssm // 2, n_grp * n_state // 2
    bmat = jnp.stack([jnp.concatenate([m[h * hc:(h + 1) * hc, h * hs:(h + 1) * hs] for m in b_full],
                                      axis=1) for h in range(2)]).astype(BF16)
    cmat = jnp.stack([jnp.concatenate([m[h * hs:(h + 1) * hs, h * hc:(h + 1) * hc] for m in c_full],
                                      axis=0) for h in range(2)]).astype(BF16)
    return _s5_branch(u2, bsz, tt_s5, bmat, cmat, scan_consts, d_skip, w_glu.astype(BF16), b_glu)


def _pair_blocks(x):
    lane = lax.broadcasted_iota(jnp.int32, x.shape, 1)
    first = lane < RWKV_HEAD
    return jnp.concatenate([jnp.where(first, x, 0.0), jnp.where(first, 0.0, x)], axis=0)


def _pair_transpose(x):
    t = _pair_blocks(x).T
    return t[:CHUNK] + t[CHUNK:]


def _rwkv_chunks(items, h_ref):
    shape = (CHUNK, LANES)
    row = lax.broadcasted_iota(jnp.int32, shape, 0)
    col = lax.broadcasted_iota(jnp.int32, shape, 1) % RWKV_HEAD
    strict = row > col
    incl = row >= col
    eye = row == col
    blocks = lambda x: _pair_blocks(x).astype(BF16)

    for it in items:
        lhs = jnp.concatenate([it["at"], it["rt"]], axis=0).astype(BF16)
        rhs = jnp.concatenate([blocks(it["bt"]), blocks(it["kt"])], axis=0)
        sc = _dot_nt(lhs, rhs)
        it["s_ab"] = jnp.where(strict, sc[:CHUNK, :LANES], 0.0)
        it["s_ak"] = jnp.where(strict, sc[:CHUNK, LANES:], 0.0).astype(BF16)
        it["s_r"] = jnp.concatenate([jnp.where(incl, sc[CHUNK:, :LANES], 0.0),
                                     jnp.where(incl, sc[CHUNK:, LANES:], 0.0)], axis=1).astype(BF16)
        it["bd_v"] = blocks(it["v"])
        it["bk_t"] = jnp.concatenate([_pair_transpose(it["bh"]), _pair_transpose(it["kh"])],
                                     axis=1).astype(BF16)

    for it in items:
        it["npow"] = it["s_ab"]
        it["inv"] = jnp.where(eye, 1.0, 0.0) + it["s_ab"]
    for _ in range(5):
        for it in items:
            it["npow"] = _dot(it["npow"].astype(BF16), blocks(it["npow"]))
        for it in items:
            it["inv"] = it["inv"] + _dot(it["inv"].astype(BF16), blocks(it["npow"]))

    for it in items:
        it["av"] = _dot(it["s_ak"], it["bd_v"])
    for it in items:
        wu = _dot(it["inv"].astype(BF16),
                  jnp.concatenate([blocks(it["at"]), blocks(it["av"])], axis=1))
        it["z"] = jnp.concatenate(
            [jnp.concatenate([blocks(wu[:, :LANES]), blocks(wu[:, LANES:])], axis=1),
             jnp.concatenate([jnp.zeros((LANES, LANES), BF16), it["bd_v"]], axis=1)], axis=0)
    for it in items:
        qy = _dot(it["s_r"], it["z"])
        mg = _dot(it["bk_t"], it["z"])
        it["qh"] = (it["rt"] + qy[:, :LANES]).astype(BF16)
        it["yp"] = qy[:, LANES:]
        it["m"] = (jnp.where(eye, jnp.broadcast_to(it["decay"], shape), 0.0)
                   + mg[:, :LANES]).astype(BF16)
        it["g"] = mg[:, LANES:]

    for it in items:
        hb = blocks(h_ref[it["pair"]])
        it["y"] = _dot(it["qh"], hb) + it["yp"]
        h_ref[it["pair"]] = _dot(it["m"], hb) + it["g"]


def _rwkv_kernel(r_ref, k2_ref, v_ref, av_ref, bv_ref, logw_ref, gate_ref, bonus_ref, lng_ref,
                 lnb_ref, ones_ref, ltri_ref, o_ref, h_ref):
    tt, d = r_ref.shape
    n_pairs = d // LANES

    @pl.when(pl.program_id(1) == 0)
    def _():
        h_ref[...] = jnp.zeros_like(h_ref)

    items = []
    for c in range(tt // CHUNK):
        rows = pl.ds(c * CHUNK, CHUNK)
        load = lambda ref: ref[rows, :].astype(F32)
        r, k2, v, av, bv, logw = map(load, (r_ref, k2_ref, v_ref, av_ref, bv_ref, logw_ref))
        l_hi, l_lo = _split2(logw)
        lc = _dot(ltri_ref[...], l_hi) + _dot(ltri_ref[...], l_lo)
        lc_end = lc[CHUNK - 1:CHUNK]
        e_in = jnp.exp(lc - logw)
        e_inc = jnp.exp(lc)
        e_neg = jnp.exp(-lc)
        e_out = jnp.exp(lc_end - lc)
        ops = dict(at=av * e_in, rt=r * e_inc, bt=bv * e_neg, kt=k2 * e_neg, bh=bv * e_out,
                   kh=k2 * e_out, v=v, decay=jnp.exp(lc_end))
        for j in range(n_pairs):
            ln = slice(j * LANES, (j + 1) * LANES)
            item = {name: val[:, ln] for name, val in ops.items()}
            item["pair"] = j
            items.append(item)

    _rwkv_chunks(items, h_ref)

    ones = ones_ref[...]
    inv_n = 1.0 / RWKV_HEAD
    for c in range(tt // CHUNK):
        rows = pl.ds(c * CHUNK, CHUNK)
        y = jnp.concatenate([it["y"] for it in items[c * n_pairs:(c + 1) * n_pairs]], axis=1)
        mean = _head_sum(y, ones) * inv_n
        yc = y - mean
        var = _head_sum(yc * yc, ones) * inv_n
        yn = yc * lax.rsqrt(var + GN_EPS) * lng_ref[...] + lnb_ref[...]
        o_ref[rows, :] = (yn + bonus_ref[rows, :]) * gate_ref[rows, :]


def _rwkv_branch(mix, bsz, lnx_g, lnx_b):
    t, d = mix[0].shape
    seq = t // bsz
    tt = min(TT_RWKV, seq)
    nt = seq // tt
    head = jnp.arange(LANES) // RWKV_HEAD
    ones = (head[:, None] == head[None, :]).astype(BF16)
    tok_i = jnp.arange(CHUNK)
    ltri = (tok_i[:, None] >= tok_i[None, :]).astype(BF16)
    tok = lambda b, i: (b * nt + i, 0)
    c2 = lambda b, i: (0, 0)
    consts = [lnx_g.reshape(1, d), lnx_b.reshape(1, d), ones, ltri]
    return pl.pallas_call(
        _rwkv_kernel,
        grid=(bsz, nt),
        in_specs=[pl.BlockSpec((tt, d), tok) for _ in mix] + [pl.BlockSpec(c.shape, c2) for c in consts],
        out_specs=pl.BlockSpec((tt, d), tok),
        out_shape=jax.ShapeDtypeStruct((t, d), F32),
        scratch_shapes=[pltpu.VMEM((d // LANES, RWKV_HEAD, LANES), F32)],
        compiler_params=pltpu.CompilerParams(dimension_semantics=("arbitrary", "arbitrary"),
                                             vmem_limit_bytes=VMEM_LIMIT),
        name="rwkv",
    )(*mix, *consts)


def _tail_kernel(x_ref, oa_ref, ob_ref, g1_ref, wg_ref, wba_ref, wbb_ref, wout_ref, g2_ref,
                 w1_ref, w2_ref, gf_ref, o_ref, *, ff_block):
    x = x_ref[...]
    d = x.shape[1]
    h = _rmsnorm(x, g1_ref[...]).astype(BF16)
    gates = jax.nn.sigmoid(_dot(h, wg_ref[...]))
    ma = _dot(oa_ref[...].astype(BF16), wba_ref[...])
    mb = _dot(ob_ref[...].astype(BF16), wbb_ref[...])
    merged = gates[:, :d] * ma + gates[:, d:] * mb
    x1 = x + _dot(merged.astype(BF16), wout_ref[...])
    h2 = _rmsnorm(x1, g2_ref[...]).astype(BF16)
    acc = x1
    for j in range(w1_ref.shape[1] // ff_block):
        cols = slice(j * ff_block, (j + 1) * ff_block)
        hid = jnp.square(jnp.maximum(_dot(h2, w1_ref[:, cols]), 0.0))
        acc = acc + _dot(hid.astype(BF16), w2_ref[cols, :])
    o_ref[...] = _rmsnorm(acc, gf_ref[...])


def _tail(x2, o_a, o_b, norm1_g, w_gate, wb_a, wb_b, w_out, norm2_g, w_ff1, w_ff2, norm_f_g):
    t, d = x2.shape
    d_mix = o_a.shape[1]
    tm = min(TM_PROJ, t)
    tok = lambda i: (i, 0)
    const = lambda i: (0, 0)
    resident = lambda a: pl.BlockSpec(a.shape, const, pipeline_mode=pl.Buffered(1))
    vec = lambda a: a.reshape(1, d)
    return pl.pallas_call(
        functools.partial(_tail_kernel, ff_block=d),
        grid=(t // tm,),
        in_specs=[pl.BlockSpec((tm, d), tok), pl.BlockSpec((tm, d_mix), tok),
                  pl.BlockSpec((tm, d_mix), tok), pl.BlockSpec((1, d), const),
                  resident(w_gate), resident(wb_a), resident(wb_b), resident(w_out),
                  pl.BlockSpec((1, d), const), resident(w_ff1), resident(w_ff2),
                  pl.BlockSpec((1, d), const)],
        out_specs=pl.BlockSpec((tm, d), tok),
        out_shape=jax.ShapeDtypeStruct((t, d), F32),
        compiler_params=pltpu.CompilerParams(dimension_semantics=("arbitrary",),
                                             vmem_limit_bytes=VMEM_LIMIT),
        name="tail",
    )(x2, o_a, o_b, vec(norm1_g), w_gate, wb_a, wb_b, w_out, vec(norm2_g), w_ff1, w_ff2,
      vec(norm_f_g))


def _layer(x2, bsz, norm1_g, w_in, lam_re, lam_im, log_dt, b_re, b_im, c_re, c_im, d_skip, w_glu,
           b_glu, mu_rwkv, w0, w2, a0, a2, g2, k_k, k_a, r_k, lnx_g, lnx_b, w_branch, w_out,
           norm2_g, w_ff1, w_ff2, out_g):
    d_ssm = d_skip.shape[0]
    d_mix_in = d_ssm + mu_rwkv.shape[0]
    u2, *mix = _in_proj(x2, x2.shape[0] // bsz, norm1_g, w_in[:, :d_mix_in].astype(BF16), d_ssm,
                        mu_rwkv, w0, w2, a0, a2, g2, k_k, k_a, r_k.reshape(-1))
    o_a = _s5_mixer(u2, bsz, lam_re, lam_im, log_dt, b_re, b_im, c_re, c_im, d_skip, w_glu, b_glu)
    o_b = _rwkv_branch(mix, bsz, lnx_g, lnx_b)
    return _tail(x2, o_a, o_b, norm1_g, w_in[:, d_mix_in:].astype(BF16),
                 w_branch[:d_ssm].astype(BF16), w_branch[d_ssm:].astype(BF16),
                 w_out.astype(BF16), norm2_g, w_ff1.astype(BF16), w_ff2.astype(BF16), out_g)


def kernel(x, norm1_g, w_in, lam_re, lam_im, log_dt, b_re, b_im, c_re, c_im, d_skip, w_glu, b_glu,
           mu_rwkv, w0, w2, a0, a2, g2, k_k, k_a, r_k, lnx_g, lnx_b, w_branch, w_out, norm2_g,
           w_ff1, w_ff2, norm_f_g):
    bsz, seq, d = x.shape
    depth = norm1_g.shape[0]
    assert depth == 1, "the fused tail applies the final norm, so exactly one layer is supported"
    x2 = x.reshape(bsz * seq, d)
    l = 0
    out = _layer(x2, bsz, norm1_g[l], w_in[l], lam_re[l], lam_im[l], log_dt[l], b_re[l], b_im[l],
                 c_re[l], c_im[l], d_skip[l], w_glu[l], b_glu[l], mu_rwkv[l], w0[l], w2[l], a0[l],
                 a2[l], g2[l], k_k[l], k_a[l], r_k[l], lnx_g[l], lnx_b[l], w_branch[l], w_out[l],
                 norm2_g[l], w_ff1[l], w_ff2[l], norm_f_g)
    return out.reshape(bsz, seq, d)
```

```python
import functools

import jax
import jax.numpy as jnp
from jax import lax
from jax.experimental import pallas as pl
from jax.experimental.pallas import tpu as pltpu

F32 = jnp.float32
BF16 = jnp.bfloat16

RMS_EPS = 1e-6
GN_EPS = 64e-5
L2_EPS = 1e-12
MIN_NEG_REAL = -1e-4
EXP_NEG_HALF = 0.6065306597126334

SSM_GROUP = 16
SSM_STATE = 64
RWKV_HEAD = 64
GATE_LORA = 128
DECAY_LORA = 64
AAA_LORA = 64
CHUNK = 64

SUBLANES = 8
LANES = 128
VMEM_LIMIT = 56 * 1024 * 1024

TM_PROJ = 512
TT_S5 = 256
TT_RWKV = 256
S5_LANE_CHUNK = 512


def _dot(a, b):
    return jnp.dot(a, b, preferred_element_type=F32)


def _dot_nt(a, b):
    return lax.dot_general(a, b, (((1,), (1,)), ((), ())), preferred_element_type=F32)


def _split2(x):
    hi = x.astype(BF16)
    lo = (x - hi.astype(F32)).astype(BF16)
    return hi, lo


def _split3(x):
    hi = x.astype(BF16)
    r1 = x - hi.astype(F32)
    lo = r1.astype(BF16)
    lo2 = (r1 - lo.astype(F32)).astype(BF16)
    return hi, lo, lo2


def _rmsnorm(x, g):
    ms = jnp.mean(x * x, axis=-1, keepdims=True)
    return x * lax.rsqrt(ms + RMS_EPS) * g


def _cmul(ar, ai, br, bi):
    return ar * br - ai * bi, ar * bi + ai * br


def _s5_prep_kernel(lre_ref, lim_ref, ldt_ref, lre_w_ref, lim_w_ref, ldt_w_ref, bre_ref, bim_ref,
                    a1_re_ref, a1_im_ref, a8_re_ref, a8_im_ref, apb_re_ref, apb_im_ref,
                    alp_re_ref, alp_im_ref, shl_re_ref, shl_im_ref, bbre_ref, bbim_ref, *, seg_len):
    def discretise(lre, lim, ldt):
        lr = jnp.minimum(lre, MIN_NEG_REAL)
        dt = jnp.exp(ldt)
        mag = jnp.exp(lr * dt)
        ab_re = mag * jnp.cos(lim * dt)
        ab_im = mag * jnp.sin(lim * dt)
        den = lr * lr + lim * lim
        xm1 = ab_re - 1.0
        q_re = (xm1 * lr + ab_im * lim) / den
        q_im = (ab_im * lr - xm1 * lim) / den
        return ab_re, ab_im, q_re, q_im

    shape = a1_re_ref.shape
    a_re, a_im, _, _ = discretise(lre_ref[...], lim_ref[...], ldt_ref[...])
    bcast = lambda x: jnp.broadcast_to(x, shape)
    pows = [(a_re, a_im)]
    for _ in range(SUBLANES - 1):
        pows.append(_cmul(*pows[-1], a_re, a_im))
    a1_re_ref[...] = bcast(a_re)
    a1_im_ref[...] = bcast(a_im)
    a8_re_ref[...] = bcast(pows[-1][0])
    a8_im_ref[...] = bcast(pows[-1][1])
    for b, (pr, pi) in enumerate(pows):
        apb_re_ref[b] = bcast(pr)
        apb_im_ref[b] = bcast(pi)
    lr_, li_ = pows[-1]
    n = SUBLANES
    while n < seg_len:
        lr_, li_ = _cmul(lr_, li_, lr_, li_)
        n *= 2
    lpows = [(lr_, li_)]
    for _ in range(SUBLANES - 1):
        lpows.append(_cmul(*lpows[-1], lr_, li_))
    row = lax.broadcasted_iota(jnp.int32, shape, 0)
    alp_re = jnp.zeros(shape, F32)
    alp_im = jnp.zeros(shape, F32)
    for i, (pr, pi) in enumerate(lpows):
        alp_re = jnp.where(row == i, bcast(pr), alp_re)
        alp_im = jnp.where(row == i, bcast(pi), alp_im)
    alp_re_ref[...] = alp_re
    alp_im_ref[...] = alp_im
    for i, s in enumerate((1, 2, 4)):
        pr, pi = lpows[s - 1]
        shl_re_ref[i] = jnp.where(row >= s, bcast(pr), 0.0)
        shl_im_ref[i] = jnp.where(row >= s, bcast(pi), 0.0)

    _, _, q_re, q_im = discretise(lre_w_ref[...], lim_w_ref[...], ldt_w_ref[...])
    br = bre_ref[...]
    bi = bim_ref[...]
    bbre_ref[...] = q_re * br - q_im * bi
    bbim_ref[...] = q_re * bi + q_im * br


def _s5_prep(lam_re, lam_im, log_dt, b_re, b_im, seg_len):
    g, p = lam_re.shape
    m = b_re.shape[-1]
    n = g * p
    row = lambda a: a.reshape(1, n)
    wide = lambda a: jnp.repeat(a, m, axis=-1)
    ldt_gp = jnp.broadcast_to(log_dt[:, None], (g, p))
    s8 = jax.ShapeDtypeStruct((SUBLANES, n), F32)
    s88 = jax.ShapeDtypeStruct((SUBLANES, SUBLANES, n), F32)
    s38 = jax.ShapeDtypeStruct((3, SUBLANES, n), F32)
    sw = jax.ShapeDtypeStruct((g, p * m), F32)
    out_shape = (s8, s8, s8, s8, s88, s88, s8, s8, s38, s38, sw, sw)
    return pl.pallas_call(functools.partial(_s5_prep_kernel, seg_len=seg_len),
                          out_shape=out_shape, name="s5_prep")(
        row(lam_re), row(lam_im), row(ldt_gp), wide(lam_re), wide(lam_im), wide(ldt_gp),
        b_re.reshape(g, p * m), b_im.reshape(g, p * m))


def _block_diag(blocks):
    g, r, c = blocks.shape
    eye = jnp.eye(g, dtype=bool)
    return jnp.where(eye[:, None, :, None], blocks[:, :, None, :], 0.0).reshape(g * r, g * c)


def _head_sum(x, ones):
    rows, width = x.shape
    groups = width // LANES
    stacked = jnp.concatenate([x[:, j * LANES:(j + 1) * LANES] for j in range(groups)], axis=0)
    sums = _dot(stacked.astype(BF16), ones)
    return jnp.concatenate([sums[j * rows:(j + 1) * rows] for j in range(groups)], axis=1)


def _in_proj_kernel(x_ref, g_ref, w_ref, mu_ref, wwa_ref, g2_ref, w0_ref, a0_ref, kk_ref, ka_ref,
                    rk_ref, ones_ref, u_ref, r_ref, k2_ref, v_ref, av_ref, bv_ref, logw_ref,
                    gate_ref, bonus_ref, prev_ref, *, d_ssm, d, tiles_per_seq):
    tm = x_ref.shape[0]

    @pl.when(pl.program_id(0) % tiles_per_seq == 0)
    def _():
        prev_ref[...] = jnp.zeros_like(prev_ref)

    h = _rmsnorm(x_ref[...], g_ref[...]).astype(BF16)

    def mixed(c0, c1):
        p = _dot(h, w_ref[:, d_ssm + c0:d_ssm + c1])
        row = lax.broadcasted_iota(jnp.int32, p.shape, 0)
        last = prev_ref[SUBLANES - 1:SUBLANES, c0:c1]
        prev = jnp.where(row == 0, jnp.broadcast_to(last, p.shape), pltpu.roll(p, 1, 0))
        prev_ref[:, c0:c1] = p[tm - SUBLANES:, :]
        return p + (prev - p) * mu_ref[:, c0:c1]

    lora_in = mixed(3 * d, 3 * d + GATE_LORA + DECAY_LORA + AAA_LORA)
    u_ref[...] = _dot(h, w_ref[:, :d_ssm])
    k = mixed(d, 2 * d)
    xg = lora_in[:, :GATE_LORA]
    xwa = lora_in[:, GATE_LORA:]
    lane = lax.broadcasted_iota(jnp.int32, xwa.shape, 1)
    zwa = jnp.where(lane < DECAY_LORA, jnp.tanh(xwa), xwa)
    lora = _dot(zwa.astype(BF16), wwa_ref[...])
    gate_ref[...] = _dot(jax.nn.sigmoid(xg).astype(BF16), g2_ref[...])
    r = mixed(0, d)
    logw_ref[...] = -EXP_NEG_HALF * jax.nn.sigmoid(w0_ref[...] + lora[:, :d])
    a = jax.nn.sigmoid(a0_ref[...] + lora[:, d:])

    ones = ones_ref[...]
    kk = k * kk_ref[...]
    kk = kk * lax.rsqrt(jnp.maximum(_head_sum(kk * kk, ones), L2_EPS * L2_EPS))
    v = mixed(2 * d, 3 * d)
    k2 = k * (1.0 + (a - 1.0) * ka_ref[...])
    k2_ref[...] = k2.astype(BF16)
    av_ref[...] = (-kk).astype(BF16)
    bv_ref[...] = (kk * a).astype(BF16)
    r_ref[...] = r.astype(BF16)
    v_ref[...] = v.astype(BF16)
    bonus_ref[...] = _head_sum(r * k2 * rk_ref[...], ones) * v


def _in_proj(x2, seq, norm_g, w_mix, d_ssm, mu, w0, w2, a0, a2, g2, k_k, k_a, r_k):
    t, dm = x2.shape
    n = w_mix.shape[1]
    d = w0.shape[0]
    tm = min(TM_PROJ, seq)
    wwa = jnp.concatenate([jnp.concatenate([w2, jnp.zeros_like(w2)], axis=1),
                           jnp.concatenate([jnp.zeros_like(a2), a2], axis=1)], axis=0)
    head = jnp.arange(LANES) // RWKV_HEAD
    ones = (head[:, None] == head[None, :]).astype(BF16)
    vec = lambda a: a.reshape(1, -1)
    consts = [vec(norm_g), w_mix, vec(mu), wwa.astype(BF16), g2.astype(BF16), vec(w0), vec(a0),
              vec(k_k), vec(k_a), vec(r_k), ones]
    const = lambda i: (0, 0)
    tok = lambda i: (i, 0)
    widths = [(d_ssm, F32)] + [(d, BF16)] * 5 + [(d, F32)] * 3
    return pl.pallas_call(
        functools.partial(_in_proj_kernel, d_ssm=d_ssm, d=d, tiles_per_seq=seq // tm),
        grid=(t // tm,),
        in_specs=[pl.BlockSpec((tm, dm), tok)] + [pl.BlockSpec(c.shape, const) for c in consts],
        out_specs=[pl.BlockSpec((tm, w), tok) for w, _ in widths],
        out_shape=[jax.ShapeDtypeStruct((t, w), dt) for w, dt in widths],
        scratch_shapes=[pltpu.VMEM((SUBLANES, n - d_ssm), F32)],
        compiler_params=pltpu.CompilerParams(dimension_semantics=("arbitrary",),
                                             vmem_limit_bytes=VMEM_LIMIT),
        name="in_proj",
    )(x2, *consts)


def _s5_kernel(u_ref, perm_ref, permt_ref, bmat_ref, cmat_ref, a1_re_ref, a1_im_ref, a8_re_ref,
               a8_im_ref, apb_re_ref, apb_im_ref, alp_re_ref, alp_im_ref, shl_re_ref, shl_im_ref,
               d_ref, wglu_ref, bglu_ref, o_ref, xs_ref, carry_ref, *, n_state):
    tt = u_ref.shape[0]
    seg_len = tt // SUBLANES
    half = n_state // 2
    lw = S5_LANE_CHUNK
    blk = lambda j: pl.ds(j * SUBLANES, SUBLANES)

    @pl.when(pl.program_id(1) == 0)
    def _():
        carry_ref[...] = jnp.zeros_like(carry_ref)

    perm = perm_ref[...]
    u_hi, u_lo, u_lo2 = _split3(u_ref[...])
    up_hi = _dot(perm, u_hi)
    up = up_hi + _dot(perm, u_lo) + _dot(perm, u_lo2)
    ub = up_hi.astype(BF16)
    d_half = ub.shape[1] // 2
    for h in range(2):
        bu = _dot(ub[:, h * d_half:(h + 1) * d_half], bmat_ref[h])
        xs_ref[:, pl.ds(h * half, half)] = bu[:, :half]
        xs_ref[:, pl.ds(n_state + h * half, half)] = bu[:, half:]

    for c in range(n_state // lw):
        re_sl, im_sl = pl.ds(c * lw, lw), pl.ds(n_state + c * lw, lw)
        ar, ai = a1_re_ref[:, re_sl], a1_im_ref[:, re_sl]
        xr = jnp.zeros((SUBLANES, lw), F32)
        xi = jnp.zeros((SUBLANES, lw), F32)
        for j in range(seg_len):
            br = xs_ref[blk(j), re_sl]
            bi = xs_ref[blk(j), im_sl]
            xr, xi = ar * xr - ai * xi + br, ar * xi + ai * xr + bi
            xs_ref[blk(j), re_sl] = xr
            xs_ref[blk(j), im_sl] = xi
        er, ei = xr, xi
        for k, s in enumerate((1, 2, 4)):
            pr, pi = shl_re_ref[k, :, re_sl], shl_im_ref[k, :, re_sl]
            sr, si = pltpu.roll(er, s, 0), pltpu.roll(ei, s, 0)
            er, ei = er + (pr * sr - pi * si), ei + (pr * si + pi * sr)
        tr, ti = carry_ref[:, re_sl], carry_ref[:, im_sl]
        pr, pi = alp_re_ref[:, re_sl], alp_im_ref[:, re_sl]
        fr = er + (pr * tr - pi * ti)
        fi = ei + (pr * ti + pi * tr)
        row = lax.broadcasted_iota(jnp.int32, fr.shape, 0)
        dr = jnp.where(row == 0, tr, pltpu.roll(fr, 1, 0))
        di = jnp.where(row == 0, ti, pltpu.roll(fi, 1, 0))
        last = SUBLANES - 1
        carry_ref[:, re_sl] = jnp.broadcast_to(fr[last:last + 1], fr.shape)
        carry_ref[:, im_sl] = jnp.broadcast_to(fi[last:last + 1], fi.shape)
        a8r, a8i = a8_re_ref[:, re_sl], a8_im_ref[:, re_sl]
        for a in range(seg_len // SUBLANES):
            for b in range(SUBLANES):
                j = a * SUBLANES + b
                pr, pi = apb_re_ref[b, :, re_sl], apb_im_ref[b, :, re_sl]
                xs_ref[blk(j), re_sl] += pr * dr - pi * di
                xs_ref[blk(j), im_sl] += pr * di + pi * dr
            dr, di = a8r * dr - a8i * di, a8r * di + a8i * dr

    ys = []
    for h in range(2):
        lhs = jnp.concatenate([xs_ref[:, pl.ds(h * half, half)],
                               xs_ref[:, pl.ds(n_state + h * half, half)]], axis=1)
        ys.append(_dot(lhs.astype(BF16), cmat_ref[h]))
    y = jnp.concatenate(ys, axis=1) + d_ref[...] * up
    z = jax.nn.gelu(y)
    gate = jax.nn.sigmoid(_dot(z.astype(BF16), wglu_ref[...]) + bglu_ref[...])
    o_hi, o_lo, o_lo2 = _split3(z * gate)
    permt = permt_ref[...]
    o_ref[...] = _dot(permt, o_hi) + _dot(permt, o_lo) + _dot(permt, o_lo2)


def _s5_branch(u2, bsz, tt, bmat, cmat, scan_consts, d_skip, w_glu, b_glu):
    t, d_ssm = u2.shape
    seq = t // bsz
    n_state = scan_consts[0].shape[1]
    nt = seq // tt
    seg_len = tt // SUBLANES
    rho = jnp.arange(tt)
    src = (rho % SUBLANES) * seg_len + rho // SUBLANES
    perm = (src[:, None] == jnp.arange(tt)[None, :]).astype(BF16)
    tok = lambda b, i: (b * nt + i, 0)
    whole = lambda a: pl.BlockSpec(a.shape, lambda b, i: (0,) * a.ndim)
    consts = [perm, perm.T, bmat, cmat, *scan_consts, d_skip.reshape(1, d_ssm), w_glu,
              b_glu.reshape(1, d_ssm)]
    return pl.pallas_call(
        functools.partial(_s5_kernel, n_state=n_state),
        grid=(bsz, nt),
        in_specs=[pl.BlockSpec((tt, d_ssm), tok)] + [whole(c) for c in consts],
        out_specs=pl.BlockSpec((tt, d_ssm), tok),
        out_shape=jax.ShapeDtypeStruct((t, d_ssm), F32),
        scratch_shapes=[pltpu.VMEM((tt, 2 * n_state), F32),
                        pltpu.VMEM((SUBLANES, 2 * n_state), F32)],
        compiler_params=pltpu.CompilerParams(dimension_semantics=("arbitrary", "arbitrary"),
                                             vmem_limit_bytes=VMEM_LIMIT),
        name="s5",
    )(u2, *consts)


def _s5_mixer(u2, bsz, lam_re, lam_im, log_dt, b_re, b_im, c_re, c_im, d_skip, w_glu, b_glu):
    d_ssm = d_skip.shape[0]
    n_grp, n_state, grp = b_re.shape
    seq = u2.shape[0] // bsz
    tt_s5 = min(TT_S5, seq)
    *scan_consts, bb_re, bb_im = _s5_prep(lam_re, lam_im, log_dt, b_re, b_im, tt_s5 // SUBLANES)
    to_blocks = lambda bb: jnp.swapaxes(bb.reshape(n_grp, n_state, grp), 1, 2)
    b_full = [_block_diag(to_blocks(bb)) for bb in (bb_re, bb_im)]
    c_full = [_block_diag(jnp.swapaxes(c, 1, 2)) for c in (c_re, -c_im)]
    hc, hs = d_ssm // 2, n_grp * n_state // 2
    bmat = jnp.stack([jnp.concatenate([m[h * hc:(h + 1) * hc, h * hs:(h + 1) * hs] for m in b_full],
                                      axis=1) for h in range(2)]).astype(BF16)
    cmat = jnp.stack([jnp.concatenate([m[h * hs:(h + 1) * hs, h * hc:(h + 1) * hc] for m in c_full],
                                      axis=0) for h in range(2)]).astype(BF16)
    return _s5_branch(u2, bsz, tt_s5, bmat, cmat, scan_consts, d_skip, w_glu.astype(BF16), b_glu)


def _pair_blocks(x):
    lane = lax.broadcasted_iota(jnp.int32, x.shape, 1)
    first = lane < RWKV_HEAD
    return jnp.concatenate([jnp.where(first, x, 0.0), jnp.where(first, 0.0, x)], axis=0)


def _pair_transpose(x):
    t = _pair_blocks(x).T
    return t[:CHUNK] + t[CHUNK:]


def _rwkv_chunks_scores(items):
    shape = (CHUNK, LANES)
    row = lax.broadcasted_iota(jnp.int32, shape, 0)
    col = lax.broadcasted_iota(jnp.int32, shape, 1) % RWKV_HEAD
    strict = row > col
    incl = row >= col
    eye = row == col
    blocks = lambda x: _pair_blocks(x).astype(BF16)

    for it in items:
        lhs = jnp.concatenate([it["at"], it["rt"]], axis=0).astype(BF16)
        rhs = jnp.concatenate([blocks(it["bt"]), blocks(it["kt"])], axis=0)
        sc = _dot_nt(lhs, rhs)
        it["s_ab"] = jnp.where(strict, sc[:CHUNK, :LANES], 0.0)
        it["s_ak"] = jnp.where(strict, sc[:CHUNK, LANES:], 0.0).astype(BF16)
        it["s_r"] = jnp.concatenate([jnp.where(incl, sc[CHUNK:, :LANES], 0.0),
                                     jnp.where(incl, sc[CHUNK:, LANES:], 0.0)], axis=1).astype(BF16)
        it["bd_v"] = blocks(it["v"])
        it["bk_t"] = jnp.concatenate([_pair_transpose(it["bh"]), _pair_transpose(it["kh"])],
                                     axis=1).astype(BF16)

    return strict, incl, eye, blocks


def _rwkv_chunks_rest(items, h_ref, eye, blocks):
    shape = (CHUNK, LANES)
    for it in items:
        it["npow"] = _dot(it["s_ab"].astype(BF16), blocks(it["s_ab"]))
        it["inv"] = jnp.where(eye, 1.0, 0.0) + it["s_ab"]
    for step in range(4):
        for it in items:
            both = _dot(jnp.concatenate([it["inv"], it["npow"]], axis=0).astype(BF16),
                        blocks(it["npow"]))
            it["inv"] = it["inv"] + both[:CHUNK]
            it["npow"] = both[CHUNK:]
    for it in items:
        it["inv"] = it["inv"] + _dot(it["inv"].astype(BF16), blocks(it["npow"]))

    for it in items:
        it["av"] = _dot(it["s_ak"], it["bd_v"])
    for it in items:
        wu = _dot(it["inv"].astype(BF16),
                  jnp.concatenate([blocks(it["at"]), blocks(it["av"])], axis=1))
        it["z"] = jnp.concatenate(
            [jnp.concatenate([blocks(wu[:, :LANES]), blocks(wu[:, LANES:])], axis=1),
             jnp.concatenate([jnp.zeros((LANES, LANES), BF16), it["bd_v"]], axis=1)], axis=0)
    for it in items:
        both = _dot(jnp.concatenate([it["s_r"], it["bk_t"]], axis=0), it["z"])
        qy, mg = both[:CHUNK], both[CHUNK:]
        m = jnp.where(eye, jnp.broadcast_to(it["decay"], shape), 0.0) + mg[:, :LANES]
        it["qm"] = jnp.concatenate([it["rt"] + qy[:, :LANES], m], axis=0).astype(BF16)
        it["yp"] = qy[:, LANES:]
        it["g"] = mg[:, LANES:]

    for it in items:
        both = _dot(it["qm"], blocks(h_ref[it["pair"]]))
        it["y"] = both[:CHUNK] + it["yp"]
        h_ref[it["pair"]] = both[CHUNK:] + it["g"]


OPERANDS = ("at", "rt", "bt", "kt", "bh", "kh", "v")


def _rwkv_kernel(r_ref, k2_ref, v_ref, av_ref, bv_ref, logw_ref, gate_ref, bonus_ref, lng_ref,
                 lnb_ref, ones_ref, ltri_ref, o_ref, h_ref, decay_ref, *op_refs):
    tt, d = r_ref.shape
    n_pairs = d // LANES
    n_chunks = tt // CHUNK
    ops = dict(zip(OPERANDS, op_refs))

    @pl.when(pl.program_id(1) == 0)
    def _():
        for ref in (h_ref, decay_ref) + tuple(op_refs):
            ref[...] = jnp.zeros_like(ref)

    items = []
    for c in range(n_chunks):
        rows = pl.ds(c * CHUNK, CHUNK)
        for j in range(n_pairs):
            ln = pl.ds(j * LANES, LANES)
            item = {name: ref[rows, ln] for name, ref in ops.items()}
            item["decay"] = decay_ref[c, :, ln]
            item["pair"] = j
            items.append(item)
    _, _, eye, blocks = _rwkv_chunks_scores(items)

    for c in range(n_chunks):
        rows = pl.ds(c * CHUNK, CHUNK)
        load = lambda ref: ref[rows, :].astype(F32)
        r, k2, v, av, bv, logw = map(load, (r_ref, k2_ref, v_ref, av_ref, bv_ref, logw_ref))
        l_hi, l_lo = _split2(logw)
        lc = _dot(ltri_ref[...], l_hi) + _dot(ltri_ref[...], l_lo)
        lc_end = lc[CHUNK - 1:CHUNK]
        e_in = jnp.exp(lc - logw)
        e_inc = jnp.exp(lc)
        e_neg = jnp.exp(-lc)
        e_out = jnp.exp(lc_end - lc)
        new = dict(at=av * e_in, rt=r * e_inc, bt=bv * e_neg, kt=k2 * e_neg, bh=bv * e_out,
                   kh=k2 * e_out, v=v)
        for name, val in new.items():
            ops[name][rows, :] = val
        decay_ref[c] = jnp.exp(lc_end)

    _rwkv_chunks_rest(items, h_ref, eye, blocks)
    ones = ones_ref[...]
    inv_n = 1.0 / RWKV_HEAD
    for c in range(n_chunks):
        rows = pl.ds(c * CHUNK, CHUNK)
        y = jnp.concatenate([it["y"] for it in items[c * n_pairs:(c + 1) * n_pairs]], axis=1)
        mean = _head_sum(y, ones) * inv_n
        yc = y - mean
        var = _head_sum(yc * yc, ones) * inv_n
        yn = yc * lax.rsqrt(var + GN_EPS) * lng_ref[...] + lnb_ref[...]
        o_ref[rows, :] = (yn + bonus_ref[rows, :]) * gate_ref[rows, :]


def _rwkv_branch(mix, bsz, lnx_g, lnx_b):
    *step_in, gate, bonus = mix
    t, d = gate.shape
    seq = t // bsz
    tt = min(TT_RWKV, seq)
    nt = seq // tt
    head = jnp.arange(LANES) // RWKV_HEAD
    ones = (head[:, None] == head[None, :]).astype(BF16)
    tok_i = jnp.arange(CHUNK)
    ltri = (tok_i[:, None] >= tok_i[None, :]).astype(BF16)
    this_tile = lambda b, i: (b * nt + jnp.minimum(i, nt - 1), 0)
    prev_tile = lambda b, i: (b * nt + jnp.maximum(i - 1, 0), 0)
    c2 = lambda b, i: (0, 0)
    consts = [lnx_g.reshape(1, d), lnx_b.reshape(1, d), ones, ltri]
    return pl.pallas_call(
        _rwkv_kernel,
        grid=(bsz, nt + 1),
        in_specs=[pl.BlockSpec((tt, d), this_tile) for _ in step_in]
        + [pl.BlockSpec((tt, d), prev_tile)] * 2 + [pl.BlockSpec(c.shape, c2) for c in consts],
        out_specs=pl.BlockSpec((tt, d), prev_tile),
        out_shape=jax.ShapeDtypeStruct((t, d), F32),
        scratch_shapes=[pltpu.VMEM((d // LANES, RWKV_HEAD, LANES), F32),
                        pltpu.VMEM((tt // CHUNK, 1, d), F32)]
        + [pltpu.VMEM((tt, d), F32)] * len(OPERANDS),
        compiler_params=pltpu.CompilerParams(dimension_semantics=("arbitrary", "arbitrary"),
                                             vmem_limit_bytes=VMEM_LIMIT),
        name="rwkv",
    )(*step_in, gate, bonus, *consts)


def _tail_kernel(x_ref, oa_ref, ob_ref, g1_ref, wg_ref, wba_ref, wbb_ref, wout_ref, g2_ref,
                 w1_ref, w2_ref, gf_ref, o_ref, *, ff_block):
    x = x_ref[...]
    d = x.shape[1]
    h = _rmsnorm(x, g1_ref[...]).astype(BF16)
    gates = jax.nn.sigmoid(_dot(h, wg_ref[...]))
    ma = _dot(oa_ref[...].astype(BF16), wba_ref[...])
    mb = _dot(ob_ref[...].astype(BF16), wbb_ref[...])
    merged = gates[:, :d] * ma + gates[:, d:] * mb
    x1 = x + _dot(merged.astype(BF16), wout_ref[...])
    h2 = _rmsnorm(x1, g2_ref[...]).astype(BF16)
    acc = x1
    for j in range(w1_ref.shape[1] // ff_block):
        cols = slice(j * ff_block, (j + 1) * ff_block)
        hid = jnp.square(jnp.maximum(_dot(h2, w1_ref[:, cols]), 0.0))
        acc = acc + _dot(hid.astype(BF16), w2_ref[cols, :])
    o_ref[...] = _rmsnorm(acc, gf_ref[...])


def _tail(x2, o_a, o_b, norm1_g, w_gate, wb_a, wb_b, w_out, norm2_g, w_ff1, w_ff2, norm_f_g):
    t, d = x2.shape
    d_mix = o_a.shape[1]
    tm = min(TM_PROJ, t)
    tok = lambda i: (i, 0)
    const = lambda i: (0, 0)
    resident = lambda a: pl.BlockSpec(a.shape, const, pipeline_mode=pl.Buffered(1))
    vec = lambda a: a.reshape(1, d)
    return pl.pallas_call(
        functools.partial(_tail_kernel, ff_block=d),
        grid=(t // tm,),
        in_specs=[pl.BlockSpec((tm, d), tok), pl.BlockSpec((tm, d_mix), tok),
                  pl.BlockSpec((tm, d_mix), tok), pl.BlockSpec((1, d), const),
                  resident(w_gate), resident(wb_a), resident(wb_b), resident(w_out),
                  pl.BlockSpec((1, d), const), resident(w_ff1), resident(w_ff2),
                  pl.BlockSpec((1, d), const)],
        out_specs=pl.BlockSpec((tm, d), tok),
        out_shape=jax.ShapeDtypeStruct((t, d), F32),
        compiler_params=pltpu.CompilerParams(dimension_semantics=("arbitrary",),
                                             vmem_limit_bytes=VMEM_LIMIT),
        name="tail",
    )(x2, o_a, o_b, vec(norm1_g), w_gate, wb_a, wb_b, w_out, vec(norm2_g), w_ff1, w_ff2,
      vec(norm_f_g))


def _layer(x2, bsz, norm1_g, w_in, lam_re, lam_im, log_dt, b_re, b_im, c_re, c_im, d_skip, w_glu,
           b_glu, mu_rwkv, w0, w2, a0, a2, g2, k_k, k_a, r_k, lnx_g, lnx_b, w_branch, w_out,
           norm2_g, w_ff1, w_ff2, out_g):
    d_ssm = d_skip.shape[0]
    d_mix_in = d_ssm + mu_rwkv.shape[0]
    u2, *mix = _in_proj(x2, x2.shape[0] // bsz, norm1_g, w_in[:, :d_mix_in].astype(BF16), d_ssm,
                        mu_rwkv, w0, w2, a0, a2, g2, k_k, k_a, r_k.reshape(-1))
    o_a = _s5_mixer(u2, bsz, lam_re, lam_im, log_dt, b_re, b_im, c_re, c_im, d_skip, w_glu, b_glu)
    o_b = _rwkv_branch(mix, bsz, lnx_g, lnx_b)
    return _tail(x2, o_a, o_b, norm1_g, w_in[:, d_mix_in:].astype(BF16),
                 w_branch[:d_ssm].astype(BF16), w_branch[d_ssm:].astype(BF16),
                 w_out.astype(BF16), norm2_g, w_ff1.astype(BF16), w_ff2.astype(BF16), out_g)


def kernel(x, norm1_g, w_in, lam_re, lam_im, log_dt, b_re, b_im, c_re, c_im, d_skip, w_glu, b_glu,
           mu_rwkv, w0, w2, a0, a2, g2, k_k, k_a, r_k, lnx_g, lnx_b, w_branch, w_out, norm2_g,
           w_ff1, w_ff2, norm_f_g):
    bsz, seq, d = x.shape
    depth = norm1_g.shape[0]
    assert depth == 1, "the fused tail applies the final norm, so exactly one layer is supported"
    x2 = x.reshape(bsz * seq, d)
    l = 0
    out = _layer(x2, bsz, norm1_g[l], w_in[l], lam_re[l], lam_im[l], log_dt[l], b_re[l], b_im[l],
                 c_re[l], c_im[l], d_skip[l], w_glu[l], b_glu[l], mu_rwkv[l], w0[l], w2[l], a0[l],
                 a2[l], g2[l], k_k[l], k_a[l], r_k[l], lnx_g[l], lnx_b[l], w_branch[l], w_out[l],
                 norm2_g[l], w_ff1[l], w_ff2[l], norm_f_g)
    return out.reshape(bsz, seq, d)
```

```python
import functools

import jax
import jax.numpy as jnp
from jax import lax
from jax.experimental import pallas as pl
from jax.experimental.pallas import tpu as pltpu

F32 = jnp.float32
BF16 = jnp.bfloat16

RMS_EPS = 1e-6
GN_EPS = 64e-5
L2_EPS = 1e-12
MIN_NEG_REAL = -1e-4
EXP_NEG_HALF = 0.6065306597126334

SSM_GROUP = 16
SSM_STATE = 64
RWKV_HEAD = 64
GATE_LORA = 128
DECAY_LORA = 64
AAA_LORA = 64
CHUNK = 64

SUBLANES = 8
LANES = 128
VMEM_LIMIT = 56 * 1024 * 1024

TM_PROJ = 512
TT_S5 = 256
TT_RWKV = 256
S5_LANE_CHUNK = 512


def _dot(a, b):
    return jnp.dot(a, b, preferred_element_type=F32)


def _dot_nt(a, b):
    return lax.dot_general(a, b, (((1,), (1,)), ((), ())), preferred_element_type=F32)


def _split2(x):
    hi = x.astype(BF16)
    lo = (x - hi.astype(F32)).astype(BF16)
    return hi, lo


def _split3(x):
    hi = x.astype(BF16)
    r1 = x - hi.astype(F32)
    lo = r1.astype(BF16)
    lo2 = (r1 - lo.astype(F32)).astype(BF16)
    return hi, lo, lo2


def _rmsnorm(x, g):
    ms = jnp.mean(x * x, axis=-1, keepdims=True)
    return x * lax.rsqrt(ms + RMS_EPS) * g


def _cmul(ar, ai, br, bi):
    return ar * br - ai * bi, ar * bi + ai * br


def _s5_prep_kernel(lre_ref, lim_ref, ldt_ref, lre_w_ref, lim_w_ref, ldt_w_ref, bre_ref, bim_ref,
                    a1_re_ref, a1_im_ref, a8_re_ref, a8_im_ref, apb_re_ref, apb_im_ref,
                    alp_re_ref, alp_im_ref, shl_re_ref, shl_im_ref, bbre_ref, bbim_ref, *, seg_len):
    def discretise(lre, lim, ldt):
        lr = jnp.minimum(lre, MIN_NEG_REAL)
        dt = jnp.exp(ldt)
        mag = jnp.exp(lr * dt)
        ab_re = mag * jnp.cos(lim * dt)
        ab_im = mag * jnp.sin(lim * dt)
        den = lr * lr + lim * lim
        xm1 = ab_re - 1.0
        q_re = (xm1 * lr + ab_im * lim) / den
        q_im = (ab_im * lr - xm1 * lim) / den
        return ab_re, ab_im, q_re, q_im

    shape = a1_re_ref.shape
    a_re, a_im, _, _ = discretise(lre_ref[...], lim_ref[...], ldt_ref[...])
    bcast = lambda x: jnp.broadcast_to(x, shape)
    pows = [(a_re, a_im)]
    for _ in range(SUBLANES - 1):
        pows.append(_cmul(*pows[-1], a_re, a_im))
    a1_re_ref[...] = bcast(a_re)
    a1_im_ref[...] = bcast(a_im)
    a8_re_ref[...] = bcast(pows[-1][0])
    a8_im_ref[...] = bcast(pows[-1][1])
    for b, (pr, pi) in enumerate(pows):
        apb_re_ref[b] = bcast(pr)
        apb_im_ref[b] = bcast(pi)
    lr_, li_ = pows[-1]
    n = SUBLANES
    while n < seg_len:
        lr_, li_ = _cmul(lr_, li_, lr_, li_)
        n *= 2
    lpows = [(lr_, li_)]
    for _ in range(SUBLANES - 1):
        lpows.append(_cmul(*lpows[-1], lr_, li_))
    row = lax.broadcasted_iota(jnp.int32, shape, 0)
    alp_re = jnp.zeros(shape, F32)
    alp_im = jnp.zeros(shape, F32)
    for i, (pr, pi) in enumerate(lpows):
        alp_re = jnp.where(row == i, bcast(pr), alp_re)
        alp_im = jnp.where(row == i, bcast(pi), alp_im)
    alp_re_ref[...] = alp_re
    alp_im_ref[...] = alp_im
    for i, s in enumerate((1, 2, 4)):
        pr, pi = lpows[s - 1]
        shl_re_ref[i] = jnp.where(row >= s, bcast(pr), 0.0)
        shl_im_ref[i] = jnp.where(row >= s, bcast(pi), 0.0)

    _, _, q_re, q_im = discretise(lre_w_ref[...], lim_w_ref[...], ldt_w_ref[...])
    br = bre_ref[...]
    bi = bim_ref[...]
    bbre_ref[...] = q_re * br - q_im * bi
    bbim_ref[...] = q_re * bi + q_im * br


def _s5_prep(lam_re, lam_im, log_dt, b_re, b_im, seg_len):
    g, p = lam_re.shape
    m = b_re.shape[-1]
    n = g * p
    row = lambda a: a.reshape(1, n)
    wide = lambda a: jnp.repeat(a, m, axis=-1)
    ldt_gp = jnp.broadcast_to(log_dt[:, None], (g, p))
    s8 = jax.ShapeDtypeStruct((SUBLANES, n), F32)
    s88 = jax.ShapeDtypeStruct((SUBLANES, SUBLANES, n), F32)
    s38 = jax.ShapeDtypeStruct((3, SUBLANES, n), F32)
    sw = jax.ShapeDtypeStruct((g, p * m), F32)
    out_shape = (s8, s8, s8, s8, s88, s88, s8, s8, s38, s38, sw, sw)
    return pl.pallas_call(functools.partial(_s5_prep_kernel, seg_len=seg_len),
                          out_shape=out_shape, name="s5_prep")(
        row(lam_re), row(lam_im), row(ldt_gp), wide(lam_re), wide(lam_im), wide(ldt_gp),
        b_re.reshape(g, p * m), b_im.reshape(g, p * m))


def _block_diag(blocks):
    g, r, c = blocks.shape
    eye = jnp.eye(g, dtype=bool)
    return jnp.where(eye[:, None, :, None], blocks[:, :, None, :], 0.0).reshape(g * r, g * c)


def _head_sum(x, ones):
    rows, width = x.shape
    groups = width // LANES
    stacked = jnp.concatenate([x[:, j * LANES:(j + 1) * LANES] for j in range(groups)], axis=0)
    sums = _dot(stacked.astype(BF16), ones)
    return jnp.concatenate([sums[j * rows:(j + 1) * rows] for j in range(groups)], axis=1)


def _mix_in_kernel(*refs, d_ssm, d, tiles_per_seq, tt_s5):
    (x_ref, g_ref, w_ref, mu_ref, wwa_ref, g2_ref, w0_ref, a0_ref, kk_ref, ka_ref, rk_ref,
     ones_ref) = refs[:12]
    s5 = dict(zip(S5_CONSTS, refs[12:12 + len(S5_CONSTS)]))
    (oa_ref, r_ref, k2_ref, v_ref, av_ref, bv_ref, logw_ref, gate_ref, bonus_ref,
     prev_ref, carry_ref, *xs_refs) = refs[12 + len(S5_CONSTS):]
    tm = x_ref.shape[0]

    @pl.when(pl.program_id(0) % tiles_per_seq == 0)
    def _():
        prev_ref[...] = jnp.zeros_like(prev_ref)
        carry_ref[...] = jnp.zeros_like(carry_ref)

    h = _rmsnorm(x_ref[...], g_ref[...]).astype(BF16)

    def mixed(c0, c1):
        p = _dot(h, w_ref[:, d_ssm + c0:d_ssm + c1])
        row = lax.broadcasted_iota(jnp.int32, p.shape, 0)
        last = prev_ref[SUBLANES - 1:SUBLANES, c0:c1]
        prev = jnp.where(row == 0, jnp.broadcast_to(last, p.shape), pltpu.roll(p, 1, 0))
        prev_ref[:, c0:c1] = p[tm - SUBLANES:, :]
        return p + (prev - p) * mu_ref[:, c0:c1]

    sub = lambda i: slice(i * tt_s5, (i + 1) * tt_s5)
    n_sub = tm // tt_s5
    ups = []
    for i in range(n_sub):
        ups.append(_s5_project(_dot(h[sub(i)], w_ref[:, :d_ssm]), s5, xs_refs[i]))
    _s5_scan(s5, xs_refs[0], carry_ref)
    lora_in = mixed(3 * d, 3 * d + GATE_LORA + DECAY_LORA + AAA_LORA)
    k = mixed(d, 2 * d)
    xg = lora_in[:, :GATE_LORA]
    xwa = lora_in[:, GATE_LORA:]
    lane = lax.broadcasted_iota(jnp.int32, xwa.shape, 1)
    zwa = jnp.where(lane < DECAY_LORA, jnp.tanh(xwa), xwa)
    lora = _dot(zwa.astype(BF16), wwa_ref[...])
    gate_ref[...] = _dot(jax.nn.sigmoid(xg).astype(BF16), g2_ref[...])
    logw_ref[...] = -EXP_NEG_HALF * jax.nn.sigmoid(w0_ref[...] + lora[:, :d])
    a = jax.nn.sigmoid(a0_ref[...] + lora[:, d:])
    for i in range(1, n_sub):
        _s5_scan(s5, xs_refs[i], carry_ref)
    oa_ref[sub(0), :] = _s5_readout(ups[0], s5, xs_refs[0])

    ones = ones_ref[...]
    r = mixed(0, d)
    kk = k * kk_ref[...]
    kk = kk * lax.rsqrt(jnp.maximum(_head_sum(kk * kk, ones), L2_EPS * L2_EPS))
    v = mixed(2 * d, 3 * d)
    k2 = k * (1.0 + (a - 1.0) * ka_ref[...])
    k2_ref[...] = k2.astype(BF16)
    av_ref[...] = (-kk).astype(BF16)
    bv_ref[...] = (kk * a).astype(BF16)
    r_ref[...] = r.astype(BF16)
    v_ref[...] = v.astype(BF16)
    bonus_ref[...] = _head_sum(r * k2 * rk_ref[...], ones) * v
    for i in range(1, n_sub):
        oa_ref[sub(i), :] = _s5_readout(ups[i], s5, xs_refs[i])


def _mix_in(x2, seq, norm_g, w_mix, d_ssm, mu, w0, w2, a0, a2, g2, k_k, k_a, r_k, s5_params):
    t, dm = x2.shape
    n = w_mix.shape[1]
    d = w0.shape[0]
    tm = min(TM_PROJ, seq)
    tt_s5 = min(TT_S5, tm)
    s5_consts = _s5_constants(tt_s5, *s5_params)
    n_state2 = s5_consts[2].shape[2]
    wwa = jnp.concatenate([jnp.concatenate([w2, jnp.zeros_like(w2)], axis=1),
                           jnp.concatenate([jnp.zeros_like(a2), a2], axis=1)], axis=0)
    head = jnp.arange(LANES) // RWKV_HEAD
    ones = (head[:, None] == head[None, :]).astype(BF16)
    vec = lambda a: a.reshape(1, -1)
    consts = [vec(norm_g), w_mix, vec(mu), wwa.astype(BF16), g2.astype(BF16), vec(w0), vec(a0),
              vec(k_k), vec(k_a), vec(r_k), ones, *s5_consts]
    whole = lambda a: pl.BlockSpec(a.shape, lambda i: (0,) * a.ndim)
    tok = lambda i: (i, 0)
    widths = [(d_ssm, F32)] + [(d, BF16)] * 5 + [(d, F32)] * 3
    return pl.pallas_call(
        functools.partial(_mix_in_kernel, d_ssm=d_ssm, d=d, tiles_per_seq=seq // tm, tt_s5=tt_s5),
        grid=(t // tm,),
        in_specs=[pl.BlockSpec((tm, dm), tok)] + [whole(c) for c in consts],
        out_specs=[pl.BlockSpec((tm, w), tok) for w, _ in widths],
        out_shape=[jax.ShapeDtypeStruct((t, w), dt) for w, dt in widths],
        scratch_shapes=[pltpu.VMEM((SUBLANES, n - d_ssm), F32),
                        pltpu.VMEM((SUBLANES, 2 * n_state2), F32)]
        + [pltpu.VMEM((tt_s5, 2 * n_state2), F32)] * (tm // tt_s5),
        compiler_params=pltpu.CompilerParams(dimension_semantics=("arbitrary",),
                                             vmem_limit_bytes=VMEM_LIMIT),
        name="mix_in",
    )(x2, *consts)


S5_CONSTS = ("perm", "permt", "bmat", "cmat", "a1_re", "a1_im", "a8_re", "a8_im", "apb_re", "apb_im",
             "alp_re", "alp_im", "shl_re", "shl_im", "d_skip", "w_glu", "b_glu")


def _s5_project(u, s5, xs_ref):
    n_state = xs_ref.shape[1] // 2
    half = n_state // 2
    lw = S5_LANE_CHUNK
    perm = s5["perm"][...]
    u_hi, u_lo, u_lo2 = _split3(u)
    up_hi = _dot(perm, u_hi)
    up = up_hi + _dot(perm, u_lo) + _dot(perm, u_lo2)
    ub = up_hi.astype(BF16)
    d_half = ub.shape[1] // 2
    for c in range(n_state // lw):
        h, cc = divmod(c, half // lw)
        for part in range(2):
            xs_ref[:, pl.ds(part * n_state + c * lw, lw)] = _dot(
                ub[:, h * d_half:(h + 1) * d_half],
                s5["bmat"][h, :, pl.ds(part * half + cc * lw, lw)])
    return up


def _s5_scan(s5, xs_ref, carry_ref):
    tt = xs_ref.shape[0]
    n_state = xs_ref.shape[1] // 2
    seg_len = tt // SUBLANES
    lw = S5_LANE_CHUNK
    blk = lambda j: pl.ds(j * SUBLANES, SUBLANES)
    for c in range(n_state // lw):
        re_sl, im_sl = pl.ds(c * lw, lw), pl.ds(n_state + c * lw, lw)
        ar, ai = s5["a1_re"][:, re_sl], s5["a1_im"][:, re_sl]
        xr = jnp.zeros((SUBLANES, lw), F32)
        xi = jnp.zeros((SUBLANES, lw), F32)
        for j in range(seg_len):
            br = xs_ref[blk(j), re_sl]
            bi = xs_ref[blk(j), im_sl]
            xr, xi = ar * xr - ai * xi + br, ar * xi + ai * xr + bi
            xs_ref[blk(j), re_sl] = xr
            xs_ref[blk(j), im_sl] = xi
        er, ei = xr, xi
        for k, sh in enumerate((1, 2, 4)):
            pr, pi = s5["shl_re"][k, :, re_sl], s5["shl_im"][k, :, re_sl]
            sr, si = pltpu.roll(er, sh, 0), pltpu.roll(ei, sh, 0)
            er, ei = er + (pr * sr - pi * si), ei + (pr * si + pi * sr)
        tr, ti = carry_ref[:, re_sl], carry_ref[:, im_sl]
        pr, pi = s5["alp_re"][:, re_sl], s5["alp_im"][:, re_sl]
        fr = er + (pr * tr - pi * ti)
        fi = ei + (pr * ti + pi * tr)
        row = lax.broadcasted_iota(jnp.int32, fr.shape, 0)
        dr = jnp.where(row == 0, tr, pltpu.roll(fr, 1, 0))
        di = jnp.where(row == 0, ti, pltpu.roll(fi, 1, 0))
        last = SUBLANES - 1
        carry_ref[:, re_sl] = jnp.broadcast_to(fr[last:last + 1], fr.shape)
        carry_ref[:, im_sl] = jnp.broadcast_to(fi[last:last + 1], fi.shape)
        a8r, a8i = s5["a8_re"][:, re_sl], s5["a8_im"][:, re_sl]
        for a in range(seg_len // SUBLANES):
            for b in range(SUBLANES):
                j = a * SUBLANES + b
                pr, pi = s5["apb_re"][b, :, re_sl], s5["apb_im"][b, :, re_sl]
                xs_ref[blk(j), re_sl] += pr * dr - pi * di
                xs_ref[blk(j), im_sl] += pr * di + pi * dr
            dr, di = a8r * dr - a8i * di, a8r * di + a8i * dr


def _s5_readout(up, s5, xs_ref):
    n_state = xs_ref.shape[1] // 2
    half = n_state // 2
    lw = S5_LANE_CHUNK
    ys = [None, None]
    for c in range(n_state // lw):
        h, cc = divmod(c, half // lw)
        for part in range(2):
            term = _dot(xs_ref[:, pl.ds(part * n_state + c * lw, lw)].astype(BF16),
                        s5["cmat"][h, pl.ds(part * half + cc * lw, lw), :])
            ys[h] = term if ys[h] is None else ys[h] + term
    y = jnp.concatenate(ys, axis=1) + s5["d_skip"][...] * up
    z = jax.nn.gelu(y)
    gate = jax.nn.sigmoid(_dot(z.astype(BF16), s5["w_glu"][...]) + s5["b_glu"][...])
    o_hi, o_lo, o_lo2 = _split3(z * gate)
    permt = s5["permt"][...]
    return _dot(permt, o_hi) + _dot(permt, o_lo) + _dot(permt, o_lo2)


def _s5_constants(tt, lam_re, lam_im, log_dt, b_re, b_im, c_re, c_im, d_skip, w_glu, b_glu):
    d_ssm = d_skip.shape[0]
    n_grp, n_state, grp = b_re.shape
    seg_len = tt // SUBLANES
    *scan_consts, bb_re, bb_im = _s5_prep(lam_re, lam_im, log_dt, b_re, b_im, seg_len)
    to_blocks = lambda bb: jnp.swapaxes(bb.reshape(n_grp, n_state, grp), 1, 2)
    b_full = [_block_diag(to_blocks(bb)) for bb in (bb_re, bb_im)]
    c_full = [_block_diag(jnp.swapaxes(c, 1, 2)) for c in (c_re, -c_im)]
    hc, hs = d_ssm // 2, n_grp * n_state // 2
    bmat = jnp.stack([jnp.concatenate([m[h * hc:(h + 1) * hc, h * hs:(h + 1) * hs] for m in b_full],
                                      axis=1) for h in range(2)]).astype(BF16)
    cmat = jnp.stack([jnp.concatenate([m[h * hs:(h + 1) * hs, h * hc:(h + 1) * hc] for m in c_full],
                                      axis=0) for h in range(2)]).astype(BF16)
    rho = jnp.arange(tt)
    src = (rho % SUBLANES) * seg_len + rho // SUBLANES
    perm = (src[:, None] == jnp.arange(tt)[None, :]).astype(BF16)
    return [perm, perm.T, bmat, cmat, *scan_consts, d_skip.reshape(1, d_ssm), w_glu.astype(BF16),
            b_glu.reshape(1, d_ssm)]


def _pair_blocks(x):
    lane = lax.broadcasted_iota(jnp.int32, x.shape, 1)
    first = lane < RWKV_HEAD
    return jnp.concatenate([jnp.where(first, x, 0.0), jnp.where(first, 0.0, x)], axis=0)


def _pair_transpose(x):
    t = _pair_blocks(x).T
    return t[:CHUNK] + t[CHUNK:]


def _rwkv_chunks_scores(items):
    shape = (CHUNK, LANES)
    row = lax.broadcasted_iota(jnp.int32, shape, 0)
    col = lax.broadcasted_iota(jnp.int32, shape, 1) % RWKV_HEAD
    strict = row > col
    incl = row >= col
    eye = row == col
    blocks = lambda x: _pair_blocks(x).astype(BF16)

    for it in items:
        lhs = jnp.concatenate([it["at"], it["rt"]], axis=0).astype(BF16)
        rhs = jnp.concatenate([blocks(it["bt"]), blocks(it["kt"])], axis=0)
        sc = _dot_nt(lhs, rhs)
        it["s_ab"] = jnp.where(strict, sc[:CHUNK, :LANES], 0.0)
        it["s_ak"] = jnp.where(strict, sc[:CHUNK, LANES:], 0.0).astype(BF16)
        it["s_r"] = jnp.concatenate([jnp.where(incl, sc[CHUNK:, :LANES], 0.0),
                                     jnp.where(incl, sc[CHUNK:, LANES:], 0.0)], axis=1).astype(BF16)
        it["bd_v"] = blocks(it["v"])
        it["bk_t"] = jnp.concatenate([_pair_transpose(it["bh"]), _pair_transpose(it["kh"])],
                                     axis=1).astype(BF16)

    return strict, incl, eye, blocks


def _rwkv_chunks_rest(items, h_ref, eye, blocks):
    shape = (CHUNK, LANES)
    for it in items:
        it["npow"] = _dot(it["s_ab"].astype(BF16), blocks(it["s_ab"]))
        it["inv"] = jnp.where(eye, 1.0, 0.0) + it["s_ab"]
    for step in range(4):
        for it in items:
            both = _dot(jnp.concatenate([it["inv"], it["npow"]], axis=0).astype(BF16),
                        blocks(it["npow"]))
            it["inv"] = it["inv"] + both[:CHUNK]
            it["npow"] = both[CHUNK:]
    for it in items:
        it["inv"] = it["inv"] + _dot(it["inv"].astype(BF16), blocks(it["npow"]))

    for it in items:
        it["av"] = _dot(it["s_ak"], it["bd_v"])
    for it in items:
        wu = _dot(it["inv"].astype(BF16),
                  jnp.concatenate([blocks(it["at"]), blocks(it["av"])], axis=1))
        it["z"] = jnp.concatenate(
            [jnp.concatenate([blocks(wu[:, :LANES]), blocks(wu[:, LANES:])], axis=1),
             jnp.concatenate([jnp.zeros((LANES, LANES), BF16), it["bd_v"]], axis=1)], axis=0)
    for it in items:
        both = _dot(jnp.concatenate([it["s_r"], it["bk_t"]], axis=0), it["z"])
        qy, mg = both[:CHUNK], both[CHUNK:]
        m = jnp.where(eye, jnp.broadcast_to(it["decay"], shape), 0.0) + mg[:, :LANES]
        it["qm"] = jnp.concatenate([it["rt"] + qy[:, :LANES], m], axis=0).astype(BF16)
        it["yp"] = qy[:, LANES:]
        it["g"] = mg[:, LANES:]

    for it in items:
        both = _dot(it["qm"], blocks(h_ref[it["pair"]]))
        it["y"] = both[:CHUNK] + it["yp"]
        h_ref[it["pair"]] = both[CHUNK:] + it["g"]


OPERANDS = ("at", "rt", "bt", "kt", "bh", "kh", "v")


def _rwkv_kernel(r_ref, k2_ref, v_ref, av_ref, bv_ref, logw_ref, gate_ref, bonus_ref, lng_ref,
                 lnb_ref, ones_ref, ltri_ref, o_ref, h_ref, decay_ref, *op_refs):
    tt, d = r_ref.shape
    n_pairs = d // LANES
    n_chunks = tt // CHUNK
    ops = dict(zip(OPERANDS, op_refs))

    @pl.when(pl.program_id(1) == 0)
    def _():
        for ref in (h_ref, decay_ref) + tuple(op_refs):
            ref[...] = jnp.zeros_like(ref)

    items = []
    for c in range(n_chunks):
        rows = pl.ds(c * CHUNK, CHUNK)
        for j in range(n_pairs):
            ln = pl.ds(j * LANES, LANES)
            item = {name: ref[rows, ln] for name, ref in ops.items()}
            item["decay"] = decay_ref[c, :, ln]
            item["pair"] = j
            items.append(item)
    _, _, eye, blocks = _rwkv_chunks_scores(items)

    for c in range(n_chunks):
        rows = pl.ds(c * CHUNK, CHUNK)
        load = lambda ref: ref[rows, :].astype(F32)
        r, k2, v, av, bv, logw = map(load, (r_ref, k2_ref, v_ref, av_ref, bv_ref, logw_ref))
        l_hi, l_lo = _split2(logw)
        lc = _dot(ltri_ref[...], l_hi) + _dot(ltri_ref[...], l_lo)
        lc_end = lc[CHUNK - 1:CHUNK]
        e_in = jnp.exp(lc - logw)
        e_inc = jnp.exp(lc)
        e_neg = jnp.exp(-lc)
        e_out = jnp.exp(lc_end - lc)
        new = dict(at=av * e_in, rt=r * e_inc, bt=bv * e_neg, kt=k2 * e_neg, bh=bv * e_out,
                   kh=k2 * e_out, v=v)
        for name, val in new.items():
            ops[name][rows, :] = val
        decay_ref[c] = jnp.exp(lc_end)

    _rwkv_chunks_rest(items, h_ref, eye, blocks)
    ones = ones_ref[...]
    inv_n = 1.0 / RWKV_HEAD
    for c in range(n_chunks):
        rows = pl.ds(c * CHUNK, CHUNK)
        y = jnp.concatenate([it["y"] for it in items[c * n_pairs:(c + 1) * n_pairs]], axis=1)
        mean = _head_sum(y, ones) * inv_n
        yc = y - mean
        var = _head_sum(yc * yc, ones) * inv_n
        yn = yc * lax.rsqrt(var + GN_EPS) * lng_ref[...] + lnb_ref[...]
        o_ref[rows, :] = (yn + bonus_ref[rows, :]) * gate_ref[rows, :]


def _rwkv_branch(mix, bsz, lnx_g, lnx_b):
    *step_in, gate, bonus = mix
    t, d = gate.shape
    seq = t // bsz
    tt = min(TT_RWKV, seq)
    nt = seq // tt
    head = jnp.arange(LANES) // RWKV_HEAD
    ones = (head[:, None] == head[None, :]).astype(BF16)
    tok_i = jnp.arange(CHUNK)
    ltri = (tok_i[:, None] >= tok_i[None, :]).astype(BF16)
    this_tile = lambda b, i: (b * nt + jnp.minimum(i, nt - 1), 0)
    prev_tile = lambda b, i: (b * nt + jnp.maximum(i - 1, 0), 0)
    c2 = lambda b, i: (0, 0)
    consts = [lnx_g.reshape(1, d), lnx_b.reshape(1, d), ones, ltri]
    return pl.pallas_call(
        _rwkv_kernel,
        grid=(bsz, nt + 1),
        in_specs=[pl.BlockSpec((tt, d), this_tile) for _ in step_in]
        + [pl.BlockSpec((tt, d), prev_tile)] * 2 + [pl.BlockSpec(c.shape, c2) for c in consts],
        out_specs=pl.BlockSpec((tt, d), prev_tile),
        out_shape=jax.ShapeDtypeStruct((t, d), F32),
        scratch_shapes=[pltpu.VMEM((d // LANES, RWKV_HEAD, LANES), F32),
                        pltpu.VMEM((tt // CHUNK, 1, d), F32)]
        + [pltpu.VMEM((tt, d), F32)] * len(OPERANDS),
        compiler_params=pltpu.CompilerParams(dimension_semantics=("arbitrary", "arbitrary"),
                                             vmem_limit_bytes=VMEM_LIMIT),
        name="rwkv",
    )(*step_in, gate, bonus, *consts)


def _tail_kernel(x_ref, oa_ref, ob_ref, g1_ref, wg_ref, wba_ref, wbb_ref, wout_ref, g2_ref,
                 w1_ref, w2_ref, gf_ref, o_ref, *, ff_block):
    x = x_ref[...]
    d = x.shape[1]
    h = _rmsnorm(x, g1_ref[...]).astype(BF16)
    gates = jax.nn.sigmoid(_dot(h, wg_ref[...]))
    ma = _dot(oa_ref[...].astype(BF16), wba_ref[...])
    mb = _dot(ob_ref[...].astype(BF16), wbb_ref[...])
    merged = gates[:, :d] * ma + gates[:, d:] * mb
    x1 = x + _dot(merged.astype(BF16), wout_ref[...])
    h2 = _rmsnorm(x1, g2_ref[...]).astype(BF16)
    acc = x1
    for j in range(w1_ref.shape[1] // ff_block):
        cols = slice(j * ff_block, (j + 1) * ff_block)
        hid = jnp.square(jnp.maximum(_dot(h2, w1_ref[:, cols]), 0.0))
        acc = acc + _dot(hid.astype(BF16), w2_ref[cols, :])
    o_ref[...] = _rmsnorm(acc, gf_ref[...])


def _tail(x2, o_a, o_b, norm1_g, w_gate, wb_a, wb_b, w_out, norm2_g, w_ff1, w_ff2, norm_f_g):
    t, d = x2.shape
    d_mix = o_a.shape[1]
    tm = min(TM_PROJ, t)
    tok = lambda i: (i, 0)
    const = lambda i: (0, 0)
    resident = lambda a: pl.BlockSpec(a.shape, const, pipeline_mode=pl.Buffered(1))
    vec = lambda a: a.reshape(1, d)
    return pl.pallas_call(
        functools.partial(_tail_kernel, ff_block=d),
        grid=(t // tm,),
        in_specs=[pl.BlockSpec((tm, d), tok), pl.BlockSpec((tm, d_mix), tok),
                  pl.BlockSpec((tm, d_mix), tok), pl.BlockSpec((1, d), const),
                  resident(w_gate), resident(wb_a), resident(wb_b), resident(w_out),
                  pl.BlockSpec((1, d), const), resident(w_ff1), resident(w_ff2),
                  pl.BlockSpec((1, d), const)],
        out_specs=pl.BlockSpec((tm, d), tok),
        out_shape=jax.ShapeDtypeStruct((t, d), F32),
        compiler_params=pltpu.CompilerParams(dimension_semantics=("arbitrary",),
                                             vmem_limit_bytes=VMEM_LIMIT),
        name="tail",
    )(x2, o_a, o_b, vec(norm1_g), w_gate, wb_a, wb_b, w_out, vec(norm2_g), w_ff1, w_ff2,
      vec(norm_f_g))


def _layer(x2, bsz, norm1_g, w_in, lam_re, lam_im, log_dt, b_re, b_im, c_re, c_im, d_skip, w_glu,
           b_glu, mu_rwkv, w0, w2, a0, a2, g2, k_k, k_a, r_k, lnx_g, lnx_b, w_branch, w_out,
           norm2_g, w_ff1, w_ff2, out_g):
    d_ssm = d_skip.shape[0]
    d_mix_in = d_ssm + mu_rwkv.shape[0]
    o_a, *mix = _mix_in(x2, x2.shape[0] // bsz, norm1_g, w_in[:, :d_mix_in].astype(BF16), d_ssm,
                        mu_rwkv, w0, w2, a0, a2, g2, k_k, k_a, r_k.reshape(-1),
                        (lam_re, lam_im, log_dt, b_re, b_im, c_re, c_im, d_skip, w_glu, b_glu))
    o_b = _rwkv_branch(mix, bsz, lnx_g, lnx_b)
    return _tail(x2, o_a, o_b, norm1_g, w_in[:, d_mix_in:].astype(BF16),
                 w_branch[:d_ssm].astype(BF16), w_branch[d_ssm:].astype(BF16),
                 w_out.astype(BF16), norm2_g, w_ff1.astype(BF16), w_ff2.astype(BF16), out_g)


def kernel(x, norm1_g, w_in, lam_re, lam_im, log_dt, b_re, b_im, c_re, c_im, d_skip, w_glu, b_glu,
           mu_rwkv, w0, w2, a0, a2, g2, k_k, k_a, r_k, lnx_g, lnx_b, w_branch, w_out, norm2_g,
           w_ff1, w_ff2, norm_f_g):
    bsz, seq, d = x.shape
    depth = norm1_g.shape[0]
    assert depth == 1, "the fused tail applies the final norm, so exactly one layer is supported"
    x2 = x.reshape(bsz * seq, d)
    l = 0
    out = _layer(x2, bsz, norm1_g[l], w_in[l], lam_re[l], lam_im[l], log_dt[l], b_re[l], b_im[l],
                 c_re[l], c_im[l], d_skip[l], w_glu[l], b_glu[l], mu_rwkv[l], w0[l], w2[l], a0[l],
                 a2[l], g2[l], k_k[l], k_a[l], r_k[l], lnx_g[l], lnx_b[l], w_branch[l], w_out[l],
                 norm2_g[l], w_ff1[l], w_ff2[l], norm_f_g)
    return out.reshape(bsz, seq, d)
```

```python
import functools

import jax
import jax.numpy as jnp
from jax import lax
from jax.experimental import pallas as pl
from jax.experimental.pallas import tpu as pltpu

F32 = jnp.float32
BF16 = jnp.bfloat16

RMS_EPS = 1e-6
GN_EPS = 64e-5
L2_EPS = 1e-12
MIN_NEG_REAL = -1e-4
EXP_NEG_HALF = 0.6065306597126334

SSM_GROUP = 16
SSM_STATE = 64
RWKV_HEAD = 64
GATE_LORA = 128
DECAY_LORA = 64
AAA_LORA = 64
CHUNK = 64

SUBLANES = 8
LANES = 128
VMEM_LIMIT = 56 * 1024 * 1024

TM_PROJ = 512
TT_S5 = 256
TT_RWKV = 256
S5_LANE_CHUNK = 512


def _dot(a, b):
    return jnp.dot(a, b, preferred_element_type=F32)


def _dot_nt(a, b):
    return lax.dot_general(a, b, (((1,), (1,)), ((), ())), preferred_element_type=F32)


def _split2(x):
    hi = x.astype(BF16)
    lo = (x - hi.astype(F32)).astype(BF16)
    return hi, lo


def _split3(x):
    hi = x.astype(BF16)
    r1 = x - hi.astype(F32)
    lo = r1.astype(BF16)
    lo2 = (r1 - lo.astype(F32)).astype(BF16)
    return hi, lo, lo2


def _rmsnorm(x, g):
    ms = jnp.mean(x * x, axis=-1, keepdims=True)
    return x * lax.rsqrt(ms + RMS_EPS) * g


def _cmul(ar, ai, br, bi):
    return ar * br - ai * bi, ar * bi + ai * br


def _s5_prep_kernel(lre_ref, lim_ref, ldt_ref, lre_w_ref, lim_w_ref, ldt_w_ref, bre_ref, bim_ref,
                    a1_re_ref, a1_im_ref, a8_re_ref, a8_im_ref, apb_re_ref, apb_im_ref,
                    alp_re_ref, alp_im_ref, shl_re_ref, shl_im_ref, bbre_ref, bbim_ref, *, seg_len):
    def discretise(lre, lim, ldt):
        lr = jnp.minimum(lre, MIN_NEG_REAL)
        dt = jnp.exp(ldt)
        mag = jnp.exp(lr * dt)
        ab_re = mag * jnp.cos(lim * dt)
        ab_im = mag * jnp.sin(lim * dt)
        den = lr * lr + lim * lim
        xm1 = ab_re - 1.0
        q_re = (xm1 * lr + ab_im * lim) / den
        q_im = (ab_im * lr - xm1 * lim) / den
        return ab_re, ab_im, q_re, q_im

    shape = a1_re_ref.shape
    a_re, a_im, _, _ = discretise(lre_ref[...], lim_ref[...], ldt_ref[...])
    bcast = lambda x: jnp.broadcast_to(x, shape)
    pows = [(a_re, a_im)]
    for _ in range(SUBLANES - 1):
        pows.append(_cmul(*pows[-1], a_re, a_im))
    a1_re_ref[...] = bcast(a_re)
    a1_im_ref[...] = bcast(a_im)
    a8_re_ref[...] = bcast(pows[-1][0])
    a8_im_ref[...] = bcast(pows[-1][1])
    for b, (pr, pi) in enumerate(pows):
        apb_re_ref[b] = bcast(pr)
        apb_im_ref[b] = bcast(pi)
    lr_, li_ = pows[-1]
    n = SUBLANES
    while n < seg_len:
        lr_, li_ = _cmul(lr_, li_, lr_, li_)
        n *= 2
    lpows = [(lr_, li_)]
    for _ in range(SUBLANES - 1):
        lpows.append(_cmul(*lpows[-1], lr_, li_))
    row = lax.broadcasted_iota(jnp.int32, shape, 0)
    alp_re = jnp.zeros(shape, F32)
    alp_im = jnp.zeros(shape, F32)
    for i, (pr, pi) in enumerate(lpows):
        alp_re = jnp.where(row == i, bcast(pr), alp_re)
        alp_im = jnp.where(row == i, bcast(pi), alp_im)
    alp_re_ref[...] = alp_re
    alp_im_ref[...] = alp_im
    for i, s in enumerate((1, 2, 4)):
        pr, pi = lpows[s - 1]
        shl_re_ref[i] = jnp.where(row >= s, bcast(pr), 0.0)
        shl_im_ref[i] = jnp.where(row >= s, bcast(pi), 0.0)

    _, _, q_re, q_im = discretise(lre_w_ref[...], lim_w_ref[...], ldt_w_ref[...])
    br = bre_ref[...]
    bi = bim_ref[...]
    bbre_ref[...] = q_re * br - q_im * bi
    bbim_ref[...] = q_re * bi + q_im * br


def _s5_prep(lam_re, lam_im, log_dt, b_re, b_im, seg_len):
    g, p = lam_re.shape
    m = b_re.shape[-1]
    n = g * p
    row = lambda a: a.reshape(1, n)
    wide = lambda a: jnp.repeat(a, m, axis=-1)
    ldt_gp = jnp.broadcast_to(log_dt[:, None], (g, p))
    s8 = jax.ShapeDtypeStruct((SUBLANES, n), F32)
    s88 = jax.ShapeDtypeStruct((SUBLANES, SUBLANES, n), F32)
    s38 = jax.ShapeDtypeStruct((3, SUBLANES, n), F32)
    sw = jax.ShapeDtypeStruct((g, p * m), F32)
    out_shape = (s8, s8, s8, s8, s88, s88, s8, s8, s38, s38, sw, sw)
    return pl.pallas_call(functools.partial(_s5_prep_kernel, seg_len=seg_len),
                          out_shape=out_shape, name="s5_prep")(
        row(lam_re), row(lam_im), row(ldt_gp), wide(lam_re), wide(lam_im), wide(ldt_gp),
        b_re.reshape(g, p * m), b_im.reshape(g, p * m))


def _block_diag(blocks):
    g, r, c = blocks.shape
    eye = jnp.eye(g, dtype=bool)
    return jnp.where(eye[:, None, :, None], blocks[:, :, None, :], 0.0).reshape(g * r, g * c)


def _head_sum(x, ones):
    rows, width = x.shape
    groups = width // LANES
    stacked = jnp.concatenate([x[:, j * LANES:(j + 1) * LANES] for j in range(groups)], axis=0)
    sums = _dot(stacked.astype(BF16), ones)
    return jnp.concatenate([sums[j * rows:(j + 1) * rows] for j in range(groups)], axis=1)


def _mix_in_kernel(*refs, d_ssm, d, tiles_per_seq, tt_s5):
    (x_ref, g_ref, w_ref, mu_ref, wwa_ref, g2_ref, w0_ref, a0_ref, kk_ref, ka_ref, rk_ref,
     ones_ref) = refs[:12]
    s5 = dict(zip(S5_CONSTS, refs[12:12 + len(S5_CONSTS)]))
    (oa_ref, r_ref, k2_ref, v_ref, av_ref, bv_ref, logw_ref, gate_ref, bonus_ref,
     prev_ref, carry_ref, *xs_refs) = refs[12 + len(S5_CONSTS):]
    tm = x_ref.shape[0]

    @pl.when(pl.program_id(0) % tiles_per_seq == 0)
    def _():
        prev_ref[...] = jnp.zeros_like(prev_ref)
        carry_ref[...] = jnp.zeros_like(carry_ref)

    h = _rmsnorm(x_ref[...], g_ref[...]).astype(BF16)

    def mixed(c0, c1):
        p = _dot(h, w_ref[:, d_ssm + c0:d_ssm + c1])
        row = lax.broadcasted_iota(jnp.int32, p.shape, 0)
        last = prev_ref[SUBLANES - 1:SUBLANES, c0:c1]
        prev = jnp.where(row == 0, jnp.broadcast_to(last, p.shape), pltpu.roll(p, 1, 0))
        prev_ref[:, c0:c1] = p[tm - SUBLANES:, :]
        return p + (prev - p) * mu_ref[:, c0:c1]

    sub = lambda i: slice(i * tt_s5, (i + 1) * tt_s5)
    n_sub = tm // tt_s5
    us = [_dot(h[sub(i)], w_ref[:, :d_ssm]) for i in range(n_sub)]
    for i in range(n_sub):
        _s5_project(us[i], s5, xs_refs[i])
    _s5_scan(s5, xs_refs[0], carry_ref)
    lora_in = mixed(3 * d, 3 * d + GATE_LORA + DECAY_LORA + AAA_LORA)
    k = mixed(d, 2 * d)
    xg = lora_in[:, :GATE_LORA]
    xwa = lora_in[:, GATE_LORA:]
    lane = lax.broadcasted_iota(jnp.int32, xwa.shape, 1)
    zwa = jnp.where(lane < DECAY_LORA, jnp.tanh(xwa), xwa)
    lora = _dot(zwa.astype(BF16), wwa_ref[...])
    gate_ref[...] = _dot(jax.nn.sigmoid(xg).astype(BF16), g2_ref[...])
    logw_ref[...] = -EXP_NEG_HALF * jax.nn.sigmoid(w0_ref[...] + lora[:, :d])
    a = jax.nn.sigmoid(a0_ref[...] + lora[:, d:])
    for i in range(1, n_sub):
        _s5_scan(s5, xs_refs[i], carry_ref)
    oa_ref[sub(0), :] = _s5_readout(us[0], s5, xs_refs[0])

    ones = ones_ref[...]
    r = mixed(0, d)
    kk = k * kk_ref[...]
    kk = kk * lax.rsqrt(jnp.maximum(_head_sum(kk * kk, ones), L2_EPS * L2_EPS))
    v = mixed(2 * d, 3 * d)
    k2 = k * (1.0 + (a - 1.0) * ka_ref[...])
    k2_ref[...] = k2.astype(BF16)
    av_ref[...] = (-kk).astype(BF16)
    bv_ref[...] = (kk * a).astype(BF16)
    r_ref[...] = r.astype(BF16)
    v_ref[...] = v.astype(BF16)
    bonus_ref[...] = _head_sum(r * k2 * rk_ref[...], ones) * v
    for i in range(1, n_sub):
        oa_ref[sub(i), :] = _s5_readout(us[i], s5, xs_refs[i])


def _mix_in(x2, seq, norm_g, w_mix, d_ssm, mu, w0, w2, a0, a2, g2, k_k, k_a, r_k, s5_params):
    t, dm = x2.shape
    n = w_mix.shape[1]
    d = w0.shape[0]
    tm = min(TM_PROJ, seq)
    tt_s5 = min(TT_S5, tm)
    s5_consts = _s5_constants(tt_s5, *s5_params)
    n_state2 = s5_consts[2].shape[2]
    wwa = jnp.concatenate([jnp.concatenate([w2, jnp.zeros_like(w2)], axis=1),
                           jnp.concatenate([jnp.zeros_like(a2), a2], axis=1)], axis=0)
    head = jnp.arange(LANES) // RWKV_HEAD
    ones = (head[:, None] == head[None, :]).astype(BF16)
    vec = lambda a: a.reshape(1, -1)
    consts = [vec(norm_g), w_mix, vec(mu), wwa.astype(BF16), g2.astype(BF16), vec(w0), vec(a0),
              vec(k_k), vec(k_a), vec(r_k), ones, *s5_consts]
    whole = lambda a: pl.BlockSpec(a.shape, lambda i: (0,) * a.ndim)
    tok = lambda i: (i, 0)
    widths = [(d_ssm, F32)] + [(d, BF16)] * 5 + [(d, F32)] * 3
    return pl.pallas_call(
        functools.partial(_mix_in_kernel, d_ssm=d_ssm, d=d, tiles_per_seq=seq // tm, tt_s5=tt_s5),
        grid=(t // tm,),
        in_specs=[pl.BlockSpec((tm, dm), tok)] + [whole(c) for c in consts],
        out_specs=[pl.BlockSpec((tm, w), tok) for w, _ in widths],
        out_shape=[jax.ShapeDtypeStruct((t, w), dt) for w, dt in widths],
        scratch_shapes=[pltpu.VMEM((SUBLANES, n - d_ssm), F32),
                        pltpu.VMEM((SUBLANES, 2 * n_state2), F32)]
        + [pltpu.VMEM((tt_s5, 2 * n_state2), F32)] * (tm // tt_s5),
        compiler_params=pltpu.CompilerParams(dimension_semantics=("arbitrary",),
                                             vmem_limit_bytes=VMEM_LIMIT),
        name="mix_in",
    )(x2, *consts)


S5_CONSTS = ("perm", "permt", "bmat", "cmat", "a1_re", "a1_im", "a8_re", "a8_im", "apb_re", "apb_im",
             "alp_re", "alp_im", "shl_re", "shl_im", "d_skip", "w_glu", "b_glu")


def _s5_project(u, s5, xs_ref):
    n_state = xs_ref.shape[1] // 2
    half = n_state // 2
    lw = S5_LANE_CHUNK
    ub = _dot(s5["perm"][...], u.astype(BF16)).astype(BF16)
    d_half = ub.shape[1] // 2
    for c in range(n_state // lw):
        h, cc = divmod(c, half // lw)
        for part in range(2):
            xs_ref[:, pl.ds(part * n_state + c * lw, lw)] = _dot(
                ub[:, h * d_half:(h + 1) * d_half],
                s5["bmat"][h, :, pl.ds(part * half + cc * lw, lw)])


def _s5_scan(s5, xs_ref, carry_ref):
    tt = xs_ref.shape[0]
    n_state = xs_ref.shape[1] // 2
    seg_len = tt // SUBLANES
    lw = S5_LANE_CHUNK
    blk = lambda j: pl.ds(j * SUBLANES, SUBLANES)
    for c in range(n_state // lw):
        re_sl, im_sl = pl.ds(c * lw, lw), pl.ds(n_state + c * lw, lw)
        ar, ai = s5["a1_re"][:, re_sl], s5["a1_im"][:, re_sl]
        xr = jnp.zeros((SUBLANES, lw), F32)
        xi = jnp.zeros((SUBLANES, lw), F32)
        for j in range(seg_len):
            br = xs_ref[blk(j), re_sl]
            bi = xs_ref[blk(j), im_sl]
            xr, xi = ar * xr - ai * xi + br, ar * xi + ai * xr + bi
            xs_ref[blk(j), re_sl] = xr
            xs_ref[blk(j), im_sl] = xi
        er, ei = xr, xi
        for k, sh in enumerate((1, 2, 4)):
            pr, pi = s5["shl_re"][k, :, re_sl], s5["shl_im"][k, :, re_sl]
            sr, si = pltpu.roll(er, sh, 0), pltpu.roll(ei, sh, 0)
            er, ei = er + (pr * sr - pi * si), ei + (pr * si + pi * sr)
        tr, ti = carry_ref[:, re_sl], carry_ref[:, im_sl]
        pr, pi = s5["alp_re"][:, re_sl], s5["alp_im"][:, re_sl]
        fr = er + (pr * tr - pi * ti)
        fi = ei + (pr * ti + pi * tr)
        row = lax.broadcasted_iota(jnp.int32, fr.shape, 0)
        dr = jnp.where(row == 0, tr, pltpu.roll(fr, 1, 0))
        di = jnp.where(row == 0, ti, pltpu.roll(fi, 1, 0))
        last = SUBLANES - 1
        carry_ref[:, re_sl] = jnp.broadcast_to(fr[last:last + 1], fr.shape)
        carry_ref[:, im_sl] = jnp.broadcast_to(fi[last:last + 1], fi.shape)
        a8r, a8i = s5["a8_re"][:, re_sl], s5["a8_im"][:, re_sl]
        for a in range(seg_len // SUBLANES):
            for b in range(SUBLANES):
                j = a * SUBLANES + b
                pr, pi = s5["apb_re"][b, :, re_sl], s5["apb_im"][b, :, re_sl]
                xs_ref[blk(j), re_sl] += pr * dr - pi * di
                xs_ref[blk(j), im_sl] += pr * di + pi * dr
            dr, di = a8r * dr - a8i * di, a8r * di + a8i * dr


def _s5_readout(u, s5, xs_ref):
    n_state = xs_ref.shape[1] // 2
    half = n_state // 2
    lw = S5_LANE_CHUNK
    n_chunks = n_state // lw
    width = u.shape[1] // n_chunks
    blocks = []
    for c in range(n_chunks):
        h, cc = divmod(c, half // lw)
        cols = pl.ds(cc * width, width)
        blocks.append(sum(_dot(xs_ref[:, pl.ds(part * n_state + c * lw, lw)].astype(BF16),
                               s5["cmat"][h, pl.ds(part * half + cc * lw, lw), cols])
                          for part in range(2)))
    cx_hi, cx_lo = _split2(jnp.concatenate(blocks, axis=1))
    permt = s5["permt"][...]
    y = _dot(permt, cx_hi) + _dot(permt, cx_lo) + s5["d_skip"][...] * u
    z = jax.nn.gelu(y)
    gate = jax.nn.sigmoid(_dot(z.astype(BF16), s5["w_glu"][...]) + s5["b_glu"][...])
    return z * gate


def _s5_constants(tt, lam_re, lam_im, log_dt, b_re, b_im, c_re, c_im, d_skip, w_glu, b_glu):
    d_ssm = d_skip.shape[0]
    n_grp, n_state, grp = b_re.shape
    seg_len = tt // SUBLANES
    *scan_consts, bb_re, bb_im = _s5_prep(lam_re, lam_im, log_dt, b_re, b_im, seg_len)
    to_blocks = lambda bb: jnp.swapaxes(bb.reshape(n_grp, n_state, grp), 1, 2)
    hg = n_grp // 2
    halves = lambda blocks: [_block_diag(blocks[h * hg:(h + 1) * hg]) for h in range(2)]
    b_re_h, b_im_h = halves(to_blocks(bb_re)), halves(to_blocks(bb_im))
    c_re_h, c_im_h = halves(jnp.swapaxes(c_re, 1, 2)), halves(jnp.swapaxes(-c_im, 1, 2))
    bmat = jnp.stack([jnp.concatenate([b_re_h[h], b_im_h[h]], axis=1)
                      for h in range(2)]).astype(BF16)
    cmat = jnp.stack([jnp.concatenate([c_re_h[h], c_im_h[h]], axis=0)
                      for h in range(2)]).astype(BF16)
    rho = jnp.arange(tt)
    src = (rho % SUBLANES) * seg_len + rho // SUBLANES
    perm = (src[:, None] == jnp.arange(tt)[None, :]).astype(BF16)
    return [perm, perm.T, bmat, cmat, *scan_consts, d_skip.reshape(1, d_ssm), w_glu.astype(BF16),
            b_glu.reshape(1, d_ssm)]


def _pair_blocks(x):
    lane = lax.broadcasted_iota(jnp.int32, x.shape, 1)
    first = lane < RWKV_HEAD
    return jnp.concatenate([jnp.where(first, x, 0.0), jnp.where(first, 0.0, x)], axis=0)


def _pair_transpose(x):
    t = _pair_blocks(x).T
    return t[:CHUNK] + t[CHUNK:]


def _rwkv_chunks_scores(items):
    shape = (CHUNK, LANES)
    row = lax.broadcasted_iota(jnp.int32, shape, 0)
    col = lax.broadcasted_iota(jnp.int32, shape, 1) % RWKV_HEAD
    strict = row > col
    incl = row >= col
    eye = row == col
    blocks = lambda x: _pair_blocks(x).astype(BF16)

    for it in items:
        lhs = jnp.concatenate([it["at"], it["rt"]], axis=0).astype(BF16)
        rhs = jnp.concatenate([blocks(it["bt"]), blocks(it["kt"])], axis=0)
        sc = _dot_nt(lhs, rhs)
        it["s_ab"] = jnp.where(strict, sc[:CHUNK, :LANES], 0.0)
        it["s_ak"] = jnp.where(strict, sc[:CHUNK, LANES:], 0.0).astype(BF16)
        it["s_r"] = jnp.concatenate([jnp.where(incl, sc[CHUNK:, :LANES], 0.0),
                                     jnp.where(incl, sc[CHUNK:, LANES:], 0.0)], axis=1).astype(BF16)
        it["bd_v"] = blocks(it["v"])
        it["bk_t"] = jnp.concatenate([_pair_transpose(it["bh"]), _pair_transpose(it["kh"])],
                                     axis=1).astype(BF16)

    return strict, incl, eye, blocks


def _rwkv_chunks_rest(items, h_ref, eye, blocks):
    shape = (CHUNK, LANES)
    for it in items:
        it["npow"] = _dot(it["s_ab"].astype(BF16), blocks(it["s_ab"]))
        it["inv"] = jnp.where(eye, 1.0, 0.0) + it["s_ab"]
    for step in range(4):
        for it in items:
            both = _dot(jnp.concatenate([it["inv"], it["npow"]], axis=0).astype(BF16),
                        blocks(it["npow"]))
            it["inv"] = it["inv"] + both[:CHUNK]
            it["npow"] = both[CHUNK:]
    for it in items:
        it["inv"] = it["inv"] + _dot(it["inv"].astype(BF16), blocks(it["npow"]))

    for it in items:
        it["av"] = _dot(it["s_ak"], it["bd_v"])
    for it in items:
        wu = _dot(it["inv"].astype(BF16),
                  jnp.concatenate([blocks(it["at"]), blocks(it["av"])], axis=1))
        it["z"] = jnp.concatenate(
            [jnp.concatenate([blocks(wu[:, :LANES]), blocks(wu[:, LANES:])], axis=1),
             jnp.concatenate([jnp.zeros((LANES, LANES), BF16), it["bd_v"]], axis=1)], axis=0)
    for it in items:
        both = _dot(jnp.concatenate([it["s_r"], it["bk_t"]], axis=0), it["z"])
        qy, mg = both[:CHUNK], both[CHUNK:]
        m = jnp.where(eye, jnp.broadcast_to(it["decay"], shape), 0.0) + mg[:, :LANES]
        it["qm"] = jnp.concatenate([it["rt"] + qy[:, :LANES], m], axis=0).astype(BF16)
        it["yp"] = qy[:, LANES:]
        it["g"] = mg[:, LANES:]

    for it in items:
        both = _dot(it["qm"], blocks(h_ref[it["pair"]]))
        it["y"] = both[:CHUNK] + it["yp"]
        h_ref[it["pair"]] = both[CHUNK:] + it["g"]


OPERANDS = ("at", "rt", "bt", "kt", "bh", "kh", "v")


def _rwkv_kernel(r_ref, k2_ref, v_ref, av_ref, bv_ref, logw_ref, gate_ref, bonus_ref, lng_ref,
                 lnb_ref, ones_ref, ltri_ref, o_ref, h_ref, decay_ref, *op_refs):
    tt, d = r_ref.shape
    n_pairs = d // LANES
    n_chunks = tt // CHUNK
    ops = dict(zip(OPERANDS, op_refs))

    @pl.when(pl.program_id(1) == 0)
    def _():
        for ref in (h_ref, decay_ref) + tuple(op_refs):
            ref[...] = jnp.zeros_like(ref)

    items = []
    for c in range(n_chunks):
        rows = pl.ds(c * CHUNK, CHUNK)
        for j in range(n_pairs):
            ln = pl.ds(j * LANES, LANES)
            item = {name: ref[rows, ln] for name, ref in ops.items()}
            item["decay"] = decay_ref[c, :, ln]
            item["pair"] = j
            items.append(item)
    _, _, eye, blocks = _rwkv_chunks_scores(items)

    for c in range(n_chunks):
        rows = pl.ds(c * CHUNK, CHUNK)
        load = lambda ref: ref[rows, :].astype(F32)
        r, k2, v, av, bv, logw = map(load, (r_ref, k2_ref, v_ref, av_ref, bv_ref, logw_ref))
        l_hi, l_lo = _split2(logw)
        lc = _dot(ltri_ref[...], l_hi) + _dot(ltri_ref[...], l_lo)
        lc_end = lc[CHUNK - 1:CHUNK]
        e_in = jnp.exp(lc - logw)
        e_inc = jnp.exp(lc)
        e_neg = jnp.exp(-lc)
        e_out = jnp.exp(lc_end - lc)
        new = dict(at=av * e_in, rt=r * e_inc, bt=bv * e_neg, kt=k2 * e_neg, bh=bv * e_out,
                   kh=k2 * e_out, v=v)
        for name, val in new.items():
            ops[name][rows, :] = val
        decay_ref[c] = jnp.exp(lc_end)

    _rwkv_chunks_rest(items, h_ref, eye, blocks)
    ones = ones_ref[...]
    inv_n = 1.0 / RWKV_HEAD
    for c in range(n_chunks):
        rows = pl.ds(c * CHUNK, CHUNK)
        y = jnp.concatenate([it["y"] for it in items[c * n_pairs:(c + 1) * n_pairs]], axis=1)
        mean = _head_sum(y, ones) * inv_n
        yc = y - mean
        var = _head_sum(yc * yc, ones) * inv_n
        yn = yc * lax.rsqrt(var + GN_EPS) * lng_ref[...] + lnb_ref[...]
        o_ref[rows, :] = (yn + bonus_ref[rows, :]) * gate_ref[rows, :]


def _rwkv_branch(mix, bsz, lnx_g, lnx_b):
    *step_in, gate, bonus = mix
    t, d = gate.shape
    seq = t // bsz
    tt = min(TT_RWKV, seq)
    nt = seq // tt
    head = jnp.arange(LANES) // RWKV_HEAD
    ones = (head[:, None] == head[None, :]).astype(BF16)
    tok_i = jnp.arange(CHUNK)
    ltri = (tok_i[:, None] >= tok_i[None, :]).astype(BF16)
    this_tile = lambda b, i: (b * nt + jnp.minimum(i, nt - 1), 0)
    prev_tile = lambda b, i: (b * nt + jnp.maximum(i - 1, 0), 0)
    c2 = lambda b, i: (0, 0)
    consts = [lnx_g.reshape(1, d), lnx_b.reshape(1, d), ones, ltri]
    return pl.pallas_call(
        _rwkv_kernel,
        grid=(bsz, nt + 1),
        in_specs=[pl.BlockSpec((tt, d), this_tile) for _ in step_in]
        + [pl.BlockSpec((tt, d), prev_tile)] * 2 + [pl.BlockSpec(c.shape, c2) for c in consts],
        out_specs=pl.BlockSpec((tt, d), prev_tile),
        out_shape=jax.ShapeDtypeStruct((t, d), F32),
        scratch_shapes=[pltpu.VMEM((d // LANES, RWKV_HEAD, LANES), F32),
                        pltpu.VMEM((tt // CHUNK, 1, d), F32)]
        + [pltpu.VMEM((tt, d), F32)] * len(OPERANDS),
        compiler_params=pltpu.CompilerParams(dimension_semantics=("arbitrary", "arbitrary"),
                                             vmem_limit_bytes=VMEM_LIMIT),
        name="rwkv",
    )(*step_in, gate, bonus, *consts)


def _tail_kernel(x_ref, oa_ref, ob_ref, g1_ref, wg_ref, wba_ref, wbb_ref, wout_ref, g2_ref,
                 w1_ref, w2_ref, gf_ref, o_ref, *, ff_block):
    x = x_ref[...]
    d = x.shape[1]
    h = _rmsnorm(x, g1_ref[...]).astype(BF16)
    gates = jax.nn.sigmoid(_dot(h, wg_ref[...]))
    ma = _dot(oa_ref[...].astype(BF16), wba_ref[...])
    mb = _dot(ob_ref[...].astype(BF16), wbb_ref[...])
    merged = gates[:, :d] * ma + gates[:, d:] * mb
    x1 = x + _dot(merged.astype(BF16), wout_ref[...])
    h2 = _rmsnorm(x1, g2_ref[...]).astype(BF16)
    acc = x1
    for j in range(w1_ref.shape[1] // ff_block):
        cols = slice(j * ff_block, (j + 1) * ff_block)
        hid = jnp.square(jnp.maximum(_dot(h2, w1_ref[:, cols]), 0.0))
        acc = acc + _dot(hid.astype(BF16), w2_ref[cols, :])
    o_ref[...] = _rmsnorm(acc, gf_ref[...])


def _tail(x2, o_a, o_b, norm1_g, w_gate, wb_a, wb_b, w_out, norm2_g, w_ff1, w_ff2, norm_f_g):
    t, d = x2.shape
    d_mix = o_a.shape[1]
    tm = min(TM_PROJ, t)
    tok = lambda i: (i, 0)
    const = lambda i: (0, 0)
    resident = lambda a: pl.BlockSpec(a.shape, const, pipeline_mode=pl.Buffered(1))
    vec = lambda a: a.reshape(1, d)
    return pl.pallas_call(
        functools.partial(_tail_kernel, ff_block=d),
        grid=(t // tm,),
        in_specs=[pl.BlockSpec((tm, d), tok), pl.BlockSpec((tm, d_mix), tok),
                  pl.BlockSpec((tm, d_mix), tok), pl.BlockSpec((1, d), const),
                  resident(w_gate), resident(wb_a), resident(wb_b), resident(w_out),
                  pl.BlockSpec((1, d), const), resident(w_ff1), resident(w_ff2),
                  pl.BlockSpec((1, d), const)],
        out_specs=pl.BlockSpec((tm, d), tok),
        out_shape=jax.ShapeDtypeStruct((t, d), F32),
        compiler_params=pltpu.CompilerParams(dimension_semantics=("arbitrary",),
                                             vmem_limit_bytes=VMEM_LIMIT),
        name="tail",
    )(x2, o_a, o_b, vec(norm1_g), w_gate, wb_a, wb_b, w_out, vec(norm2_g), w_ff1, w_ff2,
      vec(norm_f_g))


def _layer(x2, bsz, norm1_g, w_in, lam_re, lam_im, log_dt, b_re, b_im, c_re, c_im, d_skip, w_glu,
           b_glu, mu_rwkv, w0, w2, a0, a2, g2, k_k, k_a, r_k, lnx_g, lnx_b, w_branch, w_out,
           norm2_g, w_ff1, w_ff2, out_g):
    d_ssm = d_skip.shape[0]
    d_mix_in = d_ssm + mu_rwkv.shape[0]
    o_a, *mix = _mix_in(x2, x2.shape[0] // bsz, norm1_g, w_in[:, :d_mix_in].astype(BF16), d_ssm,
                        mu_rwkv, w0, w2, a0, a2, g2, k_k, k_a, r_k.reshape(-1),
                        (lam_re, lam_im, log_dt, b_re, b_im, c_re, c_im, d_skip, w_glu, b_glu))
    o_b = _rwkv_branch(mix, bsz, lnx_g, lnx_b)
    return _tail(x2, o_a, o_b, norm1_g, w_in[:, d_mix_in:].astype(BF16),
                 w_branch[:d_ssm].astype(BF16), w_branch[d_ssm:].astype(BF16),
                 w_out.astype(BF16), norm2_g, w_ff1.astype(BF16), w_ff2.astype(BF16), out_g)


def kernel(x, norm1_g, w_in, lam_re, lam_im, log_dt, b_re, b_im, c_re, c_im, d_skip, w_glu, b_glu,
           mu_rwkv, w0, w2, a0, a2, g2, k_k, k_a, r_k, lnx_g, lnx_b, w_branch, w_out, norm2_g,
           w_ff1, w_ff2, norm_f_g):
    bsz, seq, d = x.shape
    depth = norm1_g.shape[0]
    assert depth == 1, "the fused tail applies the final norm, so exactly one layer is supported"
    x2 = x.reshape(bsz * seq, d)
    l = 0
    out = _layer(x2, bsz, norm1_g[l], w_in[l], lam_re[l], lam_im[l], log_dt[l], b_re[l], b_im[l],
                 c_re[l], c_im[l], d_skip[l], w_glu[l], b_glu[l], mu_rwkv[l], w0[l], w2[l], a0[l],
                 a2[l], g2[l], k_k[l], k_a[l], r_k[l], lnx_g[l], lnx_b[l], w_branch[l], w_out[l],
                 norm2_g[l], w_ff1[l], w_ff2[l], norm_f_g)
    return out.reshape(bsz, seq, d)
```

```python
import functools

import jax
import jax.numpy as jnp
from jax import lax
from jax.experimental import pallas as pl
from jax.experimental.pallas import tpu as pltpu

F32 = jnp.float32
BF16 = jnp.bfloat16

RMS_EPS = 1e-6
GN_EPS = 64e-5
L2_EPS = 1e-12
MIN_NEG_REAL = -1e-4
EXP_NEG_HALF = 0.6065306597126334

SSM_GROUP = 16
SSM_STATE = 64
RWKV_HEAD = 64
GATE_LORA = 128
DECAY_LORA = 64
AAA_LORA = 64
CHUNK = 64

SUBLANES = 8
LANES = 128
VMEM_LIMIT = 56 * 1024 * 1024

TM_PROJ = 512
TT_S5 = 256
TT_RWKV = 256
S5_LANE_CHUNK = 512


def _dot(a, b):
    return jnp.dot(a, b, preferred_element_type=F32)


def _dot_nt(a, b):
    return lax.dot_general(a, b, (((1,), (1,)), ((), ())), preferred_element_type=F32)


def _split2(x):
    hi = x.astype(BF16)
    lo = (x - hi.astype(F32)).astype(BF16)
    return hi, lo


def _split3(x):
    hi = x.astype(BF16)
    r1 = x - hi.astype(F32)
    lo = r1.astype(BF16)
    lo2 = (r1 - lo.astype(F32)).astype(BF16)
    return hi, lo, lo2


def _rmsnorm(x, g):
    ms = jnp.mean(x * x, axis=-1, keepdims=True)
    return x * lax.rsqrt(ms + RMS_EPS) * g


def _cmul(ar, ai, br, bi):
    return ar * br - ai * bi, ar * bi + ai * br


def _s5_prep_kernel(lre_ref, lim_ref, ldt_ref, lre_w_ref, lim_w_ref, ldt_w_ref, bre_ref, bim_ref,
                    a1_re_ref, a1_im_ref, a8_re_ref, a8_im_ref, apb_re_ref, apb_im_ref,
                    alp_re_ref, alp_im_ref, shl_re_ref, shl_im_ref, bbre_ref, bbim_ref, *, seg_len):
    def discretise(lre, lim, ldt):
        lr = jnp.minimum(lre, MIN_NEG_REAL)
        dt = jnp.exp(ldt)
        mag = jnp.exp(lr * dt)
        ab_re = mag * jnp.cos(lim * dt)
        ab_im = mag * jnp.sin(lim * dt)
        den = lr * lr + lim * lim
        xm1 = ab_re - 1.0
        q_re = (xm1 * lr + ab_im * lim) / den
        q_im = (ab_im * lr - xm1 * lim) / den
        return ab_re, ab_im, q_re, q_im

    shape = a1_re_ref.shape
    a_re, a_im, _, _ = discretise(lre_ref[...], lim_ref[...], ldt_ref[...])
    bcast = lambda x: jnp.broadcast_to(x, shape)
    pows = [(a_re, a_im)]
    for _ in range(SUBLANES - 1):
        pows.append(_cmul(*pows[-1], a_re, a_im))
    a1_re_ref[...] = bcast(a_re)
    a1_im_ref[...] = bcast(a_im)
    a8_re_ref[...] = bcast(pows[-1][0])
    a8_im_ref[...] = bcast(pows[-1][1])
    shape16 = (2 * SUBLANES,) + shape[1:]
    first = lax.broadcasted_iota(jnp.int32, shape16, 0) < SUBLANES
    for q in range(SUBLANES // 2):
        (r0, i0), (r1, i1) = pows[2 * q], pows[2 * q + 1]
        bc16 = lambda x: jnp.broadcast_to(x, shape16)
        apb_re_ref[q] = jnp.where(first, bc16(r0), bc16(r1)).astype(BF16)
        apb_im_ref[q] = jnp.where(first, bc16(i0), bc16(i1)).astype(BF16)
    lr_, li_ = pows[-1]
    n = SUBLANES
    while n < seg_len:
        lr_, li_ = _cmul(lr_, li_, lr_, li_)
        n *= 2
    lpows = [(lr_, li_)]
    for _ in range(SUBLANES - 1):
        lpows.append(_cmul(*lpows[-1], lr_, li_))
    row = lax.broadcasted_iota(jnp.int32, shape, 0)
    alp_re = jnp.zeros(shape, F32)
    alp_im = jnp.zeros(shape, F32)
    for i, (pr, pi) in enumerate(lpows):
        alp_re = jnp.where(row == i, bcast(pr), alp_re)
        alp_im = jnp.where(row == i, bcast(pi), alp_im)
    alp_re_ref[...] = alp_re
    alp_im_ref[...] = alp_im
    for i, s in enumerate((1, 2, 4)):
        pr, pi = lpows[s - 1]
        shl_re_ref[i] = jnp.where(row >= s, bcast(pr), 0.0)
        shl_im_ref[i] = jnp.where(row >= s, bcast(pi), 0.0)

    _, _, q_re, q_im = discretise(lre_w_ref[...], lim_w_ref[...], ldt_w_ref[...])
    br = bre_ref[...]
    bi = bim_ref[...]
    bbre_ref[...] = q_re * br - q_im * bi
    bbim_ref[...] = q_re * bi + q_im * br


def _s5_prep(lam_re, lam_im, log_dt, b_re, b_im, seg_len):
    g, p = lam_re.shape
    m = b_re.shape[-1]
    n = g * p
    row = lambda a: a.reshape(1, n)
    wide = lambda a: jnp.repeat(a, m, axis=-1)
    ldt_gp = jnp.broadcast_to(log_dt[:, None], (g, p))
    s8 = jax.ShapeDtypeStruct((SUBLANES, n), F32)
    s88 = jax.ShapeDtypeStruct((SUBLANES // 2, 2 * SUBLANES, n), BF16)
    s38 = jax.ShapeDtypeStruct((3, SUBLANES, n), F32)
    sw = jax.ShapeDtypeStruct((g, p * m), F32)
    out_shape = (s8, s8, s8, s8, s88, s88, s8, s8, s38, s38, sw, sw)
    return pl.pallas_call(functools.partial(_s5_prep_kernel, seg_len=seg_len),
                          out_shape=out_shape, name="s5_prep")(
        row(lam_re), row(lam_im), row(ldt_gp), wide(lam_re), wide(lam_im), wide(ldt_gp),
        b_re.reshape(g, p * m), b_im.reshape(g, p * m))


def _block_diag(blocks):
    g, r, c = blocks.shape
    eye = jnp.eye(g, dtype=bool)
    return jnp.where(eye[:, None, :, None], blocks[:, :, None, :], 0.0).reshape(g * r, g * c)


def _head_sum(x, ones):
    rows, width = x.shape
    groups = width // LANES
    stacked = jnp.concatenate([x[:, j * LANES:(j + 1) * LANES] for j in range(groups)], axis=0)
    sums = _dot(stacked.astype(BF16), ones)
    return jnp.concatenate([sums[j * rows:(j + 1) * rows] for j in range(groups)], axis=1)


def _mix_in_kernel(*refs, d_ssm, d, tiles_per_seq, tt_s5):
    (x_ref, g_ref, w_ref, mu_ref, wwa_ref, g2_ref, w0_ref, a0_ref, kk_ref, ka_ref, rk_ref,
     ones_ref) = refs[:12]
    s5 = dict(zip(S5_CONSTS, refs[12:12 + len(S5_CONSTS)]))
    (oa_ref, r_ref, k2_ref, v_ref, av_ref, bv_ref, logw_ref, gate_ref, bonus_ref,
     prev_ref, carry_ref, *state_refs) = refs[12 + len(S5_CONSTS):]
    xs_refs, xb_refs = state_refs[:len(state_refs) // 2], state_refs[len(state_refs) // 2:]
    tm = x_ref.shape[0]

    @pl.when(pl.program_id(0) % tiles_per_seq == 0)
    def _():
        prev_ref[...] = jnp.zeros_like(prev_ref)
        carry_ref[...] = jnp.zeros_like(carry_ref)

    h = _rmsnorm(x_ref[...], g_ref[...]).astype(BF16)

    def mixed(c0, c1):
        p = _dot(h, w_ref[:, d_ssm + c0:d_ssm + c1])
        shape8 = (SUBLANES, c1 - c0)
        row = lax.broadcasted_iota(jnp.int32, shape8, 0)
        last = prev_ref[SUBLANES - 1:SUBLANES, c0:c1]
        prev = pltpu.roll(p, 1, 0)
        head = jnp.where(row == 0, jnp.broadcast_to(last, shape8), prev[:SUBLANES])
        prev = jnp.concatenate([head, prev[SUBLANES:]], axis=0)
        prev_ref[:, c0:c1] = p[tm - SUBLANES:, :]
        return p + (prev - p) * mu_ref[:, c0:c1]

    sub = lambda i: slice(i * tt_s5, (i + 1) * tt_s5)
    n_sub = tm // tt_s5
    us = [_dot(h[sub(i)], w_ref[:, :d_ssm]) for i in range(n_sub)]
    for i in range(n_sub):
        _s5_project(us[i], s5, xs_refs[i])
    _s5_scan(s5, xs_refs[0], xb_refs[0], carry_ref)
    lora_in = mixed(3 * d, 3 * d + GATE_LORA + DECAY_LORA + AAA_LORA)
    k = mixed(d, 2 * d)
    xg = lora_in[:, :GATE_LORA]
    xwa = lora_in[:, GATE_LORA:]
    lane = lax.broadcasted_iota(jnp.int32, xwa.shape, 1)
    zwa = jnp.where(lane < DECAY_LORA, jnp.tanh(xwa), xwa)
    lora = _dot(zwa.astype(BF16), wwa_ref[...])
    gate_ref[...] = _dot(jax.nn.sigmoid(xg).astype(BF16), g2_ref[...])
    logw_ref[...] = -EXP_NEG_HALF * jax.nn.sigmoid(w0_ref[...] + lora[:, :d])
    a = jax.nn.sigmoid(a0_ref[...] + lora[:, d:])
    for i in range(1, n_sub):
        _s5_scan(s5, xs_refs[i], xb_refs[i], carry_ref)
    oa_ref[sub(0), :] = _s5_readout(us[0], s5, xb_refs[0])

    ones = ones_ref[...]
    r = mixed(0, d)
    kk = k * kk_ref[...]
    kk = kk * lax.rsqrt(jnp.maximum(_head_sum(kk * kk, ones), L2_EPS * L2_EPS))
    v = mixed(2 * d, 3 * d)
    k2 = k * (1.0 + (a - 1.0) * ka_ref[...])
    k2_ref[...] = k2.astype(BF16)
    av_ref[...] = (-kk).astype(BF16)
    bv_ref[...] = (kk * a).astype(BF16)
    r_ref[...] = r.astype(BF16)
    v_ref[...] = v.astype(BF16)
    bonus_ref[...] = _head_sum(r * k2 * rk_ref[...], ones) * v
    for i in range(1, n_sub):
        oa_ref[sub(i), :] = _s5_readout(us[i], s5, xb_refs[i])


def _mix_in(x2, seq, norm_g, w_in, d_ssm, mu, w0, w2, a0, a2, g2, k_k, k_a, r_k, s5_params):
    t, dm = x2.shape
    d = w0.shape[0]
    tm = min(TM_PROJ, seq)
    tt_s5 = min(TT_S5, tm)
    s5_consts = _s5_constants(tt_s5, *s5_params)
    n_state2 = s5_consts[2].shape[2]
    wwa = jnp.concatenate([jnp.concatenate([w2, jnp.zeros_like(w2)], axis=1),
                           jnp.concatenate([jnp.zeros_like(a2), a2], axis=1)], axis=0)
    head = jnp.arange(LANES) // RWKV_HEAD
    ones = (head[:, None] == head[None, :]).astype(BF16)
    vec = lambda a: a.reshape(1, -1)
    consts = [vec(norm_g), w_in, vec(mu), wwa.astype(BF16), g2.astype(BF16), vec(w0), vec(a0),
              vec(k_k), vec(k_a), vec(r_k), ones, *s5_consts]
    whole = lambda a: pl.BlockSpec(a.shape, lambda i: (0,) * a.ndim)
    tok = lambda i: (i, 0)
    widths = [(d_ssm, F32)] + [(d, BF16)] * 5 + [(d, F32)] * 3
    return pl.pallas_call(
        functools.partial(_mix_in_kernel, d_ssm=d_ssm, d=d, tiles_per_seq=seq // tm, tt_s5=tt_s5),
        grid=(t // tm,),
        in_specs=[pl.BlockSpec((tm, dm), tok)] + [whole(c) for c in consts],
        out_specs=[pl.BlockSpec((tm, w), tok) for w, _ in widths],
        out_shape=[jax.ShapeDtypeStruct((t, w), dt) for w, dt in widths],
        scratch_shapes=[pltpu.VMEM((SUBLANES, mu.shape[0]), F32),
                        pltpu.VMEM((SUBLANES, 2 * n_state2), F32)]
        + [pltpu.VMEM((tt_s5, 2 * n_state2), F32)] * (tm // tt_s5)
        + [pltpu.VMEM((tt_s5, 2 * n_state2), BF16)] * (tm // tt_s5),
        compiler_params=pltpu.CompilerParams(dimension_semantics=("arbitrary",),
                                             vmem_limit_bytes=VMEM_LIMIT),
        name="mix_in",
    )(x2, *consts)


S5_CONSTS = ("perm", "permt", "bmat", "cmat", "a1_re", "a1_im", "a8_re", "a8_im", "apb_re", "apb_im",
             "alp_re", "alp_im", "shl_re", "shl_im", "d_skip", "w_glu", "b_glu")


def _s5_project(u, s5, xs_ref):
    n_state = xs_ref.shape[1] // 2
    half = n_state // 2
    lw = S5_LANE_CHUNK
    ub = _dot(s5["perm"][...], u.astype(BF16)).astype(BF16)
    d_half = ub.shape[1] // 2
    for c in range(n_state // lw):
        h, cc = divmod(c, half // lw)
        for part in range(2):
            xs_ref[:, pl.ds(part * n_state + c * lw, lw)] = _dot(
                ub[:, h * d_half:(h + 1) * d_half],
                s5["bmat"][h, :, pl.ds(part * half + cc * lw, lw)])


def _s5_scan(s5, xs_ref, xb_ref, carry_ref):
    tt = xs_ref.shape[0]
    n_state = xs_ref.shape[1] // 2
    seg_len = tt // SUBLANES
    lw = S5_LANE_CHUNK
    blk = lambda j: pl.ds(j * SUBLANES, SUBLANES)
    for c in range(n_state // lw):
        re_sl, im_sl = pl.ds(c * lw, lw), pl.ds(n_state + c * lw, lw)
        ar, ai = s5["a1_re"][:, re_sl], s5["a1_im"][:, re_sl]
        xr = jnp.zeros((SUBLANES, lw), F32)
        xi = jnp.zeros((SUBLANES, lw), F32)
        for j in range(seg_len):
            br = xs_ref[blk(j), re_sl]
            bi = xs_ref[blk(j), im_sl]
            xr, xi = ar * xr - ai * xi + br, ar * xi + ai * xr + bi
            xs_ref[blk(j), re_sl] = xr
            xs_ref[blk(j), im_sl] = xi
        er, ei = xr, xi
        for k, sh in enumerate((1, 2, 4)):
            pr, pi = s5["shl_re"][k, :, re_sl], s5["shl_im"][k, :, re_sl]
            sr, si = pltpu.roll(er, sh, 0), pltpu.roll(ei, sh, 0)
            er, ei = er + (pr * sr - pi * si), ei + (pr * si + pi * sr)
        tr, ti = carry_ref[:, re_sl], carry_ref[:, im_sl]
        pr, pi = s5["alp_re"][:, re_sl], s5["alp_im"][:, re_sl]
        fr = er + (pr * tr - pi * ti)
        fi = ei + (pr * ti + pi * tr)
        row = lax.broadcasted_iota(jnp.int32, fr.shape, 0)
        dr = jnp.where(row == 0, tr, pltpu.roll(fr, 1, 0))
        di = jnp.where(row == 0, ti, pltpu.roll(fi, 1, 0))
        last = SUBLANES - 1
        carry_ref[:, re_sl] = jnp.broadcast_to(fr[last:last + 1], fr.shape)
        carry_ref[:, im_sl] = jnp.broadcast_to(fi[last:last + 1], fi.shape)
        a8r, a8i = s5["a8_re"][:, re_sl], s5["a8_im"][:, re_sl]
        both = lambda x: jnp.concatenate([x, x], axis=0).astype(BF16)
        for a in range(seg_len // SUBLANES):
            dr2, di2 = both(dr), both(di)
            for q in range(SUBLANES // 2):
                rows = pl.ds((a * SUBLANES + 2 * q) * SUBLANES, 2 * SUBLANES)
                pr, pi = s5["apb_re"][q, :, re_sl], s5["apb_im"][q, :, re_sl]
                xb_ref[rows, re_sl] = xs_ref[rows, re_sl].astype(BF16) + (pr * dr2 - pi * di2)
                xb_ref[rows, im_sl] = xs_ref[rows, im_sl].astype(BF16) + (pr * di2 + pi * dr2)
            dr, di = a8r * dr - a8i * di, a8r * di + a8i * dr


def _s5_readout(u, s5, xb_ref):
    n_state = xb_ref.shape[1] // 2
    half = n_state // 2
    lw = S5_LANE_CHUNK
    n_chunks = n_state // lw
    width = u.shape[1] // n_chunks
    blocks = []
    for c in range(n_chunks):
        h, cc = divmod(c, half // lw)
        cols = pl.ds(cc * width, width)
        blocks.append(sum(_dot(xb_ref[:, pl.ds(part * n_state + c * lw, lw)],
                               s5["cmat"][h, pl.ds(part * half + cc * lw, lw), cols])
                          for part in range(2)))
    cx_hi, cx_lo = _split2(jnp.concatenate(blocks, axis=1))
    permt = s5["permt"][...]
    y = _dot(permt, cx_hi) + _dot(permt, cx_lo) + s5["d_skip"][...] * u
    z = jax.nn.gelu(y)
    gate = jax.nn.sigmoid(_dot(z.astype(BF16), s5["w_glu"][...]) + s5["b_glu"][...])
    return z * gate


def _s5_constants(tt, lam_re, lam_im, log_dt, b_re, b_im, c_re, c_im, d_skip, w_glu, b_glu):
    d_ssm = d_skip.shape[0]
    n_grp, n_state, grp = b_re.shape
    seg_len = tt // SUBLANES
    *scan_consts, bb_re, bb_im = _s5_prep(lam_re, lam_im, log_dt, b_re, b_im, seg_len)
    to_blocks = lambda bb: jnp.swapaxes(bb.reshape(n_grp, n_state, grp), 1, 2)
    hg = n_grp // 2
    halves = lambda blocks: [_block_diag(blocks[h * hg:(h + 1) * hg]) for h in range(2)]
    b_re_h, b_im_h = halves(to_blocks(bb_re)), halves(to_blocks(bb_im))
    c_re_h, c_im_h = halves(jnp.swapaxes(c_re, 1, 2)), halves(jnp.swapaxes(-c_im, 1, 2))
    bmat = jnp.stack([jnp.concatenate([b_re_h[h], b_im_h[h]], axis=1)
                      for h in range(2)]).astype(BF16)
    cmat = jnp.stack([jnp.concatenate([c_re_h[h], c_im_h[h]], axis=0)
                      for h in range(2)]).astype(BF16)
    rho = jnp.arange(tt)
    src = (rho % SUBLANES) * seg_len + rho // SUBLANES
    perm = (src[:, None] == jnp.arange(tt)[None, :]).astype(BF16)
    return [perm, perm.T, bmat, cmat, *scan_consts, d_skip.reshape(1, d_ssm), w_glu.astype(BF16),
            b_glu.reshape(1, d_ssm)]


def _pair_blocks(x):
    lane = lax.broadcasted_iota(jnp.int32, x.shape, 1)
    first = lane < RWKV_HEAD
    return jnp.concatenate([jnp.where(first, x, 0.0), jnp.where(first, 0.0, x)], axis=0)


def _pair_transpose(x):
    t = _pair_blocks(x).T
    return t[:CHUNK] + t[CHUNK:]


def _rwkv_chunks_scores(items):
    shape = (CHUNK, LANES)
    row = lax.broadcasted_iota(jnp.int32, shape, 0)
    col = lax.broadcasted_iota(jnp.int32, shape, 1) % RWKV_HEAD
    strict = row > col
    incl = row >= col
    eye = row == col
    blocks = lambda x: _pair_blocks(x).astype(BF16)

    for it in items:
        lhs = jnp.concatenate([it["at"], it["rt"]], axis=0).astype(BF16)
        rhs = jnp.concatenate([blocks(it["bt"]), blocks(it["kt"])], axis=0)
        sc = _dot_nt(lhs, rhs)
        it["s_ab"] = jnp.where(strict, sc[:CHUNK, :LANES], 0.0)
        it["s_ak"] = jnp.where(strict, sc[:CHUNK, LANES:], 0.0).astype(BF16)
        it["s_r"] = jnp.concatenate([jnp.where(incl, sc[CHUNK:, :LANES], 0.0),
                                     jnp.where(incl, sc[CHUNK:, LANES:], 0.0)], axis=1).astype(BF16)
        it["bd_v"] = blocks(it["v"])
        it["bk_t"] = jnp.concatenate([_pair_transpose(it["bh"]), _pair_transpose(it["kh"])],
                                     axis=1).astype(BF16)

    return strict, incl, eye, blocks


def _rwkv_chunks_rest(items, h_ref, eye, blocks):
    shape = (CHUNK, LANES)
    for it in items:
        it["npow"] = _dot(it["s_ab"].astype(BF16), blocks(it["s_ab"]))
        it["inv"] = jnp.where(eye, 1.0, 0.0) + it["s_ab"]
    for step in range(4):
        for it in items:
            both = _dot(jnp.concatenate([it["inv"], it["npow"]], axis=0).astype(BF16),
                        blocks(it["npow"]))
            it["inv"] = it["inv"] + both[:CHUNK]
            it["npow"] = both[CHUNK:]
    for it in items:
        it["inv"] = it["inv"] + _dot(it["inv"].astype(BF16), blocks(it["npow"]))

    for it in items:
        it["av"] = _dot(it["s_ak"], it["bd_v"])
    for it in items:
        wu = _dot(it["inv"].astype(BF16),
                  jnp.concatenate([blocks(it["at"]), blocks(it["av"])], axis=1))
        it["z"] = jnp.concatenate(
            [jnp.concatenate([blocks(wu[:, :LANES]), blocks(wu[:, LANES:])], axis=1),
             jnp.concatenate([jnp.zeros((LANES, LANES), BF16), it["bd_v"]], axis=1)], axis=0)
    for it in items:
        both = _dot(jnp.concatenate([it["s_r"], it["bk_t"]], axis=0), it["z"])
        qy, mg = both[:CHUNK], both[CHUNK:]
        m = jnp.where(eye, jnp.broadcast_to(it["decay"], shape), 0.0) + mg[:, :LANES]
        it["qm"] = jnp.concatenate([it["rt"] + qy[:, :LANES], m], axis=0).astype(BF16)
        it["yp"] = qy[:, LANES:]
        it["g"] = mg[:, LANES:]

    for it in items:
        both = _dot(it["qm"], blocks(h_ref[it["pair"]]))
        it["y"] = both[:CHUNK] + it["yp"]
        h_ref[it["pair"]] = both[CHUNK:] + it["g"]


OPERANDS = ("at", "rt", "bt", "kt", "bh", "kh", "v")


def _rwkv_kernel(r_ref, k2_ref, v_ref, av_ref, bv_ref, logw_ref, gate_ref, bonus_ref, lng_ref,
                 lnb_ref, ones_ref, ltri_ref, o_ref, h_ref, decay_ref, *op_refs):
    tt, d = r_ref.shape
    n_pairs = d // LANES
    n_chunks = tt // CHUNK
    ops = dict(zip(OPERANDS, op_refs))

    @pl.when(pl.program_id(1) == 0)
    def _():
        for ref in (h_ref, decay_ref) + tuple(op_refs):
            ref[...] = jnp.zeros_like(ref)

    items = []
    for c in range(n_chunks):
        rows = pl.ds(c * CHUNK, CHUNK)
        for j in range(n_pairs):
            ln = pl.ds(j * LANES, LANES)
            item = {name: ref[rows, ln] for name, ref in ops.items()}
            item["decay"] = decay_ref[c, :, ln]
            item["pair"] = j
            items.append(item)
    _, _, eye, blocks = _rwkv_chunks_scores(items)

    for c in range(n_chunks):
        rows = pl.ds(c * CHUNK, CHUNK)
        load = lambda ref: ref[rows, :].astype(F32)
        r, k2, v, av, bv, logw = map(load, (r_ref, k2_ref, v_ref, av_ref, bv_ref, logw_ref))
        l_hi, l_lo = _split2(logw)
        lc = _dot(ltri_ref[...], l_hi) + _dot(ltri_ref[...], l_lo)
        lc_end = lc[CHUNK - 1:CHUNK]
        e_in = jnp.exp(lc - logw)
        e_inc = jnp.exp(lc)
        e_neg = jnp.exp(-lc)
        e_out = jnp.exp(lc_end - lc)
        new = dict(at=av * e_in, rt=r * e_inc, bt=bv * e_neg, kt=k2 * e_neg, bh=bv * e_out,
                   kh=k2 * e_out, v=v)
        for name, val in new.items():
            ops[name][rows, :] = val
        decay_ref[c] = jnp.exp(lc_end)

    _rwkv_chunks_rest(items, h_ref, eye, blocks)
    ones = ones_ref[...]
    inv_n = 1.0 / RWKV_HEAD
    for c in range(n_chunks):
        rows = pl.ds(c * CHUNK, CHUNK)
        y = jnp.concatenate([it["y"] for it in items[c * n_pairs:(c + 1) * n_pairs]], axis=1)
        mean = _head_sum(y, ones) * inv_n
        yc = y - mean
        var = _head_sum(yc * yc, ones) * inv_n
        yn = yc * lax.rsqrt(var + GN_EPS) * lng_ref[...] + lnb_ref[...]
        o_ref[rows, :] = (yn + bonus_ref[rows, :]) * gate_ref[rows, :]


def _rwkv_branch(mix, bsz, lnx_g, lnx_b):
    *step_in, gate, bonus = mix
    t, d = gate.shape
    seq = t // bsz
    tt = min(TT_RWKV, seq)
    nt = seq // tt
    head = jnp.arange(LANES) // RWKV_HEAD
    ones = (head[:, None] == head[None, :]).astype(BF16)
    tok_i = jnp.arange(CHUNK)
    ltri = (tok_i[:, None] >= tok_i[None, :]).astype(BF16)
    this_tile = lambda b, i: (b * nt + jnp.minimum(i, nt - 1), 0)
    prev_tile = lambda b, i: (b * nt + jnp.maximum(i - 1, 0), 0)
    c2 = lambda b, i: (0, 0)
    consts = [lnx_g.reshape(1, d), lnx_b.reshape(1, d), ones, ltri]
    return pl.pallas_call(
        _rwkv_kernel,
        grid=(bsz, nt + 1),
        in_specs=[pl.BlockSpec((tt, d), this_tile) for _ in step_in]
        + [pl.BlockSpec((tt, d), prev_tile)] * 2 + [pl.BlockSpec(c.shape, c2) for c in consts],
        out_specs=pl.BlockSpec((tt, d), prev_tile),
        out_shape=jax.ShapeDtypeStruct((t, d), F32),
        scratch_shapes=[pltpu.VMEM((d // LANES, RWKV_HEAD, LANES), F32),
                        pltpu.VMEM((tt // CHUNK, 1, d), F32)]
        + [pltpu.VMEM((tt, d), F32)] * len(OPERANDS),
        compiler_params=pltpu.CompilerParams(dimension_semantics=("arbitrary", "arbitrary"),
                                             vmem_limit_bytes=VMEM_LIMIT),
        name="rwkv",
    )(*step_in, gate, bonus, *consts)


def _tail_kernel(x_ref, oa_ref, ob_ref, g1_ref, win_ref, wb_ref, wout_ref, g2_ref,
                 w1_ref, w2_ref, gf_ref, o_ref, *, ff_block):
    x = x_ref[...]
    d = x.shape[1]
    d_mix = oa_ref.shape[1]
    h = _rmsnorm(x, g1_ref[...]).astype(BF16)
    gates = jax.nn.sigmoid(_dot(h, win_ref[:, win_ref.shape[1] - 2 * d:]))
    ma = _dot(oa_ref[...].astype(BF16), wb_ref[:d_mix, :])
    mb = _dot(ob_ref[...].astype(BF16), wb_ref[d_mix:, :])
    merged = gates[:, :d] * ma + gates[:, d:] * mb
    x1 = x + _dot(merged.astype(BF16), wout_ref[...])
    h2 = _rmsnorm(x1, g2_ref[...]).astype(BF16)
    acc = x1
    for j in range(w1_ref.shape[1] // ff_block):
        cols = slice(j * ff_block, (j + 1) * ff_block)
        hid = jnp.square(jnp.maximum(_dot(h2, w1_ref[:, cols]), 0.0))
        acc = acc + _dot(hid.astype(BF16), w2_ref[cols, :])
    o_ref[...] = _rmsnorm(acc, gf_ref[...])


def _tail(x2, o_a, o_b, norm1_g, w_in, w_branch, w_out, norm2_g, w_ff1, w_ff2, norm_f_g):
    t, d = x2.shape
    d_mix = o_a.shape[1]
    tm = min(TM_PROJ, t)
    tok = lambda i: (i, 0)
    const = lambda i: (0, 0)
    resident = lambda a: pl.BlockSpec(a.shape, const, pipeline_mode=pl.Buffered(1))
    vec = lambda a: a.reshape(1, d)
    return pl.pallas_call(
        functools.partial(_tail_kernel, ff_block=d),
        grid=(t // tm,),
        in_specs=[pl.BlockSpec((tm, d), tok), pl.BlockSpec((tm, d_mix), tok),
                  pl.BlockSpec((tm, d_mix), tok), pl.BlockSpec((1, d), const),
                  resident(w_in), resident(w_branch), resident(w_out),
                  pl.BlockSpec((1, d), const), resident(w_ff1), resident(w_ff2),
                  pl.BlockSpec((1, d), const)],
        out_specs=pl.BlockSpec((tm, d), tok),
        out_shape=jax.ShapeDtypeStruct((t, d), F32),
        compiler_params=pltpu.CompilerParams(dimension_semantics=("arbitrary",),
                                             vmem_limit_bytes=VMEM_LIMIT),
        name="tail",
    )(x2, o_a, o_b, vec(norm1_g), w_in, w_branch, w_out, vec(norm2_g), w_ff1, w_ff2,
      vec(norm_f_g))


def _layer(x2, bsz, norm1_g, w_in, lam_re, lam_im, log_dt, b_re, b_im, c_re, c_im, d_skip, w_glu,
           b_glu, mu_rwkv, w0, w2, a0, a2, g2, k_k, k_a, r_k, lnx_g, lnx_b, w_branch, w_out,
           norm2_g, w_ff1, w_ff2, out_g):
    d_ssm = d_skip.shape[0]
    w_in = w_in.astype(BF16)
    o_a, *mix = _mix_in(x2, x2.shape[0] // bsz, norm1_g, w_in, d_ssm, mu_rwkv, w0, w2, a0, a2, g2,
                        k_k, k_a, r_k.reshape(-1),
                        (lam_re, lam_im, log_dt, b_re, b_im, c_re, c_im, d_skip, w_glu, b_glu))
    o_b = _rwkv_branch(mix, bsz, lnx_g, lnx_b)
    return _tail(x2, o_a, o_b, norm1_g, w_in, w_branch.astype(BF16), w_out.astype(BF16), norm2_g,
                 w_ff1.astype(BF16), w_ff2.astype(BF16), out_g)


def kernel(x, norm1_g, w_in, lam_re, lam_im, log_dt, b_re, b_im, c_re, c_im, d_skip, w_glu, b_glu,
           mu_rwkv, w0, w2, a0, a2, g2, k_k, k_a, r_k, lnx_g, lnx_b, w_branch, w_out, norm2_g,
           w_ff1, w_ff2, norm_f_g):
    bsz, seq, d = x.shape
    depth = norm1_g.shape[0]
    assert depth == 1, "the fused tail applies the final norm, so exactly one layer is supported"
    x2 = x.reshape(bsz * seq, d)
    l = 0
    out = _layer(x2, bsz, norm1_g[l], w_in[l], lam_re[l], lam_im[l], log_dt[l], b_re[l], b_im[l],
                 c_re[l], c_im[l], d_skip[l], w_glu[l], b_glu[l], mu_rwkv[l], w0[l], w2[l], a0[l],
                 a2[l], g2[l], k_k[l], k_a[l], r_k[l], lnx_g[l], lnx_b[l], w_branch[l], w_out[l],
                 norm2_g[l], w_ff1[l], w_ff2[l], norm_f_g)
    return out.reshape(bsz, seq, d)
```

```python
import functools

import jax
import jax.numpy as jnp
from jax import lax
from jax.experimental import pallas as pl
from jax.experimental.pallas import tpu as pltpu

F32 = jnp.float32
BF16 = jnp.bfloat16

RMS_EPS = 1e-6
GN_EPS = 64e-5
L2_EPS = 1e-12
MIN_NEG_REAL = -1e-4
EXP_NEG_HALF = 0.6065306597126334

SSM_GROUP = 16
SSM_STATE = 64
RWKV_HEAD = 64
GATE_LORA = 128
DECAY_LORA = 64
AAA_LORA = 64
CHUNK = 64

SUBLANES = 8
LANES = 128
VMEM_LIMIT = 56 * 1024 * 1024

TM_PROJ = 512
TT_S5 = 256
TT_RWKV = 256
S5_LANE_CHUNK = 512


def _dot(a, b):
    return jnp.dot(a, b, preferred_element_type=F32)


def _dot_nt(a, b):
    return lax.dot_general(a, b, (((1,), (1,)), ((), ())), preferred_element_type=F32)


def _split2(x):
    hi = x.astype(BF16)
    lo = (x - hi.astype(F32)).astype(BF16)
    return hi, lo


def _split3(x):
    hi = x.astype(BF16)
    r1 = x - hi.astype(F32)
    lo = r1.astype(BF16)
    lo2 = (r1 - lo.astype(F32)).astype(BF16)
    return hi, lo, lo2


def _rmsnorm(x, g):
    ms = jnp.mean(x * x, axis=-1, keepdims=True)
    return x * lax.rsqrt(ms + RMS_EPS) * g


def _cmul(ar, ai, br, bi):
    return ar * br - ai * bi, ar * bi + ai * br


def _s5_prep_kernel(lre_ref, lim_ref, ldt_ref, lre_w_ref, lim_w_ref, ldt_w_ref, bre_ref, bim_ref,
                    a1_re_ref, a1_im_ref, a8_re_ref, a8_im_ref, apb_re_ref, apb_im_ref,
                    alp_re_ref, alp_im_ref, shl_re_ref, shl_im_ref, bbre_ref, bbim_ref, *, seg_len):
    def discretise(lre, lim, ldt):
        lr = jnp.minimum(lre, MIN_NEG_REAL)
        dt = jnp.exp(ldt)
        mag = jnp.exp(lr * dt)
        ab_re = mag * jnp.cos(lim * dt)
        ab_im = mag * jnp.sin(lim * dt)
        den = lr * lr + lim * lim
        xm1 = ab_re - 1.0
        q_re = (xm1 * lr + ab_im * lim) / den
        q_im = (ab_im * lr - xm1 * lim) / den
        return ab_re, ab_im, q_re, q_im

    shape = a1_re_ref.shape
    a_re, a_im, _, _ = discretise(lre_ref[...], lim_ref[...], ldt_ref[...])
    bcast = lambda x: jnp.broadcast_to(x, shape)
    pows = [(a_re, a_im)]
    for _ in range(SUBLANES - 1):
        pows.append(_cmul(*pows[-1], a_re, a_im))
    a1_re_ref[...] = bcast(a_re)
    a1_im_ref[...] = bcast(a_im)
    a8_re_ref[...] = bcast(pows[-1][0])
    a8_im_ref[...] = bcast(pows[-1][1])
    shape16 = (2 * SUBLANES,) + shape[1:]
    first = lax.broadcasted_iota(jnp.int32, shape16, 0) < SUBLANES
    for q in range(SUBLANES // 2):
        (r0, i0), (r1, i1) = pows[2 * q], pows[2 * q + 1]
        bc16 = lambda x: jnp.broadcast_to(x, shape16)
        apb_re_ref[q] = jnp.where(first, bc16(r0), bc16(r1)).astype(BF16)
        apb_im_ref[q] = jnp.where(first, bc16(i0), bc16(i1)).astype(BF16)
    lr_, li_ = pows[-1]
    n = SUBLANES
    while n < seg_len:
        lr_, li_ = _cmul(lr_, li_, lr_, li_)
        n *= 2
    lpows = [(lr_, li_)]
    for _ in range(SUBLANES - 1):
        lpows.append(_cmul(*lpows[-1], lr_, li_))
    row = lax.broadcasted_iota(jnp.int32, shape, 0)
    alp_re = jnp.zeros(shape, F32)
    alp_im = jnp.zeros(shape, F32)
    for i, (pr, pi) in enumerate(lpows):
        alp_re = jnp.where(row == i, bcast(pr), alp_re)
        alp_im = jnp.where(row == i, bcast(pi), alp_im)
    alp_re_ref[...] = alp_re
    alp_im_ref[...] = alp_im
    for i, s in enumerate((1, 2, 4)):
        pr, pi = lpows[s - 1]
        shl_re_ref[i] = jnp.where(row >= s, bcast(pr), 0.0)
        shl_im_ref[i] = jnp.where(row >= s, bcast(pi), 0.0)

    _, _, q_re, q_im = discretise(lre_w_ref[...], lim_w_ref[...], ldt_w_ref[...])
    br = bre_ref[...]
    bi = bim_ref[...]
    bbre_ref[...] = q_re * br - q_im * bi
    bbim_ref[...] = q_re * bi + q_im * br


def _s5_prep(lam_re, lam_im, log_dt, b_re, b_im, seg_len):
    g, p = lam_re.shape
    m = b_re.shape[-1]
    n = g * p
    row = lambda a: a.reshape(1, n)
    wide = lambda a: jnp.repeat(a, m, axis=-1)
    ldt_gp = jnp.broadcast_to(log_dt[:, None], (g, p))
    s8 = jax.ShapeDtypeStruct((SUBLANES, n), F32)
    s88 = jax.ShapeDtypeStruct((SUBLANES // 2, 2 * SUBLANES, n), BF16)
    s38 = jax.ShapeDtypeStruct((3, SUBLANES, n), F32)
    sw = jax.ShapeDtypeStruct((g, p * m), F32)
    out_shape = (s8, s8, s8, s8, s88, s88, s8, s8, s38, s38, sw, sw)
    return pl.pallas_call(functools.partial(_s5_prep_kernel, seg_len=seg_len),
                          out_shape=out_shape, name="s5_prep")(
        row(lam_re), row(lam_im), row(ldt_gp), wide(lam_re), wide(lam_im), wide(ldt_gp),
        b_re.reshape(g, p * m), b_im.reshape(g, p * m))


def _block_diag(blocks):
    g, r, c = blocks.shape
    eye = jnp.eye(g, dtype=bool)
    return jnp.where(eye[:, None, :, None], blocks[:, :, None, :], 0.0).reshape(g * r, g * c)


def _head_sum(x, ones):
    rows, width = x.shape
    groups = width // LANES
    stacked = jnp.concatenate([x[:, j * LANES:(j + 1) * LANES] for j in range(groups)], axis=0)
    sums = _dot(stacked.astype(BF16), ones)
    return jnp.concatenate([sums[j * rows:(j + 1) * rows] for j in range(groups)], axis=1)


def _mix_in_kernel(*refs, d_ssm, d, tiles_per_seq, tt_s5):
    (x_ref, g_ref, w_ref, mu_ref, wwa_ref, g2_ref, w0_ref, a0_ref, kk_ref, ka_ref, rk_ref,
     ones_ref) = refs[:12]
    s5 = dict(zip(S5_CONSTS, refs[12:12 + len(S5_CONSTS)]))
    (oa_ref, r_ref, k2_ref, v_ref, av_ref, bv_ref, logw_ref, gate_ref, bonus_ref, h_ref,
     prev_ref, carry_ref, *state_refs) = refs[12 + len(S5_CONSTS):]
    xs_refs, xb_refs = state_refs[:len(state_refs) // 2], state_refs[len(state_refs) // 2:]
    tm = x_ref.shape[0]

    @pl.when(pl.program_id(0) % tiles_per_seq == 0)
    def _():
        prev_ref[...] = jnp.zeros_like(prev_ref)
        carry_ref[...] = jnp.zeros_like(carry_ref)

    h = _rmsnorm(x_ref[...], g_ref[...]).astype(BF16)
    h_ref[...] = h

    def mixed(c0, c1):
        p = _dot(h, w_ref[:, d_ssm + c0:d_ssm + c1])
        shape8 = (SUBLANES, c1 - c0)
        row = lax.broadcasted_iota(jnp.int32, shape8, 0)
        last = prev_ref[SUBLANES - 1:SUBLANES, c0:c1]
        prev = pltpu.roll(p, 1, 0)
        head = jnp.where(row == 0, jnp.broadcast_to(last, shape8), prev[:SUBLANES])
        prev = jnp.concatenate([head, prev[SUBLANES:]], axis=0)
        prev_ref[:, c0:c1] = p[tm - SUBLANES:, :]
        return p + (prev - p) * mu_ref[:, c0:c1]

    sub = lambda i: slice(i * tt_s5, (i + 1) * tt_s5)
    n_sub = tm // tt_s5
    us = [_dot(h[sub(i)], w_ref[:, :d_ssm]) for i in range(n_sub)]
    for i in range(n_sub):
        _s5_project(us[i], s5, xs_refs[i])
    _s5_scan(s5, xs_refs[0], xb_refs[0], carry_ref)
    lora_in = mixed(3 * d, 3 * d + GATE_LORA + DECAY_LORA + AAA_LORA)
    k = mixed(d, 2 * d)
    xg = lora_in[:, :GATE_LORA]
    xwa = lora_in[:, GATE_LORA:]
    lane = lax.broadcasted_iota(jnp.int32, xwa.shape, 1)
    zwa = jnp.where(lane < DECAY_LORA, jnp.tanh(xwa), xwa)
    lora = _dot(zwa.astype(BF16), wwa_ref[...])
    gate_ref[...] = _dot(jax.nn.sigmoid(xg).astype(BF16), g2_ref[...])
    logw_ref[...] = -EXP_NEG_HALF * jax.nn.sigmoid(w0_ref[...] + lora[:, :d])
    a = jax.nn.sigmoid(a0_ref[...] + lora[:, d:])
    for i in range(1, n_sub):
        _s5_scan(s5, xs_refs[i], xb_refs[i], carry_ref)
    oa_ref[sub(0), :] = _s5_readout(us[0], s5, xb_refs[0])

    ones = ones_ref[...]
    r = mixed(0, d)
    kk = k * kk_ref[...]
    kk = kk * lax.rsqrt(jnp.maximum(_head_sum(kk * kk, ones), L2_EPS * L2_EPS))
    v = mixed(2 * d, 3 * d)
    k2 = k * (1.0 + (a - 1.0) * ka_ref[...])
    k2_ref[...] = k2.astype(BF16)
    av_ref[...] = (-kk).astype(BF16)
    bv_ref[...] = (kk * a).astype(BF16)
    r_ref[...] = r.astype(BF16)
    v_ref[...] = v.astype(BF16)
    bonus_ref[...] = _head_sum(r * k2 * rk_ref[...], ones) * v
    for i in range(1, n_sub):
        oa_ref[sub(i), :] = _s5_readout(us[i], s5, xb_refs[i])


def _mix_in(x2, seq, norm_g, w_in, d_ssm, mu, w0, w2, a0, a2, g2, k_k, k_a, r_k, s5_params):
    t, dm = x2.shape
    d = w0.shape[0]
    tm = min(TM_PROJ, seq)
    tt_s5 = min(TT_S5, tm)
    s5_consts = _s5_constants(tt_s5, *s5_params)
    n_state2 = s5_consts[2].shape[2]
    wwa = jnp.concatenate([jnp.concatenate([w2, jnp.zeros_like(w2)], axis=1),
                           jnp.concatenate([jnp.zeros_like(a2), a2], axis=1)], axis=0)
    head = jnp.arange(LANES) // RWKV_HEAD
    ones = (head[:, None] == head[None, :]).astype(BF16)
    vec = lambda a: a.reshape(1, -1)
    consts = [vec(norm_g), w_in, vec(mu), wwa.astype(BF16), g2.astype(BF16), vec(w0), vec(a0),
              vec(k_k), vec(k_a), vec(r_k), ones, *s5_consts]
    whole = lambda a: pl.BlockSpec(a.shape, lambda i: (0,) * a.ndim)
    tok = lambda i: (i, 0)
    widths = [(d_ssm, F32)] + [(d, BF16)] * 5 + [(d, F32)] * 3 + [(dm, BF16)]
    return pl.pallas_call(
        functools.partial(_mix_in_kernel, d_ssm=d_ssm, d=d, tiles_per_seq=seq // tm, tt_s5=tt_s5),
        grid=(t // tm,),
        in_specs=[pl.BlockSpec((tm, dm), tok)] + [whole(c) for c in consts],
        out_specs=[pl.BlockSpec((tm, w), tok) for w, _ in widths],
        out_shape=[jax.ShapeDtypeStruct((t, w), dt) for w, dt in widths],
        scratch_shapes=[pltpu.VMEM((SUBLANES, mu.shape[0]), F32),
                        pltpu.VMEM((SUBLANES, 2 * n_state2), F32)]
        + [pltpu.VMEM((tt_s5, 2 * n_state2), F32)] * (tm // tt_s5)
        + [pltpu.VMEM((tt_s5, 2 * n_state2), BF16)] * (tm // tt_s5),
        compiler_params=pltpu.CompilerParams(dimension_semantics=("arbitrary",),
                                             vmem_limit_bytes=VMEM_LIMIT),
        name="mix_in",
    )(x2, *consts)


S5_CONSTS = ("perm", "permt", "bmat", "cmat", "a1_re", "a1_im", "a8_re", "a8_im", "apb_re", "apb_im",
             "alp_re", "alp_im", "shl_re", "shl_im", "d_skip", "w_glu", "b_glu")


def _s5_project(u, s5, xs_ref):
    n_state = xs_ref.shape[1] // 2
    half = n_state // 2
    lw = S5_LANE_CHUNK
    ub = _dot(s5["perm"][...], u.astype(BF16)).astype(BF16)
    d_half = ub.shape[1] // 2
    for c in range(n_state // lw):
        h, cc = divmod(c, half // lw)
        for part in range(2):
            xs_ref[:, pl.ds(part * n_state + c * lw, lw)] = _dot(
                ub[:, h * d_half:(h + 1) * d_half],
                s5["bmat"][h, :, pl.ds(part * half + cc * lw, lw)])


def _s5_scan(s5, xs_ref, xb_ref, carry_ref):
    tt = xs_ref.shape[0]
    n_state = xs_ref.shape[1] // 2
    seg_len = tt // SUBLANES
    lw = S5_LANE_CHUNK
    blk = lambda j: pl.ds(j * SUBLANES, SUBLANES)
    for c in range(n_state // lw):
        re_sl, im_sl = pl.ds(c * lw, lw), pl.ds(n_state + c * lw, lw)
        ar, ai = s5["a1_re"][:, re_sl], s5["a1_im"][:, re_sl]
        xr = jnp.zeros((SUBLANES, lw), F32)
        xi = jnp.zeros((SUBLANES, lw), F32)
        for j in range(seg_len):
            br = xs_ref[blk(j), re_sl]
            bi = xs_ref[blk(j), im_sl]
            xr, xi = ar * xr - ai * xi + br, ar * xi + ai * xr + bi
            xs_ref[blk(j), re_sl] = xr
            xs_ref[blk(j), im_sl] = xi
        er, ei = xr, xi
        for k, sh in enumerate((1, 2, 4)):
            pr, pi = s5["shl_re"][k, :, re_sl], s5["shl_im"][k, :, re_sl]
            sr, si = pltpu.roll(er, sh, 0), pltpu.roll(ei, sh, 0)
            er, ei = er + (pr * sr - pi * si), ei + (pr * si + pi * sr)
        tr, ti = carry_ref[:, re_sl], carry_ref[:, im_sl]
        pr, pi = s5["alp_re"][:, re_sl], s5["alp_im"][:, re_sl]
        fr = er + (pr * tr - pi * ti)
        fi = ei + (pr * ti + pi * tr)
        row = lax.broadcasted_iota(jnp.int32, fr.shape, 0)
        dr = jnp.where(row == 0, tr, pltpu.roll(fr, 1, 0))
        di = jnp.where(row == 0, ti, pltpu.roll(fi, 1, 0))
        last = SUBLANES - 1
        carry_ref[:, re_sl] = jnp.broadcast_to(fr[last:last + 1], fr.shape)
        carry_ref[:, im_sl] = jnp.broadcast_to(fi[last:last + 1], fi.shape)
        a8r, a8i = s5["a8_re"][:, re_sl], s5["a8_im"][:, re_sl]
        both = lambda x: jnp.concatenate([x, x], axis=0).astype(BF16)
        for a in range(seg_len // SUBLANES):
            dr2, di2 = both(dr), both(di)
            for q in range(SUBLANES // 2):
                rows = pl.ds((a * SUBLANES + 2 * q) * SUBLANES, 2 * SUBLANES)
                pr, pi = s5["apb_re"][q, :, re_sl], s5["apb_im"][q, :, re_sl]
                xb_ref[rows, re_sl] = xs_ref[rows, re_sl].astype(BF16) + (pr * dr2 - pi * di2)
                xb_ref[rows, im_sl] = xs_ref[rows, im_sl].astype(BF16) + (pr * di2 + pi * dr2)
            dr, di = a8r * dr - a8i * di, a8r * di + a8i * dr


def _s5_readout(u, s5, xb_ref):
    n_state = xb_ref.shape[1] // 2
    half = n_state // 2
    lw = S5_LANE_CHUNK
    n_chunks = n_state // lw
    width = u.shape[1] // n_chunks
    blocks = []
    for c in range(n_chunks):
        h, cc = divmod(c, half // lw)
        cols = pl.ds(cc * width, width)
        blocks.append(sum(_dot(xb_ref[:, pl.ds(part * n_state + c * lw, lw)],
                               s5["cmat"][h, pl.ds(part * half + cc * lw, lw), cols])
                          for part in range(2)))
    cx_hi, cx_lo = _split2(jnp.concatenate(blocks, axis=1))
    permt = s5["permt"][...]
    y = _dot(permt, cx_hi) + _dot(permt, cx_lo) + s5["d_skip"][...] * u
    z = jax.nn.gelu(y)
    gate = jax.nn.sigmoid(_dot(z.astype(BF16), s5["w_glu"][...]) + s5["b_glu"][...])
    return z * gate


def _s5_constants(tt, lam_re, lam_im, log_dt, b_re, b_im, c_re, c_im, d_skip, w_glu, b_glu):
    d_ssm = d_skip.shape[0]
    n_grp, n_state, grp = b_re.shape
    seg_len = tt // SUBLANES
    *scan_consts, bb_re, bb_im = _s5_prep(lam_re, lam_im, log_dt, b_re, b_im, seg_len)
    to_blocks = lambda bb: jnp.swapaxes(bb.reshape(n_grp, n_state, grp), 1, 2)
    hg = n_grp // 2
    halves = lambda blocks: [_block_diag(blocks[h * hg:(h + 1) * hg]) for h in range(2)]
    b_re_h, b_im_h = halves(to_blocks(bb_re)), halves(to_blocks(bb_im))
    c_re_h, c_im_h = halves(jnp.swapaxes(c_re, 1, 2)), halves(jnp.swapaxes(-c_im, 1, 2))
    bmat = jnp.stack([jnp.concatenate([b_re_h[h], b_im_h[h]], axis=1)
                      for h in range(2)]).astype(BF16)
    cmat = jnp.stack([jnp.concatenate([c_re_h[h], c_im_h[h]], axis=0)
                      for h in range(2)]).astype(BF16)
    rho = jnp.arange(tt)
    src = (rho % SUBLANES) * seg_len + rho // SUBLANES
    perm = (src[:, None] == jnp.arange(tt)[None, :]).astype(BF16)
    return [perm, perm.T, bmat, cmat, *scan_consts, d_skip.reshape(1, d_ssm), w_glu.astype(BF16),
            b_glu.reshape(1, d_ssm)]


def _pair_blocks(x):
    lane = lax.broadcasted_iota(jnp.int32, x.shape, 1)
    first = lane < RWKV_HEAD
    return jnp.concatenate([jnp.where(first, x, 0.0), jnp.where(first, 0.0, x)], axis=0)


def _pair_transpose(x):
    t = _pair_blocks(x).T
    return t[:CHUNK] + t[CHUNK:]


def _rwkv_chunks_scores(items):
    shape = (CHUNK, LANES)
    row = lax.broadcasted_iota(jnp.int32, shape, 0)
    col = lax.broadcasted_iota(jnp.int32, shape, 1) % RWKV_HEAD
    strict = row > col
    incl = row >= col
    eye = row == col
    blocks = lambda x: _pair_blocks(x).astype(BF16)

    for it in items:
        lhs = jnp.concatenate([it["at"], it["rt"]], axis=0).astype(BF16)
        rhs = jnp.concatenate([blocks(it["bt"]), blocks(it["kt"])], axis=0)
        sc = _dot_nt(lhs, rhs)
        it["s_ab"] = jnp.where(strict, sc[:CHUNK, :LANES], 0.0)
        it["s_ak"] = jnp.where(strict, sc[:CHUNK, LANES:], 0.0).astype(BF16)
        it["s_r"] = jnp.concatenate([jnp.where(incl, sc[CHUNK:, :LANES], 0.0),
                                     jnp.where(incl, sc[CHUNK:, LANES:], 0.0)], axis=1).astype(BF16)
        it["bd_v"] = blocks(it["v"])
        it["bk_t"] = jnp.concatenate([_pair_transpose(it["bh"]), _pair_transpose(it["kh"])],
                                     axis=1).astype(BF16)

    return strict, incl, eye, blocks


def _rwkv_chunks_rest(items, h_ref, eye, blocks, filler):
    shape = (CHUNK, LANES)
    for it in items:
        it["npow"] = _dot(it["s_ab"].astype(BF16), blocks(it["s_ab"]))
        it["inv"] = jnp.where(eye, 1.0, 0.0) + it["s_ab"]
    filler()
    for step in range(4):
        for it in items:
            both = _dot(jnp.concatenate([it["inv"], it["npow"]], axis=0).astype(BF16),
                        blocks(it["npow"]))
            it["inv"] = it["inv"] + both[:CHUNK]
            it["npow"] = both[CHUNK:]
        filler()
    for it in items:
        it["inv"] = it["inv"] + _dot(it["inv"].astype(BF16), blocks(it["npow"]))
    filler()

    for it in items:
        it["av"] = _dot(it["s_ak"], it["bd_v"])
    for it in items:
        wu = _dot(it["inv"].astype(BF16),
                  jnp.concatenate([blocks(it["at"]), blocks(it["av"])], axis=1))
        it["z"] = jnp.concatenate(
            [jnp.concatenate([blocks(wu[:, :LANES]), blocks(wu[:, LANES:])], axis=1),
             jnp.concatenate([jnp.zeros((LANES, LANES), BF16), it["bd_v"]], axis=1)], axis=0)
    filler()
    for it in items:
        both = _dot(jnp.concatenate([it["s_r"], it["bk_t"]], axis=0), it["z"])
        qy, mg = both[:CHUNK], both[CHUNK:]
        m = jnp.where(eye, jnp.broadcast_to(it["decay"], shape), 0.0) + mg[:, :LANES]
        it["qm"] = jnp.concatenate([it["rt"] + qy[:, :LANES], m], axis=0).astype(BF16)
        it["yp"] = qy[:, LANES:]
        it["g"] = mg[:, LANES:]
    filler()

    for it in items:
        both = _dot(it["qm"], blocks(h_ref[it["pair"]]))
        it["y"] = both[:CHUNK] + it["yp"]
        h_ref[it["pair"]] = both[CHUNK:] + it["g"]


OPERANDS = ("at", "rt", "bt", "kt", "bh", "kh", "v")


def _rwkv_kernel(r_ref, k2_ref, v_ref, av_ref, bv_ref, logw_ref, gate_ref, bonus_ref, hn_ref,
                 wg_ref, lng_ref, lnb_ref, ones_ref, ltri_ref, o_ref, zg_ref, h_ref, decay_ref,
                 *op_refs):
    tt, d = r_ref.shape
    n_pairs = d // LANES
    n_chunks = tt // CHUNK
    ops = dict(zip(OPERANDS, op_refs))

    @pl.when(pl.program_id(1) == 0)
    def _():
        for ref in (h_ref, decay_ref) + tuple(op_refs):
            ref[...] = jnp.zeros_like(ref)

    items = []
    for c in range(n_chunks):
        rows = pl.ds(c * CHUNK, CHUNK)
        for j in range(n_pairs):
            ln = pl.ds(j * LANES, LANES)
            item = {name: ref[rows, ln] for name, ref in ops.items()}
            item["decay"] = decay_ref[c, :, ln]
            item["pair"] = j
            items.append(item)
    _, _, eye, blocks = _rwkv_chunks_scores(items)

    for c in range(n_chunks):
        rows = pl.ds(c * CHUNK, CHUNK)
        load = lambda ref: ref[rows, :].astype(F32)
        r, k2, v, av, bv, logw = map(load, (r_ref, k2_ref, v_ref, av_ref, bv_ref, logw_ref))
        l_hi, l_lo = _split2(logw)
        lc = _dot(ltri_ref[...], l_hi) + _dot(ltri_ref[...], l_lo)
        lc_end = lc[CHUNK - 1:CHUNK]
        e_in = jnp.exp(lc - logw)
        e_inc = jnp.exp(lc)
        e_neg = jnp.exp(-lc)
        e_out = jnp.exp(lc_end - lc)
        new = dict(at=av * e_in, rt=r * e_inc, bt=bv * e_neg, kt=k2 * e_neg, bh=bv * e_out,
                   kh=k2 * e_out, v=v)
        for name, val in new.items():
            ops[name][rows, :] = val
        decay_ref[c] = jnp.exp(lc_end)

    n_fill = 8
    piece = zg_ref.shape[1] // n_fill
    pending = list(range(n_fill))

    def filler():
        if pending:
            cols = pl.ds(pending.pop(0) * piece, piece)
            zg_ref[:, cols] = _dot(hn_ref[...], wg_ref[:, cols]).astype(BF16)

    _rwkv_chunks_rest(items, h_ref, eye, blocks, filler)
    while pending:
        filler()
    ones = ones_ref[...]
    inv_n = 1.0 / RWKV_HEAD
    for c in range(n_chunks):
        rows = pl.ds(c * CHUNK, CHUNK)
        y = jnp.concatenate([it["y"] for it in items[c * n_pairs:(c + 1) * n_pairs]], axis=1)
        mean = _head_sum(y, ones) * inv_n
        yc = y - mean
        var = _head_sum(yc * yc, ones) * inv_n
        yn = yc * lax.rsqrt(var + GN_EPS) * lng_ref[...] + lnb_ref[...]
        o_ref[rows, :] = (yn + bonus_ref[rows, :]) * gate_ref[rows, :]


def _rwkv_branch(mix, w_gate, bsz, lnx_g, lnx_b):
    *step_in, gate, bonus, hn = mix
    t, d = gate.shape
    seq = t // bsz
    tt = min(TT_RWKV, seq)
    nt = seq // tt
    head = jnp.arange(LANES) // RWKV_HEAD
    ones = (head[:, None] == head[None, :]).astype(BF16)
    tok_i = jnp.arange(CHUNK)
    ltri = (tok_i[:, None] >= tok_i[None, :]).astype(BF16)
    this_tile = lambda b, i: (b * nt + jnp.minimum(i, nt - 1), 0)
    prev_tile = lambda b, i: (b * nt + jnp.maximum(i - 1, 0), 0)
    c2 = lambda b, i: (0, 0)
    consts = [w_gate, lnx_g.reshape(1, d), lnx_b.reshape(1, d), ones, ltri]
    n_gate = w_gate.shape[1]
    return pl.pallas_call(
        _rwkv_kernel,
        grid=(bsz, nt + 1),
        in_specs=[pl.BlockSpec((tt, d), this_tile) for _ in step_in]
        + [pl.BlockSpec((tt, d), prev_tile)] * 2 + [pl.BlockSpec((tt, hn.shape[1]), prev_tile)]
        + [pl.BlockSpec(c.shape, c2) for c in consts],
        out_specs=[pl.BlockSpec((tt, d), prev_tile), pl.BlockSpec((tt, n_gate), prev_tile)],
        out_shape=[jax.ShapeDtypeStruct((t, d), F32), jax.ShapeDtypeStruct((t, n_gate), BF16)],
        scratch_shapes=[pltpu.VMEM((d // LANES, RWKV_HEAD, LANES), F32),
                        pltpu.VMEM((tt // CHUNK, 1, d), F32)]
        + [pltpu.VMEM((tt, d), F32)] * len(OPERANDS),
        compiler_params=pltpu.CompilerParams(dimension_semantics=("arbitrary", "arbitrary"),
                                             vmem_limit_bytes=VMEM_LIMIT),
        name="rwkv",
    )(*step_in, gate, bonus, hn, *consts)


def _tail_kernel(x_ref, oa_ref, ob_ref, zg_ref, wb_ref, wout_ref, g2_ref,
                 w1_ref, w2_ref, gf_ref, o_ref, *, ff_block):
    x = x_ref[...]
    d = x.shape[1]
    d_mix = oa_ref.shape[1]
    gates = jax.nn.sigmoid(zg_ref[...].astype(F32))
    ma = _dot(oa_ref[...].astype(BF16), wb_ref[:d_mix, :])
    mb = _dot(ob_ref[...].astype(BF16), wb_ref[d_mix:, :])
    merged = gates[:, :d] * ma + gates[:, d:] * mb
    x1 = x + _dot(merged.astype(BF16), wout_ref[...])
    h2 = _rmsnorm(x1, g2_ref[...]).astype(BF16)
    acc = x1
    for j in range(w1_ref.shape[1] // ff_block):
        cols = slice(j * ff_block, (j + 1) * ff_block)
        hid = jnp.square(jnp.maximum(_dot(h2, w1_ref[:, cols]), 0.0))
        acc = acc + _dot(hid.astype(BF16), w2_ref[cols, :])
    o_ref[...] = _rmsnorm(acc, gf_ref[...])


def _tail(x2, o_a, o_b, zg, w_branch, w_out, norm2_g, w_ff1, w_ff2, norm_f_g):
    t, d = x2.shape
    d_mix = o_a.shape[1]
    tm = min(TM_PROJ, t)
    tok = lambda i: (i, 0)
    const = lambda i: (0, 0)
    resident = lambda a: pl.BlockSpec(a.shape, const, pipeline_mode=pl.Buffered(1))
    vec = lambda a: a.reshape(1, d)
    return pl.pallas_call(
        functools.partial(_tail_kernel, ff_block=d),
        grid=(t // tm,),
        in_specs=[pl.BlockSpec((tm, d), tok), pl.BlockSpec((tm, d_mix), tok),
                  pl.BlockSpec((tm, d_mix), tok), pl.BlockSpec((tm, zg.shape[1]), tok),
                  resident(w_branch), resident(w_out),
                  pl.BlockSpec((1, d), const), resident(w_ff1), resident(w_ff2),
                  pl.BlockSpec((1, d), const)],
        out_specs=pl.BlockSpec((tm, d), tok),
        out_shape=jax.ShapeDtypeStruct((t, d), F32),
        compiler_params=pltpu.CompilerParams(dimension_semantics=("arbitrary",),
                                             vmem_limit_bytes=VMEM_LIMIT),
        name="tail",
    )(x2, o_a, o_b, zg, w_branch, w_out, vec(norm2_g), w_ff1, w_ff2, vec(norm_f_g))


def _layer(x2, bsz, norm1_g, w_in, lam_re, lam_im, log_dt, b_re, b_im, c_re, c_im, d_skip, w_glu,
           b_glu, mu_rwkv, w0, w2, a0, a2, g2, k_k, k_a, r_k, lnx_g, lnx_b, w_branch, w_out,
           norm2_g, w_ff1, w_ff2, out_g):
    d_ssm = d_skip.shape[0]
    w_in = w_in.astype(BF16)
    o_a, *mix = _mix_in(x2, x2.shape[0] // bsz, norm1_g, w_in, d_ssm, mu_rwkv, w0, w2, a0, a2, g2,
                        k_k, k_a, r_k.reshape(-1),
                        (lam_re, lam_im, log_dt, b_re, b_im, c_re, c_im, d_skip, w_glu, b_glu))
    o_b, zg = _rwkv_branch(mix, w_in[:, d_ssm + mu_rwkv.shape[0]:], bsz, lnx_g, lnx_b)
    return _tail(x2, o_a, o_b, zg, w_branch.astype(BF16), w_out.astype(BF16), norm2_g,
                 w_ff1.astype(BF16), w_ff2.astype(BF16), out_g)


def kernel(x, norm1_g, w_in, lam_re, lam_im, log_dt, b_re, b_im, c_re, c_im, d_skip, w_glu, b_glu,
           mu_rwkv, w0, w2, a0, a2, g2, k_k, k_a, r_k, lnx_g, lnx_b, w_branch, w_out, norm2_g,
           w_ff1, w_ff2, norm_f_g):
    bsz, seq, d = x.shape
    depth = norm1_g.shape[0]
    assert depth == 1, "the fused tail applies the final norm, so exactly one layer is supported"
    x2 = x.reshape(bsz * seq, d)
    l = 0
    out = _layer(x2, bsz, norm1_g[l], w_in[l], lam_re[l], lam_im[l], log_dt[l], b_re[l], b_im[l],
                 c_re[l], c_im[l], d_skip[l], w_glu[l], b_glu[l], mu_rwkv[l], w0[l], w2[l], a0[l],
                 a2[l], g2[l], k_k[l], k_a[l], r_k[l], lnx_g[l], lnx_b[l], w_branch[l], w_out[l],
                 norm2_g[l], w_ff1[l], w_ff2[l], norm_f_g)
    return out.reshape(bsz, seq, d)
```

```python
import functools

import jax
import jax.numpy as jnp
from jax import lax
from jax.experimental import pallas as pl
from jax.experimental.pallas import tpu as pltpu

F32 = jnp.float32
BF16 = jnp.bfloat16

RMS_EPS = 1e-6
GN_EPS = 64e-5
L2_EPS = 1e-12
MIN_NEG_REAL = -1e-4
EXP_NEG_HALF = 0.6065306597126334

SSM_GROUP = 16
SSM_STATE = 64
RWKV_HEAD = 64
GATE_LORA = 128
DECAY_LORA = 64
AAA_LORA = 64
CHUNK = 64

SUBLANES = 8
LANES = 128
VMEM_LIMIT = 56 * 1024 * 1024

TM_PROJ = 512
TT_S5 = 256
TT_RWKV = 256
S5_LANE_CHUNK = 512


def _dot(a, b):
    return jnp.dot(a, b, preferred_element_type=F32)


def _dot_nt(a, b):
    return lax.dot_general(a, b, (((1,), (1,)), ((), ())), preferred_element_type=F32)


def _split2(x):
    hi = x.astype(BF16)
    lo = (x - hi.astype(F32)).astype(BF16)
    return hi, lo


def _split3(x):
    hi = x.astype(BF16)
    r1 = x - hi.astype(F32)
    lo = r1.astype(BF16)
    lo2 = (r1 - lo.astype(F32)).astype(BF16)
    return hi, lo, lo2


def _rmsnorm(x, g):
    ms = jnp.mean(x * x, axis=-1, keepdims=True)
    return x * lax.rsqrt(ms + RMS_EPS) * g


def _cmul(ar, ai, br, bi):
    return ar * br - ai * bi, ar * bi + ai * br


def _s5_prep_kernel(lre_ref, lim_ref, ldt_ref, lre_w_ref, lim_w_ref, ldt_w_ref, bre_ref, bim_ref,
                    a1_re_ref, a1_im_ref, a8_re_ref, a8_im_ref, apb_re_ref, apb_im_ref,
                    alp_re_ref, alp_im_ref, shl_re_ref, shl_im_ref, bbre_ref, bbim_ref, *, seg_len):
    def discretise(lre, lim, ldt):
        lr = jnp.minimum(lre, MIN_NEG_REAL)
        dt = jnp.exp(ldt)
        mag = jnp.exp(lr * dt)
        ab_re = mag * jnp.cos(lim * dt)
        ab_im = mag * jnp.sin(lim * dt)
        den = lr * lr + lim * lim
        xm1 = ab_re - 1.0
        q_re = (xm1 * lr + ab_im * lim) / den
        q_im = (ab_im * lr - xm1 * lim) / den
        return ab_re, ab_im, q_re, q_im

    shape = a1_re_ref.shape
    a_re, a_im, _, _ = discretise(lre_ref[...], lim_ref[...], ldt_ref[...])
    bcast = lambda x: jnp.broadcast_to(x, shape)
    pows = [(a_re, a_im)]
    for _ in range(SUBLANES - 1):
        pows.append(_cmul(*pows[-1], a_re, a_im))
    a1_re_ref[...] = bcast(a_re)
    a1_im_ref[...] = bcast(a_im)
    a8_re_ref[...] = bcast(pows[-1][0])
    a8_im_ref[...] = bcast(pows[-1][1])
    shape16 = (2 * SUBLANES,) + shape[1:]
    first = lax.broadcasted_iota(jnp.int32, shape16, 0) < SUBLANES
    for q in range(SUBLANES // 2):
        (r0, i0), (r1, i1) = pows[2 * q], pows[2 * q + 1]
        bc16 = lambda x: jnp.broadcast_to(x, shape16)
        apb_re_ref[q] = jnp.where(first, bc16(r0), bc16(r1)).astype(BF16)
        apb_im_ref[q] = jnp.where(first, bc16(i0), bc16(i1)).astype(BF16)
    lr_, li_ = pows[-1]
    n = SUBLANES
    while n < seg_len:
        lr_, li_ = _cmul(lr_, li_, lr_, li_)
        n *= 2
    lpows = [(lr_, li_)]
    for _ in range(SUBLANES - 1):
        lpows.append(_cmul(*lpows[-1], lr_, li_))
    row = lax.broadcasted_iota(jnp.int32, shape, 0)
    alp_re = jnp.zeros(shape, F32)
    alp_im = jnp.zeros(shape, F32)
    for i, (pr, pi) in enumerate(lpows):
        alp_re = jnp.where(row == i, bcast(pr), alp_re)
        alp_im = jnp.where(row == i, bcast(pi), alp_im)
    alp_re_ref[...] = alp_re
    alp_im_ref[...] = alp_im
    for i, s in enumerate((1, 2, 4)):
        pr, pi = lpows[s - 1]
        shl_re_ref[i] = jnp.where(row >= s, bcast(pr), 0.0)
        shl_im_ref[i] = jnp.where(row >= s, bcast(pi), 0.0)

    _, _, q_re, q_im = discretise(lre_w_ref[...], lim_w_ref[...], ldt_w_ref[...])
    br = bre_ref[...]
    bi = bim_ref[...]
    bbre_ref[...] = q_re * br - q_im * bi
    bbim_ref[...] = q_re * bi + q_im * br


def _s5_prep(lam_re, lam_im, log_dt, b_re, b_im, seg_len):
    g, p = lam_re.shape
    m = b_re.shape[-1]
    n = g * p
    row = lambda a: a.reshape(1, n)
    wide = lambda a: jnp.repeat(a, m, axis=-1)
    ldt_gp = jnp.broadcast_to(log_dt[:, None], (g, p))
    s8 = jax.ShapeDtypeStruct((SUBLANES, n), F32)
    s88 = jax.ShapeDtypeStruct((SUBLANES // 2, 2 * SUBLANES, n), BF16)
    s38 = jax.ShapeDtypeStruct((3, SUBLANES, n), F32)
    sw = jax.ShapeDtypeStruct((g, p * m), F32)
    out_shape = (s8, s8, s8, s8, s88, s88, s8, s8, s38, s38, sw, sw)
    return pl.pallas_call(functools.partial(_s5_prep_kernel, seg_len=seg_len),
                          out_shape=out_shape, name="s5_prep")(
        row(lam_re), row(lam_im), row(ldt_gp), wide(lam_re), wide(lam_im), wide(ldt_gp),
        b_re.reshape(g, p * m), b_im.reshape(g, p * m))


def _block_diag(blocks):
    g, r, c = blocks.shape
    eye = jnp.eye(g, dtype=bool)
    return jnp.where(eye[:, None, :, None], blocks[:, :, None, :], 0.0).reshape(g * r, g * c)


def _head_sum(x, ones):
    rows, width = x.shape
    groups = width // LANES
    stacked = jnp.concatenate([x[:, j * LANES:(j + 1) * LANES] for j in range(groups)], axis=0)
    sums = _dot(stacked.astype(BF16), ones)
    return jnp.concatenate([sums[j * rows:(j + 1) * rows] for j in range(groups)], axis=1)


def _mix_in_kernel(*refs, d_ssm, d, tiles_per_seq, tt_s5):
    (x_ref, g_ref, w_ref, mu_ref, wwa_ref, g2_ref, w0_ref, a0_ref, kk_ref, ka_ref, rk_ref,
     ones_ref) = refs[:12]
    s5 = dict(zip(S5_CONSTS, refs[12:12 + len(S5_CONSTS)]))
    (oa_ref, r_ref, k2_ref, v_ref, av_ref, bv_ref, logw_ref, gate_ref, bonus_ref, h_ref,
     prev_ref, carry_ref, *state_refs) = refs[12 + len(S5_CONSTS):]
    xs_refs, xb_refs = state_refs[:len(state_refs) // 2], state_refs[len(state_refs) // 2:]
    tm = x_ref.shape[0]

    @pl.when(pl.program_id(0) % tiles_per_seq == 0)
    def _():
        prev_ref[...] = jnp.zeros_like(prev_ref)
        carry_ref[...] = jnp.zeros_like(carry_ref)

    h = _rmsnorm(x_ref[...], g_ref[...]).astype(BF16)
    h_ref[...] = h

    def mixed(c0, c1):
        p = _dot(h, w_ref[:, d_ssm + c0:d_ssm + c1])
        shape8 = (SUBLANES, c1 - c0)
        row = lax.broadcasted_iota(jnp.int32, shape8, 0)
        last = prev_ref[SUBLANES - 1:SUBLANES, c0:c1]
        prev = pltpu.roll(p, 1, 0)
        head = jnp.where(row == 0, jnp.broadcast_to(last, shape8), prev[:SUBLANES])
        prev = jnp.concatenate([head, prev[SUBLANES:]], axis=0)
        prev_ref[:, c0:c1] = p[tm - SUBLANES:, :]
        return p + (prev - p) * mu_ref[:, c0:c1]

    sub = lambda i: slice(i * tt_s5, (i + 1) * tt_s5)
    n_sub = tm // tt_s5
    us = [_dot(h[sub(i)], w_ref[:, :d_ssm]) for i in range(n_sub)]
    for i in range(n_sub):
        _s5_project(us[i], s5, xs_refs[i])
    _s5_scan(s5, xs_refs[0], xb_refs[0], carry_ref)
    lora_in = mixed(3 * d, 3 * d + GATE_LORA + DECAY_LORA + AAA_LORA)
    k = mixed(d, 2 * d)
    xg = lora_in[:, :GATE_LORA]
    xwa = lora_in[:, GATE_LORA:]
    lane = lax.broadcasted_iota(jnp.int32, xwa.shape, 1)
    zwa = jnp.where(lane < DECAY_LORA, jnp.tanh(xwa), xwa)
    lora = _dot(zwa.astype(BF16), wwa_ref[...])
    gate_ref[...] = _dot(jax.nn.sigmoid(xg).astype(BF16), g2_ref[...])
    logw_ref[...] = -EXP_NEG_HALF * jax.nn.sigmoid(w0_ref[...] + lora[:, :d])
    a = jax.nn.sigmoid(a0_ref[...] + lora[:, d:])
    for i in range(1, n_sub):
        _s5_scan(s5, xs_refs[i], xb_refs[i], carry_ref)
    oa_ref[sub(0), :] = _s5_readout(us[0], s5, xb_refs[0])

    ones = ones_ref[...]
    r = mixed(0, d)
    kk = k * kk_ref[...]
    kk = kk * lax.rsqrt(jnp.maximum(_head_sum(kk * kk, ones), L2_EPS * L2_EPS))
    v = mixed(2 * d, 3 * d)
    k2 = k * (1.0 + (a - 1.0) * ka_ref[...])
    k2_ref[...] = k2.astype(BF16)
    av_ref[...] = (-kk).astype(BF16)
    bv_ref[...] = (kk * a).astype(BF16)
    r_ref[...] = r.astype(BF16)
    v_ref[...] = v.astype(BF16)
    bonus_ref[...] = _head_sum(r * k2 * rk_ref[...], ones) * v
    for i in range(1, n_sub):
        oa_ref[sub(i), :] = _s5_readout(us[i], s5, xb_refs[i])


def _mix_in(x2, seq, norm_g, w_in, d_ssm, mu, w0, w2, a0, a2, g2, k_k, k_a, r_k, s5_params):
    t, dm = x2.shape
    d = w0.shape[0]
    tm = min(TM_PROJ, seq)
    tt_s5 = min(TT_S5, tm)
    s5_consts = _s5_constants(tt_s5, *s5_params)
    n_state2 = s5_consts[2].shape[2]
    wwa = jnp.concatenate([jnp.concatenate([w2, jnp.zeros_like(w2)], axis=1),
                           jnp.concatenate([jnp.zeros_like(a2), a2], axis=1)], axis=0)
    head = jnp.arange(LANES) // RWKV_HEAD
    ones = (head[:, None] == head[None, :]).astype(BF16)
    vec = lambda a: a.reshape(1, -1)
    consts = [vec(norm_g), w_in, vec(mu), wwa.astype(BF16), g2.astype(BF16), vec(w0), vec(a0),
              vec(k_k), vec(k_a), vec(r_k), ones, *s5_consts]
    whole = lambda a: pl.BlockSpec(a.shape, lambda i: (0,) * a.ndim)
    tok = lambda i: (i, 0)
    widths = [(d_ssm, F32)] + [(d, BF16)] * 5 + [(d, F32)] * 3 + [(dm, BF16)]
    return pl.pallas_call(
        functools.partial(_mix_in_kernel, d_ssm=d_ssm, d=d, tiles_per_seq=seq // tm, tt_s5=tt_s5),
        grid=(t // tm,),
        in_specs=[pl.BlockSpec((tm, dm), tok)] + [whole(c) for c in consts],
        out_specs=[pl.BlockSpec((tm, w), tok) for w, _ in widths],
        out_shape=[jax.ShapeDtypeStruct((t, w), dt) for w, dt in widths],
        scratch_shapes=[pltpu.VMEM((SUBLANES, mu.shape[0]), F32),
                        pltpu.VMEM((SUBLANES, 2 * n_state2), F32)]
        + [pltpu.VMEM((tt_s5, 2 * n_state2), F32)] * (tm // tt_s5)
        + [pltpu.VMEM((tt_s5, 2 * n_state2), BF16)] * (tm // tt_s5),
        compiler_params=pltpu.CompilerParams(dimension_semantics=("arbitrary",),
                                             vmem_limit_bytes=VMEM_LIMIT),
        name="mix_in",
    )(x2, *consts)


S5_CONSTS = ("perm", "permt", "bmat", "cmat", "a1_re", "a1_im", "a8_re", "a8_im", "apb_re", "apb_im",
             "alp_re", "alp_im", "shl_re", "shl_im", "d_skip", "w_glu", "b_glu")


def _s5_project(u, s5, xs_ref):
    n_state = xs_ref.shape[1] // 2
    half = n_state // 2
    lw = S5_LANE_CHUNK
    ub = _dot(s5["perm"][...], u.astype(BF16)).astype(BF16)
    d_half = ub.shape[1] // 2
    for c in range(n_state // lw):
        h, cc = divmod(c, half // lw)
        for part in range(2):
            xs_ref[:, pl.ds(part * n_state + c * lw, lw)] = _dot(
                ub[:, h * d_half:(h + 1) * d_half],
                s5["bmat"][h, :, pl.ds(part * half + cc * lw, lw)])


def _s5_scan(s5, xs_ref, xb_ref, carry_ref):
    tt = xs_ref.shape[0]
    n_state = xs_ref.shape[1] // 2
    seg_len = tt // SUBLANES
    lw = S5_LANE_CHUNK
    blk = lambda j: pl.ds(j * SUBLANES, SUBLANES)
    for c in range(n_state // lw):
        re_sl, im_sl = pl.ds(c * lw, lw), pl.ds(n_state + c * lw, lw)
        ar, ai = s5["a1_re"][:, re_sl], s5["a1_im"][:, re_sl]
        xr = jnp.zeros((SUBLANES, lw), F32)
        xi = jnp.zeros((SUBLANES, lw), F32)
        for j in range(seg_len):
            br = xs_ref[blk(j), re_sl]
            bi = xs_ref[blk(j), im_sl]
            xr, xi = ar * xr - ai * xi + br, ar * xi + ai * xr + bi
            xs_ref[blk(j), re_sl] = xr
            xs_ref[blk(j), im_sl] = xi
        er, ei = xr, xi
        for k, sh in enumerate((1, 2, 4)):
            pr, pi = s5["shl_re"][k, :, re_sl], s5["shl_im"][k, :, re_sl]
            sr, si = pltpu.roll(er, sh, 0), pltpu.roll(ei, sh, 0)
            er, ei = er + (pr * sr - pi * si), ei + (pr * si + pi * sr)
        tr, ti = carry_ref[:, re_sl], carry_ref[:, im_sl]
        pr, pi = s5["alp_re"][:, re_sl], s5["alp_im"][:, re_sl]
        fr = er + (pr * tr - pi * ti)
        fi = ei + (pr * ti + pi * tr)
        row = lax.broadcasted_iota(jnp.int32, fr.shape, 0)
        dr = jnp.where(row == 0, tr, pltpu.roll(fr, 1, 0))
        di = jnp.where(row == 0, ti, pltpu.roll(fi, 1, 0))
        last = SUBLANES - 1
        carry_ref[:, re_sl] = jnp.broadcast_to(fr[last:last + 1], fr.shape)
        carry_ref[:, im_sl] = jnp.broadcast_to(fi[last:last + 1], fi.shape)
        a8r, a8i = s5["a8_re"][:, re_sl], s5["a8_im"][:, re_sl]
        both = lambda x: jnp.concatenate([x, x], axis=0).astype(BF16)
        for a in range(seg_len // SUBLANES):
            dr2, di2 = both(dr), both(di)
            for q in range(SUBLANES // 2):
                rows = pl.ds((a * SUBLANES + 2 * q) * SUBLANES, 2 * SUBLANES)
                pr, pi = s5["apb_re"][q, :, re_sl], s5["apb_im"][q, :, re_sl]
                xb_ref[rows, re_sl] = xs_ref[rows, re_sl].astype(BF16) + (pr * dr2 - pi * di2)
                xb_ref[rows, im_sl] = xs_ref[rows, im_sl].astype(BF16) + (pr * di2 + pi * dr2)
            dr, di = a8r * dr - a8i * di, a8r * di + a8i * dr


def _s5_readout(u, s5, xb_ref):
    n_state = xb_ref.shape[1] // 2
    half = n_state // 2
    lw = S5_LANE_CHUNK
    n_chunks = n_state // lw
    width = u.shape[1] // n_chunks
    blocks = []
    for c in range(n_chunks):
        h, cc = divmod(c, half // lw)
        cols = pl.ds(cc * width, width)
        blocks.append(sum(_dot(xb_ref[:, pl.ds(part * n_state + c * lw, lw)],
                               s5["cmat"][h, pl.ds(part * half + cc * lw, lw), cols])
                          for part in range(2)))
    cx_hi, cx_lo = _split2(jnp.concatenate(blocks, axis=1))
    permt = s5["permt"][...]
    y = _dot(permt, cx_hi) + _dot(permt, cx_lo) + s5["d_skip"][...] * u
    z = jax.nn.gelu(y)
    gate = jax.nn.sigmoid(_dot(z.astype(BF16), s5["w_glu"][...]) + s5["b_glu"][...])
    return z * gate


def _s5_constants(tt, lam_re, lam_im, log_dt, b_re, b_im, c_re, c_im, d_skip, w_glu, b_glu):
    d_ssm = d_skip.shape[0]
    n_grp, n_state, grp = b_re.shape
    seg_len = tt // SUBLANES
    *scan_consts, bb_re, bb_im = _s5_prep(lam_re, lam_im, log_dt, b_re, b_im, seg_len)
    to_blocks = lambda bb: jnp.swapaxes(bb.reshape(n_grp, n_state, grp), 1, 2)
    hg = n_grp // 2
    halves = lambda blocks: [_block_diag(blocks[h * hg:(h + 1) * hg]) for h in range(2)]
    b_re_h, b_im_h = halves(to_blocks(bb_re)), halves(to_blocks(bb_im))
    c_re_h, c_im_h = halves(jnp.swapaxes(c_re, 1, 2)), halves(jnp.swapaxes(-c_im, 1, 2))
    bmat = jnp.stack([jnp.concatenate([b_re_h[h], b_im_h[h]], axis=1)
                      for h in range(2)]).astype(BF16)
    cmat = jnp.stack([jnp.concatenate([c_re_h[h], c_im_h[h]], axis=0)
                      for h in range(2)]).astype(BF16)
    rho = jnp.arange(tt)
    src = (rho % SUBLANES) * seg_len + rho // SUBLANES
    perm = (src[:, None] == jnp.arange(tt)[None, :]).astype(BF16)
    return [perm, perm.T, bmat, cmat, *scan_consts, d_skip.reshape(1, d_ssm), w_glu.astype(BF16),
            b_glu.reshape(1, d_ssm)]


def _pair_blocks(x):
    lane = lax.broadcasted_iota(jnp.int32, x.shape, 1)
    first = lane < RWKV_HEAD
    return jnp.concatenate([jnp.where(first, x, 0.0), jnp.where(first, 0.0, x)], axis=0)


def _pair_transpose(x):
    t = _pair_blocks(x).T
    return t[:CHUNK] + t[CHUNK:]


def _rwkv_chunks_scores(items):
    shape = (CHUNK, LANES)
    row = lax.broadcasted_iota(jnp.int32, shape, 0)
    col = lax.broadcasted_iota(jnp.int32, shape, 1) % RWKV_HEAD
    strict = row > col
    incl = row >= col
    eye = row == col
    blocks = lambda x: _pair_blocks(x).astype(BF16)

    for it in items:
        lhs = jnp.concatenate([it["at"], it["rt"]], axis=0).astype(BF16)
        rhs = jnp.concatenate([blocks(it["bt"]), blocks(it["kt"])], axis=0)
        sc = _dot_nt(lhs, rhs)
        it["s_ab"] = jnp.where(strict, sc[:CHUNK, :LANES], 0.0)
        it["s_ak"] = jnp.where(strict, sc[:CHUNK, LANES:], 0.0).astype(BF16)
        it["s_r"] = jnp.concatenate([jnp.where(incl, sc[CHUNK:, :LANES], 0.0),
                                     jnp.where(incl, sc[CHUNK:, LANES:], 0.0)], axis=1).astype(BF16)
        it["bd_v"] = blocks(it["v"])
        it["bk_t"] = jnp.concatenate([_pair_transpose(it["bh"]), _pair_transpose(it["kh"])],
                                     axis=1).astype(BF16)

    return strict, incl, eye, blocks


def _rwkv_chunks_rest(items, h_ref, eye, blocks, filler):
    shape = (CHUNK, LANES)
    for it in items:
        it["npow"] = _dot(it["s_ab"].astype(BF16), blocks(it["s_ab"]))
        it["inv"] = jnp.where(eye, 1.0, 0.0) + it["s_ab"]
    filler()
    for step in range(4):
        for it in items:
            both = _dot(jnp.concatenate([it["inv"], it["npow"]], axis=0).astype(BF16),
                        blocks(it["npow"]))
            it["inv"] = it["inv"] + both[:CHUNK]
            it["npow"] = both[CHUNK:]
        filler()
    for it in items:
        it["inv"] = it["inv"] + _dot(it["inv"].astype(BF16), blocks(it["npow"]))
    filler()

    for it in items:
        it["av"] = _dot(it["s_ak"], it["bd_v"])
    for it in items:
        wu = _dot(it["inv"].astype(BF16),
                  jnp.concatenate([blocks(it["at"]), blocks(it["av"])], axis=1))
        it["z"] = jnp.concatenate(
            [jnp.concatenate([blocks(wu[:, :LANES]), blocks(wu[:, LANES:])], axis=1),
             jnp.concatenate([jnp.zeros((LANES, LANES), BF16), it["bd_v"]], axis=1)], axis=0)
    filler()
    for it in items:
        both = _dot(jnp.concatenate([it["s_r"], it["bk_t"]], axis=0), it["z"])
        qy, mg = both[:CHUNK], both[CHUNK:]
        m = jnp.where(eye, jnp.broadcast_to(it["decay"], shape), 0.0) + mg[:, :LANES]
        it["qm"] = jnp.concatenate([it["rt"] + qy[:, :LANES], m], axis=0).astype(BF16)
        it["yp"] = qy[:, LANES:]
        it["g"] = mg[:, LANES:]
    filler()

    for it in items:
        both = _dot(it["qm"], blocks(h_ref[it["pair"]]))
        it["y"] = both[:CHUNK] + it["yp"]
        h_ref[it["pair"]] = both[CHUNK:] + it["g"]


OPERANDS = ("at", "rt", "bt", "kt", "bh", "kh", "v")


def _rwkv_kernel(r_ref, k2_ref, v_ref, av_ref, bv_ref, logw_ref, gate_ref, bonus_ref, hn_ref,
                 win_ref, lng_ref, lnb_ref, ones_ref, ltri_ref, o_ref, zg_ref, h_ref, decay_ref,
                 *op_refs):
    tt, d = r_ref.shape
    n_pairs = d // LANES
    n_chunks = tt // CHUNK
    ops = dict(zip(OPERANDS, op_refs))

    @pl.when(pl.program_id(1) == 0)
    def _():
        for ref in (h_ref, decay_ref) + tuple(op_refs):
            ref[...] = jnp.zeros_like(ref)

    items = []
    for c in range(n_chunks):
        rows = pl.ds(c * CHUNK, CHUNK)
        for j in range(n_pairs):
            ln = pl.ds(j * LANES, LANES)
            item = {name: ref[rows, ln] for name, ref in ops.items()}
            item["decay"] = decay_ref[c, :, ln]
            item["pair"] = j
            items.append(item)
    _, _, eye, blocks = _rwkv_chunks_scores(items)

    for c in range(n_chunks):
        rows = pl.ds(c * CHUNK, CHUNK)
        load = lambda ref: ref[rows, :].astype(F32)
        r, k2, v, av, bv, logw = map(load, (r_ref, k2_ref, v_ref, av_ref, bv_ref, logw_ref))
        l_hi, l_lo = _split2(logw)
        lc = _dot(ltri_ref[...], l_hi) + _dot(ltri_ref[...], l_lo)
        lc_end = lc[CHUNK - 1:CHUNK]
        e_in = jnp.exp(lc - logw)
        e_inc = jnp.exp(lc)
        e_neg = jnp.exp(-lc)
        e_out = jnp.exp(lc_end - lc)
        new = dict(at=av * e_in, rt=r * e_inc, bt=bv * e_neg, kt=k2 * e_neg, bh=bv * e_out,
                   kh=k2 * e_out, v=v)
        for name, val in new.items():
            ops[name][rows, :] = val
        decay_ref[c] = jnp.exp(lc_end)

    n_fill = 8
    piece = zg_ref.shape[1] // n_fill
    gate_col0 = win_ref.shape[1] - zg_ref.shape[1]
    pending = list(range(n_fill))

    def filler():
        if pending:
            q = pending.pop(0)
            zg_ref[:, pl.ds(q * piece, piece)] = _dot(
                hn_ref[...], win_ref[:, pl.ds(gate_col0 + q * piece, piece)]).astype(BF16)

    _rwkv_chunks_rest(items, h_ref, eye, blocks, filler)
    while pending:
        filler()
    ones = ones_ref[...]
    inv_n = 1.0 / RWKV_HEAD
    for c in range(n_chunks):
        rows = pl.ds(c * CHUNK, CHUNK)
        y = jnp.concatenate([it["y"] for it in items[c * n_pairs:(c + 1) * n_pairs]], axis=1)
        mean = _head_sum(y, ones) * inv_n
        yc = y - mean
        var = _head_sum(yc * yc, ones) * inv_n
        yn = yc * lax.rsqrt(var + GN_EPS) * lng_ref[...] + lnb_ref[...]
        o_ref[rows, :] = (yn + bonus_ref[rows, :]) * gate_ref[rows, :]


def _rwkv_branch(mix, w_in, n_gate, bsz, lnx_g, lnx_b):
    *step_in, gate, bonus, hn = mix
    t, d = gate.shape
    seq = t // bsz
    tt = min(TT_RWKV, seq)
    nt = seq // tt
    head = jnp.arange(LANES) // RWKV_HEAD
    ones = (head[:, None] == head[None, :]).astype(BF16)
    tok_i = jnp.arange(CHUNK)
    ltri = (tok_i[:, None] >= tok_i[None, :]).astype(BF16)
    this_tile = lambda b, i: (b * nt + jnp.minimum(i, nt - 1), 0)
    prev_tile = lambda b, i: (b * nt + jnp.maximum(i - 1, 0), 0)
    c2 = lambda b, i: (0, 0)
    consts = [lnx_g.reshape(1, d), lnx_b.reshape(1, d), ones, ltri]
    return pl.pallas_call(
        _rwkv_kernel,
        grid=(bsz, nt + 1),
        in_specs=[pl.BlockSpec((tt, d), this_tile) for _ in step_in]
        + [pl.BlockSpec((tt, d), prev_tile)] * 2 + [pl.BlockSpec((tt, hn.shape[1]), prev_tile)]
        + [pl.BlockSpec(w_in.shape, c2, pipeline_mode=pl.Buffered(1))]
        + [pl.BlockSpec(c.shape, c2) for c in consts],
        out_specs=[pl.BlockSpec((tt, d), prev_tile), pl.BlockSpec((tt, n_gate), prev_tile)],
        out_shape=[jax.ShapeDtypeStruct((t, d), F32), jax.ShapeDtypeStruct((t, n_gate), BF16)],
        scratch_shapes=[pltpu.VMEM((d // LANES, RWKV_HEAD, LANES), F32),
                        pltpu.VMEM((tt // CHUNK, 1, d), F32)]
        + [pltpu.VMEM((tt, d), F32)] * len(OPERANDS),
        compiler_params=pltpu.CompilerParams(dimension_semantics=("arbitrary", "arbitrary"),
                                             vmem_limit_bytes=VMEM_LIMIT),
        name="rwkv",
    )(*step_in, gate, bonus, hn, w_in, *consts)


def _tail_kernel(x_ref, oa_ref, ob_ref, zg_ref, wb_ref, wout_ref, g2_ref,
                 w1_ref, w2_ref, gf_ref, o_ref, *, ff_block):
    x = x_ref[...]
    d = x.shape[1]
    d_mix = oa_ref.shape[1]
    gates = jax.nn.sigmoid(zg_ref[...].astype(F32))
    ma = _dot(oa_ref[...].astype(BF16), wb_ref[:d_mix, :])
    mb = _dot(ob_ref[...].astype(BF16), wb_ref[d_mix:, :])
    merged = gates[:, :d] * ma + gates[:, d:] * mb
    x1 = x + _dot(merged.astype(BF16), wout_ref[...])
    h2 = _rmsnorm(x1, g2_ref[...]).astype(BF16)
    acc = x1
    for j in range(w1_ref.shape[1] // ff_block):
        cols = slice(j * ff_block, (j + 1) * ff_block)
        hid = jnp.square(jnp.maximum(_dot(h2, w1_ref[:, cols]), 0.0))
        acc = acc + _dot(hid.astype(BF16), w2_ref[cols, :])
    o_ref[...] = _rmsnorm(acc, gf_ref[...])


def _tail(x2, o_a, o_b, zg, w_branch, w_out, norm2_g, w_ff1, w_ff2, norm_f_g):
    t, d = x2.shape
    d_mix = o_a.shape[1]
    tm = min(TM_PROJ, t)
    tok = lambda i: (i, 0)
    const = lambda i: (0, 0)
    resident = lambda a: pl.BlockSpec(a.shape, const, pipeline_mode=pl.Buffered(1))
    vec = lambda a: a.reshape(1, d)
    return pl.pallas_call(
        functools.partial(_tail_kernel, ff_block=d),
        grid=(t // tm,),
        in_specs=[pl.BlockSpec((tm, d), tok), pl.BlockSpec((tm, d_mix), tok),
                  pl.BlockSpec((tm, d_mix), tok), pl.BlockSpec((tm, zg.shape[1]), tok),
                  resident(w_branch), resident(w_out),
                  pl.BlockSpec((1, d), const), resident(w_ff1), resident(w_ff2),
                  pl.BlockSpec((1, d), const)],
        out_specs=pl.BlockSpec((tm, d), tok),
        out_shape=jax.ShapeDtypeStruct((t, d), F32),
        compiler_params=pltpu.CompilerParams(dimension_semantics=("arbitrary",),
                                             vmem_limit_bytes=VMEM_LIMIT),
        name="tail",
    )(x2, o_a, o_b, zg, w_branch, w_out, vec(norm2_g), w_ff1, w_ff2, vec(norm_f_g))


def _layer(x2, bsz, norm1_g, w_in, lam_re, lam_im, log_dt, b_re, b_im, c_re, c_im, d_skip, w_glu,
           b_glu, mu_rwkv, w0, w2, a0, a2, g2, k_k, k_a, r_k, lnx_g, lnx_b, w_branch, w_out,
           norm2_g, w_ff1, w_ff2, out_g):
    d_ssm = d_skip.shape[0]
    w_in = w_in.astype(BF16)
    o_a, *mix = _mix_in(x2, x2.shape[0] // bsz, norm1_g, w_in, d_ssm, mu_rwkv, w0, w2, a0, a2, g2,
                        k_k, k_a, r_k.reshape(-1),
                        (lam_re, lam_im, log_dt, b_re, b_im, c_re, c_im, d_skip, w_glu, b_glu))
    o_b, zg = _rwkv_branch(mix, w_in, w_in.shape[1] - d_ssm - mu_rwkv.shape[0], bsz, lnx_g, lnx_b)
    return _tail(x2, o_a, o_b, zg, w_branch.astype(BF16), w_out.astype(BF16), norm2_g,
                 w_ff1.astype(BF16), w_ff2.astype(BF16), out_g)


def kernel(x, norm1_g, w_in, lam_re, lam_im, log_dt, b_re, b_im, c_re, c_im, d_skip, w_glu, b_glu,
           mu_rwkv, w0, w2, a0, a2, g2, k_k, k_a, r_k, lnx_g, lnx_b, w_branch, w_out, norm2_g,
           w_ff1, w_ff2, norm_f_g):
    bsz, seq, d = x.shape
    depth = norm1_g.shape[0]
    assert depth == 1, "the fused tail applies the final norm, so exactly one layer is supported"
    x2 = x.reshape(bsz * seq, d)
    l = 0
    out = _layer(x2, bsz, norm1_g[l], w_in[l], lam_re[l], lam_im[l], log_dt[l], b_re[l], b_im[l],
                 c_re[l], c_im[l], d_skip[l], w_glu[l], b_glu[l], mu_rwkv[l], w0[l], w2[l], a0[l],
                 a2[l], g2[l], k_k[l], k_a[l], r_k[l], lnx_g[l], lnx_b[l], w_branch[l], w_out[l],
                 norm2_g[l], w_ff1[l], w_ff2[l], norm_f_g)
    return out.reshape(bsz, seq, d)
```

```python
import functools

import jax
import jax.numpy as jnp
from jax import lax
from jax.experimental import pallas as pl
from jax.experimental.pallas import tpu as pltpu

F32 = jnp.float32
BF16 = jnp.bfloat16

RMS_EPS = 1e-6
GN_EPS = 64e-5
L2_EPS = 1e-12
MIN_NEG_REAL = -1e-4
EXP_NEG_HALF = 0.6065306597126334

SSM_GROUP = 16
SSM_STATE = 64
RWKV_HEAD = 64
GATE_LORA = 128
DECAY_LORA = 64
AAA_LORA = 64
CHUNK = 64

SUBLANES = 8
LANES = 128
BF16_TILE_ROWS = 2 * SUBLANES
VMEM_LIMIT = 56 * 1024 * 1024

TM_PROJ = 512
TT_S5 = 256
TT_RWKV = 256
S5_LANE_CHUNK = 512


def _dot(a, b):
    return jnp.dot(a, b, preferred_element_type=F32)


def _dot_nt(a, b):
    return lax.dot_general(a, b, (((1,), (1,)), ((), ())), preferred_element_type=F32)


def _split2(x):
    hi = x.astype(BF16)
    lo = (x - hi.astype(F32)).astype(BF16)
    return hi, lo


def _split3(x):
    hi = x.astype(BF16)
    r1 = x - hi.astype(F32)
    lo = r1.astype(BF16)
    lo2 = (r1 - lo.astype(F32)).astype(BF16)
    return hi, lo, lo2


def _rmsnorm(x, g):
    ms = jnp.mean(x * x, axis=-1, keepdims=True)
    return x * lax.rsqrt(ms + RMS_EPS) * g


def _cmul(ar, ai, br, bi):
    return ar * br - ai * bi, ar * bi + ai * br


def _s5_prep_kernel(lre_ref, lim_ref, ldt_ref, lre_w_ref, lim_w_ref, ldt_w_ref, bre_ref, bim_ref,
                    a1_re_ref, a1_im_ref, a8_re_ref, a8_im_ref, apb_re_ref, apb_im_ref,
                    alp_re_ref, alp_im_ref, shl_re_ref, shl_im_ref, bbre_ref, bbim_ref, *, seg_len):
    def discretise(lre, lim, ldt):
        lr = jnp.minimum(lre, MIN_NEG_REAL)
        dt = jnp.exp(ldt)
        mag = jnp.exp(lr * dt)
        ab_re = mag * jnp.cos(lim * dt)
        ab_im = mag * jnp.sin(lim * dt)
        den = lr * lr + lim * lim
        xm1 = ab_re - 1.0
        q_re = (xm1 * lr + ab_im * lim) / den
        q_im = (ab_im * lr - xm1 * lim) / den
        return ab_re, ab_im, q_re, q_im

    shape = a1_re_ref.shape
    a_re, a_im, _, _ = discretise(lre_ref[...], lim_ref[...], ldt_ref[...])
    bcast = lambda x: jnp.broadcast_to(x, shape)
    pows = [(a_re, a_im)]
    for _ in range(SUBLANES - 1):
        pows.append(_cmul(*pows[-1], a_re, a_im))
    a1_re_ref[...] = bcast(a_re)
    a1_im_ref[...] = bcast(a_im)
    a8_re_ref[...] = bcast(pows[-1][0])
    a8_im_ref[...] = bcast(pows[-1][1])
    shape16 = (2 * SUBLANES,) + shape[1:]
    first = lax.broadcasted_iota(jnp.int32, shape16, 0) < SUBLANES
    for q in range(SUBLANES // 2):
        (r0, i0), (r1, i1) = pows[2 * q], pows[2 * q + 1]
        bc16 = lambda x: jnp.broadcast_to(x, shape16)
        apb_re_ref[q] = jnp.where(first, bc16(r0), bc16(r1)).astype(BF16)
        apb_im_ref[q] = jnp.where(first, bc16(i0), bc16(i1)).astype(BF16)
    lr_, li_ = pows[-1]
    n = SUBLANES
    while n < seg_len:
        lr_, li_ = _cmul(lr_, li_, lr_, li_)
        n *= 2
    lpows = [(lr_, li_)]
    for _ in range(SUBLANES - 1):
        lpows.append(_cmul(*lpows[-1], lr_, li_))
    row = lax.broadcasted_iota(jnp.int32, shape, 0)
    alp_re = jnp.zeros(shape, F32)
    alp_im = jnp.zeros(shape, F32)
    for i, (pr, pi) in enumerate(lpows):
        alp_re = jnp.where(row == i, bcast(pr), alp_re)
        alp_im = jnp.where(row == i, bcast(pi), alp_im)
    alp_re_ref[...] = alp_re
    alp_im_ref[...] = alp_im
    for i, s in enumerate((1, 2, 4)):
        pr, pi = lpows[s - 1]
        shl_re_ref[i] = jnp.where(row >= s, bcast(pr), 0.0)
        shl_im_ref[i] = jnp.where(row >= s, bcast(pi), 0.0)

    _, _, q_re, q_im = discretise(lre_w_ref[...], lim_w_ref[...], ldt_w_ref[...])
    br = bre_ref[...]
    bi = bim_ref[...]
    bbre_ref[...] = q_re * br - q_im * bi
    bbim_ref[...] = q_re * bi + q_im * br


def _s5_prep(lam_re, lam_im, log_dt, b_re, b_im, seg_len):
    g, p = lam_re.shape
    m = b_re.shape[-1]
    n = g * p
    row = lambda a: a.reshape(1, n)
    wide = lambda a: jnp.repeat(a, m, axis=-1)
    ldt_gp = jnp.broadcast_to(log_dt[:, None], (g, p))
    s8 = jax.ShapeDtypeStruct((SUBLANES, n), F32)
    s88 = jax.ShapeDtypeStruct((SUBLANES // 2, 2 * SUBLANES, n), BF16)
    s38 = jax.ShapeDtypeStruct((3, SUBLANES, n), F32)
    sw = jax.ShapeDtypeStruct((g, p * m), F32)
    out_shape = (s8, s8, s8, s8, s88, s88, s8, s8, s38, s38, sw, sw)
    return pl.pallas_call(functools.partial(_s5_prep_kernel, seg_len=seg_len),
                          out_shape=out_shape, name="s5_prep")(
        row(lam_re), row(lam_im), row(ldt_gp), wide(lam_re), wide(lam_im), wide(ldt_gp),
        b_re.reshape(g, p * m), b_im.reshape(g, p * m))


def _block_diag(blocks):
    g, r, c = blocks.shape
    eye = jnp.eye(g, dtype=bool)
    return jnp.where(eye[:, None, :, None], blocks[:, :, None, :], 0.0).reshape(g * r, g * c)


def _head_sum(x, ones):
    rows, width = x.shape
    groups = width // LANES
    stacked = jnp.concatenate([x[:, j * LANES:(j + 1) * LANES] for j in range(groups)], axis=0)
    sums = _dot(stacked.astype(BF16), ones)
    return jnp.concatenate([sums[j * rows:(j + 1) * rows] for j in range(groups)], axis=1)


def _mix_in_kernel(*refs, d_ssm, d, tiles_per_seq, tt_s5):
    (x_ref, g_ref, w_ref, mu_ref, wwa_ref, g2_ref, w0_ref, a0_ref, kk_ref, ka_ref, rk_ref,
     ones_ref) = refs[:12]
    s5 = dict(zip(S5_CONSTS, refs[12:12 + len(S5_CONSTS)]))
    (oa_ref, r_ref, k2_ref, v_ref, av_ref, bv_ref, logw_ref, gate_ref, bonus_ref, h_ref,
     prev_ref, carry_ref, *state_refs) = refs[12 + len(S5_CONSTS):]
    xs_refs, xb_refs = state_refs[:len(state_refs) // 2], state_refs[len(state_refs) // 2:]
    tm = x_ref.shape[0]

    @pl.when(pl.program_id(0) % tiles_per_seq == 0)
    def _():
        prev_ref[...] = jnp.zeros_like(prev_ref)
        carry_ref[...] = jnp.zeros_like(carry_ref)

    h = _rmsnorm(x_ref[...], g_ref[...]).astype(BF16)
    h_ref[...] = h

    def mixed(c0, c1):
        p = _dot(h, w_ref[:, d_ssm + c0:d_ssm + c1])
        shape8 = (SUBLANES, c1 - c0)
        row = lax.broadcasted_iota(jnp.int32, shape8, 0)
        last = prev_ref[SUBLANES - 1:SUBLANES, c0:c1]
        prev = pltpu.roll(p, 1, 0)
        head = jnp.where(row == 0, jnp.broadcast_to(last, shape8), prev[:SUBLANES])
        prev = jnp.concatenate([head, prev[SUBLANES:]], axis=0)
        prev_ref[:, c0:c1] = p[tm - SUBLANES:, :]
        return p + (prev - p) * mu_ref[:, c0:c1]

    sub = lambda i: slice(i * tt_s5, (i + 1) * tt_s5)
    n_sub = tm // tt_s5
    us = [_dot(h[sub(i)], w_ref[:, :d_ssm]) for i in range(n_sub)]
    for i in range(n_sub):
        _s5_project(us[i], s5, xs_refs[i])
    _s5_scan(s5, xs_refs[0], xb_refs[0], carry_ref)
    lora_in = mixed(3 * d, 3 * d + GATE_LORA + DECAY_LORA + AAA_LORA)
    k = mixed(d, 2 * d)
    xg = lora_in[:, :GATE_LORA]
    xwa = lora_in[:, GATE_LORA:]
    lane = lax.broadcasted_iota(jnp.int32, xwa.shape, 1)
    zwa = jnp.where(lane < DECAY_LORA, jnp.tanh(xwa), xwa)
    lora = _dot(zwa.astype(BF16), wwa_ref[...])
    gate_ref[...] = _dot(jax.nn.sigmoid(xg).astype(BF16), g2_ref[...])
    logw_ref[...] = -EXP_NEG_HALF * jax.nn.sigmoid(w0_ref[...] + lora[:, :d])
    a = jax.nn.sigmoid(a0_ref[...] + lora[:, d:])
    for i in range(1, n_sub):
        _s5_scan(s5, xs_refs[i], xb_refs[i], carry_ref)
    oa_ref[sub(0), :] = _s5_readout(us[0], s5, xb_refs[0])

    ones = ones_ref[...]
    r = mixed(0, d)
    kk = k * kk_ref[...]
    kk = kk * lax.rsqrt(jnp.maximum(_head_sum(kk * kk, ones), L2_EPS * L2_EPS))
    v = mixed(2 * d, 3 * d)
    k2 = k * (1.0 + (a - 1.0) * ka_ref[...])
    k2_ref[...] = k2.astype(BF16)
    av_ref[...] = (-kk).astype(BF16)
    bv_ref[...] = (kk * a).astype(BF16)
    r_ref[...] = r.astype(BF16)
    v_ref[...] = v.astype(BF16)
    bonus_ref[...] = _head_sum(r * k2 * rk_ref[...], ones) * v
    for i in range(1, n_sub):
        oa_ref[sub(i), :] = _s5_readout(us[i], s5, xb_refs[i])


def _mix_in(x2, seq, norm_g, w_in, d_ssm, mu, w0, w2, a0, a2, g2, k_k, k_a, r_k, s5_params):
    t, dm = x2.shape
    d = w0.shape[0]
    tm = min(TM_PROJ, seq)
    tt_s5 = min(TT_S5, tm)
    s5_consts = _s5_constants(tt_s5, *s5_params)
    n_state2 = s5_consts[2].shape[2]
    wwa = jnp.concatenate([jnp.concatenate([w2, jnp.zeros_like(w2)], axis=1),
                           jnp.concatenate([jnp.zeros_like(a2), a2], axis=1)], axis=0)
    head = jnp.arange(LANES) // RWKV_HEAD
    ones = (head[:, None] == head[None, :]).astype(BF16)
    vec = lambda a: a.reshape(1, -1)
    consts = [vec(norm_g), w_in, vec(mu), wwa.astype(BF16), g2.astype(BF16), vec(w0), vec(a0),
              vec(k_k), vec(k_a), vec(r_k), ones, *s5_consts]
    whole = lambda a: pl.BlockSpec(a.shape, lambda i: (0,) * a.ndim)
    tok = lambda i: (i, 0)
    widths = [(d_ssm, F32)] + [(d, BF16)] * 5 + [(d, F32)] * 3 + [(dm, BF16)]
    return pl.pallas_call(
        functools.partial(_mix_in_kernel, d_ssm=d_ssm, d=d, tiles_per_seq=seq // tm, tt_s5=tt_s5),
        grid=(t // tm,),
        in_specs=[pl.BlockSpec((tm, dm), tok)] + [whole(c) for c in consts],
        out_specs=[pl.BlockSpec((tm, w), tok) for w, _ in widths],
        out_shape=[jax.ShapeDtypeStruct((t, w), dt) for w, dt in widths],
        scratch_shapes=[pltpu.VMEM((SUBLANES, mu.shape[0]), F32),
                        pltpu.VMEM((SUBLANES, 2 * n_state2), F32)]
        + [pltpu.VMEM((tt_s5, 2 * n_state2), F32)] * (tm // tt_s5)
        + [pltpu.VMEM((tt_s5, 2 * n_state2), BF16)] * (tm // tt_s5),
        compiler_params=pltpu.CompilerParams(dimension_semantics=("arbitrary",),
                                             vmem_limit_bytes=VMEM_LIMIT),
        name="mix_in",
    )(x2, *consts)


S5_CONSTS = ("perm", "permt", "bmat", "cmat", "a1_re", "a1_im", "a8_re", "a8_im", "apb_re", "apb_im",
             "alp_re", "alp_im", "shl_re", "shl_im", "d_skip", "w_glu", "b_glu")


def _s5_project(u, s5, xs_ref):
    n_state = xs_ref.shape[1] // 2
    half = n_state // 2
    lw = S5_LANE_CHUNK
    ub = _dot(s5["perm"][...], u.astype(BF16)).astype(BF16)
    d_half = ub.shape[1] // 2
    for c in range(n_state // lw):
        h, cc = divmod(c, half // lw)
        for part in range(2):
            xs_ref[:, pl.ds(part * n_state + c * lw, lw)] = _dot(
                ub[:, h * d_half:(h + 1) * d_half],
                s5["bmat"][h, :, pl.ds(part * half + cc * lw, lw)])


def _s5_scan(s5, xs_ref, xb_ref, carry_ref):
    tt = xs_ref.shape[0]
    n_state = xs_ref.shape[1] // 2
    seg_len = tt // SUBLANES
    lw = S5_LANE_CHUNK
    blk = lambda j: pl.ds(j * SUBLANES, SUBLANES)
    for c in range(n_state // lw):
        re_sl, im_sl = pl.ds(c * lw, lw), pl.ds(n_state + c * lw, lw)
        ar, ai = s5["a1_re"][:, re_sl], s5["a1_im"][:, re_sl]
        xr = jnp.zeros((SUBLANES, lw), F32)
        xi = jnp.zeros((SUBLANES, lw), F32)
        for j in range(seg_len):
            br = xs_ref[blk(j), re_sl]
            bi = xs_ref[blk(j), im_sl]
            xr, xi = ar * xr - ai * xi + br, ar * xi + ai * xr + bi
            xs_ref[blk(j), re_sl] = xr
            xs_ref[blk(j), im_sl] = xi
        er, ei = xr, xi
        for k, sh in enumerate((1, 2, 4)):
            pr, pi = s5["shl_re"][k, :, re_sl], s5["shl_im"][k, :, re_sl]
            sr, si = pltpu.roll(er, sh, 0), pltpu.roll(ei, sh, 0)
            er, ei = er + (pr * sr - pi * si), ei + (pr * si + pi * sr)
        tr, ti = carry_ref[:, re_sl], carry_ref[:, im_sl]
        pr, pi = s5["alp_re"][:, re_sl], s5["alp_im"][:, re_sl]
        fr = er + (pr * tr - pi * ti)
        fi = ei + (pr * ti + pi * tr)
        row = lax.broadcasted_iota(jnp.int32, fr.shape, 0)
        dr = jnp.where(row == 0, tr, pltpu.roll(fr, 1, 0))
        di = jnp.where(row == 0, ti, pltpu.roll(fi, 1, 0))
        last = SUBLANES - 1
        carry_ref[:, re_sl] = jnp.broadcast_to(fr[last:last + 1], fr.shape)
        carry_ref[:, im_sl] = jnp.broadcast_to(fi[last:last + 1], fi.shape)
        a8r, a8i = s5["a8_re"][:, re_sl], s5["a8_im"][:, re_sl]
        both = lambda x: jnp.concatenate([x, x], axis=0).astype(BF16)
        for a in range(seg_len // SUBLANES):
            dr2, di2 = both(dr), both(di)
            for q in range(SUBLANES // 2):
                rows = pl.ds((a * SUBLANES + 2 * q) * SUBLANES, 2 * SUBLANES)
                pr, pi = s5["apb_re"][q, :, re_sl], s5["apb_im"][q, :, re_sl]
                xb_ref[rows, re_sl] = xs_ref[rows, re_sl].astype(BF16) + (pr * dr2 - pi * di2)
                xb_ref[rows, im_sl] = xs_ref[rows, im_sl].astype(BF16) + (pr * di2 + pi * dr2)
            dr, di = a8r * dr - a8i * di, a8r * di + a8i * dr


def _s5_readout(u, s5, xb_ref):
    n_state = xb_ref.shape[1] // 2
    half = n_state // 2
    lw = S5_LANE_CHUNK
    n_chunks = n_state // lw
    width = u.shape[1] // n_chunks
    blocks = []
    for c in range(n_chunks):
        h, cc = divmod(c, half // lw)
        cols = pl.ds(cc * width, width)
        blocks.append(sum(_dot(xb_ref[:, pl.ds(part * n_state + c * lw, lw)],
                               s5["cmat"][h, pl.ds(part * half + cc * lw, lw), cols])
                          for part in range(2)))
    cx_hi, cx_lo = _split2(jnp.concatenate(blocks, axis=1))
    permt = s5["permt"][...]
    y = _dot(permt, cx_hi) + _dot(permt, cx_lo) + s5["d_skip"][...] * u
    z = jax.nn.gelu(y)
    gate = jax.nn.sigmoid(_dot(z.astype(BF16), s5["w_glu"][...]) + s5["b_glu"][...])
    return z * gate


def _s5_constants(tt, lam_re, lam_im, log_dt, b_re, b_im, c_re, c_im, d_skip, w_glu, b_glu):
    d_ssm = d_skip.shape[0]
    n_grp, n_state, grp = b_re.shape
    seg_len = tt // SUBLANES
    *scan_consts, bb_re, bb_im = _s5_prep(lam_re, lam_im, log_dt, b_re, b_im, seg_len)
    to_blocks = lambda bb: jnp.swapaxes(bb.reshape(n_grp, n_state, grp), 1, 2)
    hg = n_grp // 2
    halves = lambda blocks: [_block_diag(blocks[h * hg:(h + 1) * hg]) for h in range(2)]
    b_re_h, b_im_h = halves(to_blocks(bb_re)), halves(to_blocks(bb_im))
    c_re_h, c_im_h = halves(jnp.swapaxes(c_re, 1, 2)), halves(jnp.swapaxes(-c_im, 1, 2))
    bmat = jnp.stack([jnp.concatenate([b_re_h[h], b_im_h[h]], axis=1)
                      for h in range(2)]).astype(BF16)
    cmat = jnp.stack([jnp.concatenate([c_re_h[h], c_im_h[h]], axis=0)
                      for h in range(2)]).astype(BF16)
    rho = jnp.arange(tt)
    src = (rho % SUBLANES) * seg_len + rho // SUBLANES
    perm = (src[:, None] == jnp.arange(tt)[None, :]).astype(BF16)
    return [perm, perm.T, bmat, cmat, *scan_consts, d_skip.reshape(1, d_ssm), w_glu.astype(BF16),
            b_glu.reshape(1, d_ssm)]


def _pair_blocks(x):
    lane = lax.broadcasted_iota(jnp.int32, x.shape, 1)
    first = lane < RWKV_HEAD
    return jnp.concatenate([jnp.where(first, x, 0.0), jnp.where(first, 0.0, x)], axis=0)


def _pair_transpose(x):
    t = _pair_blocks(x).T
    return t[:CHUNK] + t[CHUNK:]


def _rwkv_chunks_scores(items):
    shape = (CHUNK, LANES)
    row = lax.broadcasted_iota(jnp.int32, shape, 0)
    col = lax.broadcasted_iota(jnp.int32, shape, 1) % RWKV_HEAD
    strict = row > col
    incl = row >= col
    eye = row == col
    blocks = lambda x: _pair_blocks(x).astype(BF16)

    for it in items:
        lhs = jnp.concatenate([it["at"], it["rt"]], axis=0).astype(BF16)
        rhs = jnp.concatenate([blocks(it["bt"]), blocks(it["kt"])], axis=0)
        sc = _dot_nt(lhs, rhs)
        it["s_ab"] = jnp.where(strict, sc[:CHUNK, :LANES], 0.0)
        it["s_ak"] = jnp.where(strict, sc[:CHUNK, LANES:], 0.0).astype(BF16)
        it["s_r"] = jnp.concatenate([jnp.where(incl, sc[CHUNK:, :LANES], 0.0),
                                     jnp.where(incl, sc[CHUNK:, LANES:], 0.0)], axis=1).astype(BF16)
        it["bd_v"] = blocks(it["v"])
        it["bk_t"] = jnp.concatenate([_pair_transpose(it["bh"]), _pair_transpose(it["kh"])],
                                     axis=1).astype(BF16)

    return strict, incl, eye, blocks


def _rwkv_chunks_rest(items, h_ref, eye, blocks, filler):
    shape = (CHUNK, LANES)
    for it in items:
        it["npow"] = _dot(it["s_ab"].astype(BF16), blocks(it["s_ab"]))
        it["inv"] = jnp.where(eye, 1.0, 0.0) + it["s_ab"]
    filler()
    for step in range(4):
        for it in items:
            both = _dot(jnp.concatenate([it["inv"], it["npow"]], axis=0).astype(BF16),
                        blocks(it["npow"]))
            it["inv"] = it["inv"] + both[:CHUNK]
            it["npow"] = both[CHUNK:]
        filler()
    for it in items:
        it["inv"] = it["inv"] + _dot(it["inv"].astype(BF16), blocks(it["npow"]))
    filler()

    for it in items:
        it["av"] = _dot(it["s_ak"], it["bd_v"])
    for it in items:
        wu = _dot(it["inv"].astype(BF16),
                  jnp.concatenate([blocks(it["at"]), blocks(it["av"])], axis=1))
        it["z"] = jnp.concatenate(
            [jnp.concatenate([blocks(wu[:, :LANES]), blocks(wu[:, LANES:])], axis=1),
             jnp.concatenate([jnp.zeros((LANES, LANES), BF16), it["bd_v"]], axis=1)], axis=0)
    filler()
    for it in items:
        both = _dot(jnp.concatenate([it["s_r"], it["bk_t"]], axis=0), it["z"])
        qy, mg = both[:CHUNK], both[CHUNK:]
        m = jnp.where(eye, jnp.broadcast_to(it["decay"], shape), 0.0) + mg[:, :LANES]
        it["qm"] = jnp.concatenate([it["rt"] + qy[:, :LANES], m], axis=0).astype(BF16)
        it["yp"] = qy[:, LANES:]
        it["g"] = mg[:, LANES:]
    filler()

    for it in items:
        both = _dot(it["qm"], blocks(h_ref[it["pair"]]))
        it["y"] = both[:CHUNK] + it["yp"]
        h_ref[it["pair"]] = both[CHUNK:] + it["g"]


OPERANDS = ("at", "rt", "bt", "kt", "bh", "kh", "v")


def _rwkv_kernel(r_ref, k2_ref, v_ref, av_ref, bv_ref, logw_ref, gate_ref, bonus_ref, hn_ref,
                 win_ref, lng_ref, lnb_ref, ones_ref, ltri_ref, *refs, n_cast):
    cast_in, (o_ref, zg_ref) = refs[:n_cast], refs[n_cast:n_cast + 2]
    cast_out = refs[n_cast + 2:2 * n_cast + 2]
    h_ref, decay_ref, *op_refs = refs[2 * n_cast + 2:]
    for src, dst in zip(cast_in, cast_out):
        dst[...] = src[...].astype(BF16)
    tt, d = r_ref.shape
    n_pairs = d // LANES
    n_chunks = tt // CHUNK
    ops = dict(zip(OPERANDS, op_refs))

    @pl.when(pl.program_id(1) == 0)
    def _():
        for ref in (h_ref, decay_ref) + tuple(op_refs):
            ref[...] = jnp.zeros_like(ref)

    items = []
    for c in range(n_chunks):
        rows = pl.ds(c * CHUNK, CHUNK)
        for j in range(n_pairs):
            ln = pl.ds(j * LANES, LANES)
            item = {name: ref[rows, ln] for name, ref in ops.items()}
            item["decay"] = decay_ref[c, :, ln]
            item["pair"] = j
            items.append(item)
    _, _, eye, blocks = _rwkv_chunks_scores(items)

    for c in range(n_chunks):
        rows = pl.ds(c * CHUNK, CHUNK)
        load = lambda ref: ref[rows, :].astype(F32)
        r, k2, v, av, bv, logw = map(load, (r_ref, k2_ref, v_ref, av_ref, bv_ref, logw_ref))
        l_hi, l_lo = _split2(logw)
        lc = _dot(ltri_ref[...], l_hi) + _dot(ltri_ref[...], l_lo)
        lc_end = lc[CHUNK - 1:CHUNK]
        e_in = jnp.exp(lc - logw)
        e_inc = jnp.exp(lc)
        e_neg = jnp.exp(-lc)
        e_out = jnp.exp(lc_end - lc)
        new = dict(at=av * e_in, rt=r * e_inc, bt=bv * e_neg, kt=k2 * e_neg, bh=bv * e_out,
                   kh=k2 * e_out, v=v)
        for name, val in new.items():
            ops[name][rows, :] = val
        decay_ref[c] = jnp.exp(lc_end)

    n_fill = 8
    piece = zg_ref.shape[1] // n_fill
    gate_col0 = win_ref.shape[1] - zg_ref.shape[1]
    pending = list(range(n_fill))

    def filler():
        if pending:
            q = pending.pop(0)
            zg_ref[:, pl.ds(q * piece, piece)] = _dot(
                hn_ref[...], win_ref[:, pl.ds(gate_col0 + q * piece, piece)]).astype(BF16)

    _rwkv_chunks_rest(items, h_ref, eye, blocks, filler)
    while pending:
        filler()
    ones = ones_ref[...]
    inv_n = 1.0 / RWKV_HEAD
    for c in range(n_chunks):
        rows = pl.ds(c * CHUNK, CHUNK)
        y = jnp.concatenate([it["y"] for it in items[c * n_pairs:(c + 1) * n_pairs]], axis=1)
        mean = _head_sum(y, ones) * inv_n
        yc = y - mean
        var = _head_sum(yc * yc, ones) * inv_n
        yn = yc * lax.rsqrt(var + GN_EPS) * lng_ref[...] + lnb_ref[...]
        o_ref[rows, :] = (yn + bonus_ref[rows, :]) * gate_ref[rows, :]


def _rwkv_branch(mix, w_in, n_gate, bsz, lnx_g, lnx_b, later_weights):
    *step_in, gate, bonus, hn = mix
    t, d = gate.shape
    seq = t // bsz
    tt = min(TT_RWKV, seq)
    nt = seq // tt
    n_steps = bsz * (nt + 1)
    parts = max(p for p in range(1, n_steps + 1)
                if all(w.shape[0] % (p * BF16_TILE_ROWS) == 0 for w in later_weights))
    cast_block = lambda b, i: (jnp.minimum(b * (nt + 1) + i, parts - 1), 0)
    cast_specs = [pl.BlockSpec((w.shape[0] // parts, w.shape[1]), cast_block) for w in later_weights]
    head = jnp.arange(LANES) // RWKV_HEAD
    ones = (head[:, None] == head[None, :]).astype(BF16)
    tok_i = jnp.arange(CHUNK)
    ltri = (tok_i[:, None] >= tok_i[None, :]).astype(BF16)
    this_tile = lambda b, i: (b * nt + jnp.minimum(i, nt - 1), 0)
    prev_tile = lambda b, i: (b * nt + jnp.maximum(i - 1, 0), 0)
    c2 = lambda b, i: (0, 0)
    consts = [lnx_g.reshape(1, d), lnx_b.reshape(1, d), ones, ltri]
    return pl.pallas_call(
        functools.partial(_rwkv_kernel, n_cast=len(later_weights)),
        grid=(bsz, nt + 1),
        in_specs=[pl.BlockSpec((tt, d), this_tile) for _ in step_in]
        + [pl.BlockSpec((tt, d), prev_tile)] * 2 + [pl.BlockSpec((tt, hn.shape[1]), prev_tile)]
        + [pl.BlockSpec(w_in.shape, c2, pipeline_mode=pl.Buffered(1))]
        + [pl.BlockSpec(c.shape, c2) for c in consts] + cast_specs,
        out_specs=[pl.BlockSpec((tt, d), prev_tile), pl.BlockSpec((tt, n_gate), prev_tile)]
        + cast_specs,
        out_shape=[jax.ShapeDtypeStruct((t, d), F32), jax.ShapeDtypeStruct((t, n_gate), BF16)]
        + [jax.ShapeDtypeStruct(w.shape, BF16) for w in later_weights],
        scratch_shapes=[pltpu.VMEM((d // LANES, RWKV_HEAD, LANES), F32),
                        pltpu.VMEM((tt // CHUNK, 1, d), F32)]
        + [pltpu.VMEM((tt, d), F32)] * len(OPERANDS),
        compiler_params=pltpu.CompilerParams(dimension_semantics=("arbitrary", "arbitrary"),
                                             vmem_limit_bytes=VMEM_LIMIT),
        name="rwkv",
    )(*step_in, gate, bonus, hn, w_in, *consts, *later_weights)


def _tail_kernel(x_ref, oa_ref, ob_ref, zg_ref, wb_ref, wout_ref, g2_ref,
                 w1_ref, w2_ref, gf_ref, o_ref, *, ff_block):
    x = x_ref[...]
    d = x.shape[1]
    d_mix = oa_ref.shape[1]
    gates = jax.nn.sigmoid(zg_ref[...].astype(F32))
    ma = _dot(oa_ref[...].astype(BF16), wb_ref[:d_mix, :])
    mb = _dot(ob_ref[...].astype(BF16), wb_ref[d_mix:, :])
    merged = gates[:, :d] * ma + gates[:, d:] * mb
    x1 = x + _dot(merged.astype(BF16), wout_ref[...])
    h2 = _rmsnorm(x1, g2_ref[...]).astype(BF16)
    acc = x1
    for j in range(w1_ref.shape[1] // ff_block):
        cols = slice(j * ff_block, (j + 1) * ff_block)
        hid = jnp.square(jnp.maximum(_dot(h2, w1_ref[:, cols]), 0.0))
        acc = acc + _dot(hid.astype(BF16), w2_ref[cols, :])
    o_ref[...] = _rmsnorm(acc, gf_ref[...])


def _tail(x2, o_a, o_b, zg, w_branch, w_out, norm2_g, w_ff1, w_ff2, norm_f_g):
    t, d = x2.shape
    d_mix = o_a.shape[1]
    tm = min(TM_PROJ, t)
    tok = lambda i: (i, 0)
    const = lambda i: (0, 0)
    resident = lambda a: pl.BlockSpec(a.shape, const, pipeline_mode=pl.Buffered(1))
    vec = lambda a: a.reshape(1, d)
    return pl.pallas_call(
        functools.partial(_tail_kernel, ff_block=d),
        grid=(t // tm,),
        in_specs=[pl.BlockSpec((tm, d), tok), pl.BlockSpec((tm, d_mix), tok),
                  pl.BlockSpec((tm, d_mix), tok), pl.BlockSpec((tm, zg.shape[1]), tok),
                  resident(w_branch), resident(w_out),
                  pl.BlockSpec((1, d), const), resident(w_ff1), resident(w_ff2),
                  pl.BlockSpec((1, d), const)],
        out_specs=pl.BlockSpec((tm, d), tok),
        out_shape=jax.ShapeDtypeStruct((t, d), F32),
        compiler_params=pltpu.CompilerParams(dimension_semantics=("arbitrary",),
                                             vmem_limit_bytes=VMEM_LIMIT),
        name="tail",
    )(x2, o_a, o_b, zg, w_branch, w_out, vec(norm2_g), w_ff1, w_ff2, vec(norm_f_g))


def _layer(x2, bsz, norm1_g, w_in, lam_re, lam_im, log_dt, b_re, b_im, c_re, c_im, d_skip, w_glu,
           b_glu, mu_rwkv, w0, w2, a0, a2, g2, k_k, k_a, r_k, lnx_g, lnx_b, w_branch, w_out,
           norm2_g, w_ff1, w_ff2, out_g):
    d_ssm = d_skip.shape[0]
    w_in = w_in.astype(BF16)
    o_a, *mix = _mix_in(x2, x2.shape[0] // bsz, norm1_g, w_in, d_ssm, mu_rwkv, w0, w2, a0, a2, g2,
                        k_k, k_a, r_k.reshape(-1),
                        (lam_re, lam_im, log_dt, b_re, b_im, c_re, c_im, d_skip, w_glu, b_glu))
    o_b, zg, *tail_weights = _rwkv_branch(mix, w_in, w_in.shape[1] - d_ssm - mu_rwkv.shape[0], bsz,
                                          lnx_g, lnx_b, (w_branch, w_out, w_ff1, w_ff2))
    w_branch, w_out, w_ff1, w_ff2 = tail_weights
    return _tail(x2, o_a, o_b, zg, w_branch, w_out, norm2_g, w_ff1, w_ff2, out_g)


def kernel(x, norm1_g, w_in, lam_re, lam_im, log_dt, b_re, b_im, c_re, c_im, d_skip, w_glu, b_glu,
           mu_rwkv, w0, w2, a0, a2, g2, k_k, k_a, r_k, lnx_g, lnx_b, w_branch, w_out, norm2_g,
           w_ff1, w_ff2, norm_f_g):
    bsz, seq, d = x.shape
    depth = norm1_g.shape[0]
    assert depth == 1, "the fused tail applies the final norm, so exactly one layer is supported"
    x2 = x.reshape(bsz * seq, d)
    l = 0
    out = _layer(x2, bsz, norm1_g[l], w_in[l], lam_re[l], lam_im[l], log_dt[l], b_re[l], b_im[l],
                 c_re[l], c_im[l], d_skip[l], w_glu[l], b_glu[l], mu_rwkv[l], w0[l], w2[l], a0[l],
                 a2[l], g2[l], k_k[l], k_a[l], r_k[l], lnx_g[l], lnx_b[l], w_branch[l], w_out[l],
                 norm2_g[l], w_ff1[l], w_ff2[l], norm_f_g)
    return out.reshape(bsz, seq, d)
```

```python
import functools

import jax
import jax.numpy as jnp
import numpy as np
from jax import lax
from jax.experimental import pallas as pl
from jax.experimental.pallas import tpu as pltpu

F32 = jnp.float32
BF16 = jnp.bfloat16

RMS_EPS = 1e-6
GN_EPS = 64e-5
L2_EPS = 1e-12
MIN_NEG_REAL = -1e-4
EXP_NEG_HALF = 0.6065306597126334

SSM_GROUP = 16
SSM_STATE = 64
RWKV_HEAD = 64
GATE_LORA = 128
DECAY_LORA = 64
AAA_LORA = 64
CHUNK = 64

SUBLANES = 8
LANES = 128
BF16_TILE_ROWS = 2 * SUBLANES
VMEM_LIMIT = 56 * 1024 * 1024

TM_PROJ = 512
TT_S5 = 256
TT_RWKV = 256
S5_LANE_CHUNK = 512


def _dot(a, b):
    return jnp.dot(a, b, preferred_element_type=F32)


def _dot_nt(a, b):
    return lax.dot_general(a, b, (((1,), (1,)), ((), ())), preferred_element_type=F32)


def _split2(x):
    hi = x.astype(BF16)
    lo = (x - hi.astype(F32)).astype(BF16)
    return hi, lo


def _split3(x):
    hi = x.astype(BF16)
    r1 = x - hi.astype(F32)
    lo = r1.astype(BF16)
    lo2 = (r1 - lo.astype(F32)).astype(BF16)
    return hi, lo, lo2


def _rmsnorm(x, g):
    ms = jnp.mean(x * x, axis=-1, keepdims=True)
    return x * lax.rsqrt(ms + RMS_EPS) * g


def _cmul(ar, ai, br, bi):
    return ar * br - ai * bi, ar * bi + ai * br


def _s5_prep_kernel(rows_ref, grid_ref, a1_re_ref, a1_im_ref, a8_re_ref, a8_im_ref, apb_re_ref,
                    apb_im_ref, alp_re_ref, alp_im_ref, shl_re_ref, shl_im_ref, q_ref, *, seg_len):
    def discretise(lre, lim, ldt):
        lr = jnp.minimum(lre, MIN_NEG_REAL)
        dt = jnp.exp(ldt)
        mag = jnp.exp(lr * dt)
        ab_re = mag * jnp.cos(lim * dt)
        ab_im = mag * jnp.sin(lim * dt)
        den = lr * lr + lim * lim
        xm1 = ab_re - 1.0
        q_re = (xm1 * lr + ab_im * lim) / den
        q_im = (ab_im * lr - xm1 * lim) / den
        return ab_re, ab_im, q_re, q_im

    shape = a1_re_ref.shape
    a_re, a_im, _, _ = discretise(rows_ref[0], rows_ref[1], rows_ref[2])
    bcast = lambda x: jnp.broadcast_to(x, shape)
    pows = [(a_re, a_im)]
    for _ in range(SUBLANES - 1):
        pows.append(_cmul(*pows[-1], a_re, a_im))
    a1_re_ref[...] = bcast(a_re)
    a1_im_ref[...] = bcast(a_im)
    a8_re_ref[...] = bcast(pows[-1][0])
    a8_im_ref[...] = bcast(pows[-1][1])
    shape16 = (2 * SUBLANES,) + shape[1:]
    first = lax.broadcasted_iota(jnp.int32, shape16, 0) < SUBLANES
    for q in range(SUBLANES // 2):
        (r0, i0), (r1, i1) = pows[2 * q], pows[2 * q + 1]
        bc16 = lambda x: jnp.broadcast_to(x, shape16)
        apb_re_ref[q] = jnp.where(first, bc16(r0), bc16(r1)).astype(BF16)
        apb_im_ref[q] = jnp.where(first, bc16(i0), bc16(i1)).astype(BF16)
    lr_, li_ = pows[-1]
    n = SUBLANES
    while n < seg_len:
        lr_, li_ = _cmul(lr_, li_, lr_, li_)
        n *= 2
    lpows = [(lr_, li_)]
    for _ in range(SUBLANES - 1):
        lpows.append(_cmul(*lpows[-1], lr_, li_))
    row = lax.broadcasted_iota(jnp.int32, shape, 0)
    alp_re = jnp.zeros(shape, F32)
    alp_im = jnp.zeros(shape, F32)
    for i, (pr, pi) in enumerate(lpows):
        alp_re = jnp.where(row == i, bcast(pr), alp_re)
        alp_im = jnp.where(row == i, bcast(pi), alp_im)
    alp_re_ref[...] = alp_re
    alp_im_ref[...] = alp_im
    for i, s in enumerate((1, 2, 4)):
        pr, pi = lpows[s - 1]
        shl_re_ref[i] = jnp.where(row >= s, bcast(pr), 0.0)
        shl_im_ref[i] = jnp.where(row >= s, bcast(pi), 0.0)

    _, _, q_re, q_im = discretise(grid_ref[0], grid_ref[1], grid_ref[2])
    q_ref[0] = q_re
    q_ref[1] = q_im


def _s5_prep(lam_re, lam_im, log_dt, seg_len):
    g, p = lam_re.shape
    n = g * p
    lam3 = jnp.stack([lam_re, lam_im, jnp.broadcast_to(log_dt[:, None], (g, p))])
    s8 = jax.ShapeDtypeStruct((SUBLANES, n), F32)
    s88 = jax.ShapeDtypeStruct((SUBLANES // 2, 2 * SUBLANES, n), BF16)
    s38 = jax.ShapeDtypeStruct((3, SUBLANES, n), F32)
    sq = jax.ShapeDtypeStruct((2, g, p), F32)
    out_shape = (s8, s8, s8, s8, s88, s88, s8, s8, s38, s38, sq)
    return pl.pallas_call(functools.partial(_s5_prep_kernel, seg_len=seg_len),
                          out_shape=out_shape, name="s5_prep")(lam3.reshape(3, 1, n), lam3)


def _head_ones():
    head = np.arange(LANES) // RWKV_HEAD
    return jnp.asarray(head[:, None] == head[None, :], BF16)


def _head_sum(x, ones):
    rows, width = x.shape
    groups = width // LANES
    stacked = jnp.concatenate([x[:, j * LANES:(j + 1) * LANES] for j in range(groups)], axis=0)
    sums = _dot(stacked.astype(BF16), ones)
    return jnp.concatenate([sums[j * rows:(j + 1) * rows] for j in range(groups)], axis=1)


def _mix_in_kernel(*refs, d_ssm, d, tiles_per_seq, tt_s5):
    (x_ref, g_ref, w_ref, mu_ref, wwa_ref, g2_ref, w0_ref, a0_ref, kk_ref, ka_ref, rk_ref,
     ones_ref) = refs[:12]
    s5 = dict(zip(S5_CONSTS, refs[12:12 + len(S5_CONSTS)]))
    (oa_ref, r_ref, k2_ref, v_ref, av_ref, bv_ref, logw_ref, gate_ref, bonus_ref, h_ref,
     prev_ref, carry_ref, *state_refs) = refs[12 + len(S5_CONSTS):]
    xs_refs, xb_refs = state_refs[:len(state_refs) // 2], state_refs[len(state_refs) // 2:]
    tm = x_ref.shape[0]

    @pl.when(pl.program_id(0) % tiles_per_seq == 0)
    def _():
        prev_ref[...] = jnp.zeros_like(prev_ref)
        carry_ref[...] = jnp.zeros_like(carry_ref)

    h = _rmsnorm(x_ref[...], g_ref[...]).astype(BF16)
    h_ref[...] = h

    def mixed(c0, c1):
        p = _dot(h, w_ref[:, d_ssm + c0:d_ssm + c1])
        shape8 = (SUBLANES, c1 - c0)
        row = lax.broadcasted_iota(jnp.int32, shape8, 0)
        last = prev_ref[SUBLANES - 1:SUBLANES, c0:c1]
        prev = pltpu.roll(p, 1, 0)
        head = jnp.where(row == 0, jnp.broadcast_to(last, shape8), prev[:SUBLANES])
        prev = jnp.concatenate([head, prev[SUBLANES:]], axis=0)
        prev_ref[:, c0:c1] = p[tm - SUBLANES:, :]
        return p + (prev - p) * mu_ref[:, c0:c1]

    sub = lambda i: slice(i * tt_s5, (i + 1) * tt_s5)
    n_sub = tm // tt_s5
    us = [_dot(h[sub(i)], w_ref[:, :d_ssm]) for i in range(n_sub)]
    for i in range(n_sub):
        _s5_project(us[i], s5, xs_refs[i])
    _s5_scan(s5, xs_refs[0], xb_refs[0], carry_ref)
    lora_in = mixed(3 * d, 3 * d + GATE_LORA + DECAY_LORA + AAA_LORA)
    k = mixed(d, 2 * d)
    xg = lora_in[:, :GATE_LORA]
    xwa = lora_in[:, GATE_LORA:]
    lane = lax.broadcasted_iota(jnp.int32, xwa.shape, 1)
    zwa = jnp.where(lane < DECAY_LORA, jnp.tanh(xwa), xwa)
    lora = _dot(zwa.astype(BF16), wwa_ref[...])
    gate_ref[...] = _dot(jax.nn.sigmoid(xg).astype(BF16), g2_ref[...])
    logw_ref[...] = -EXP_NEG_HALF * jax.nn.sigmoid(w0_ref[...] + lora[:, :d])
    a = jax.nn.sigmoid(a0_ref[...] + lora[:, d:])
    for i in range(1, n_sub):
        _s5_scan(s5, xs_refs[i], xb_refs[i], carry_ref)
    oa_ref[sub(0), :] = _s5_readout(us[0], s5, xb_refs[0])

    ones = ones_ref[...]
    r = mixed(0, d)
    kk = k * kk_ref[...]
    kk = kk * lax.rsqrt(jnp.maximum(_head_sum(kk * kk, ones), L2_EPS * L2_EPS))
    v = mixed(2 * d, 3 * d)
    k2 = k * (1.0 + (a - 1.0) * ka_ref[...])
    k2_ref[...] = k2.astype(BF16)
    av_ref[...] = (-kk).astype(BF16)
    bv_ref[...] = (kk * a).astype(BF16)
    r_ref[...] = r.astype(BF16)
    v_ref[...] = v.astype(BF16)
    bonus_ref[...] = _head_sum(r * k2 * rk_ref[...], ones) * v
    for i in range(1, n_sub):
        oa_ref[sub(i), :] = _s5_readout(us[i], s5, xb_refs[i])


def _mix_in(x2, seq, norm_g, w_in, d_ssm, mu, w0, w2, a0, a2, g2, k_k, k_a, r_k, s5_params):
    t, dm = x2.shape
    d = w0.shape[0]
    tm = min(TM_PROJ, seq)
    tt_s5 = min(TT_S5, tm)
    s5_consts = _s5_constants(tt_s5, *s5_params)
    n_state2 = s5_consts[2].shape[2]
    wwa = jnp.concatenate([jnp.concatenate([w2, jnp.zeros_like(w2)], axis=1),
                           jnp.concatenate([jnp.zeros_like(a2), a2], axis=1)], axis=0)
    ones = _head_ones()
    vec = lambda a: a.reshape(1, -1)
    consts = [vec(norm_g), w_in, vec(mu), wwa.astype(BF16), g2.astype(BF16), vec(w0), vec(a0),
              vec(k_k), vec(k_a), vec(r_k), ones, *s5_consts]
    whole = lambda a: pl.BlockSpec(a.shape, lambda i: (0,) * a.ndim)
    tok = lambda i: (i, 0)
    widths = [(d_ssm, F32)] + [(d, BF16)] * 5 + [(d, F32)] * 3 + [(dm, BF16)]
    return pl.pallas_call(
        functools.partial(_mix_in_kernel, d_ssm=d_ssm, d=d, tiles_per_seq=seq // tm, tt_s5=tt_s5),
        grid=(t // tm,),
        in_specs=[pl.BlockSpec((tm, dm), tok)] + [whole(c) for c in consts],
        out_specs=[pl.BlockSpec((tm, w), tok) for w, _ in widths],
        out_shape=[jax.ShapeDtypeStruct((t, w), dt) for w, dt in widths],
        scratch_shapes=[pltpu.VMEM((SUBLANES, mu.shape[0]), F32),
                        pltpu.VMEM((SUBLANES, 2 * n_state2), F32)]
        + [pltpu.VMEM((tt_s5, 2 * n_state2), F32)] * (tm // tt_s5)
        + [pltpu.VMEM((tt_s5, 2 * n_state2), BF16)] * (tm // tt_s5),
        compiler_params=pltpu.CompilerParams(dimension_semantics=("arbitrary",),
                                             vmem_limit_bytes=VMEM_LIMIT),
        name="mix_in",
    )(x2, *consts)


S5_CONSTS = ("perm", "permt", "bmat", "cmat", "a1_re", "a1_im", "a8_re", "a8_im", "apb_re", "apb_im",
             "alp_re", "alp_im", "shl_re", "shl_im", "d_skip", "w_glu", "b_glu")


def _s5_project(u, s5, xs_ref):
    n_state = xs_ref.shape[1] // 2
    half = n_state // 2
    lw = S5_LANE_CHUNK
    ub = _dot(s5["perm"][...], u.astype(BF16)).astype(BF16)
    d_half = ub.shape[1] // 2
    for c in range(n_state // lw):
        h, cc = divmod(c, half // lw)
        for part in range(2):
            xs_ref[:, pl.ds(part * n_state + c * lw, lw)] = _dot(
                ub[:, h * d_half:(h + 1) * d_half],
                s5["bmat"][h, :, pl.ds(part * half + cc * lw, lw)])


def _s5_scan(s5, xs_ref, xb_ref, carry_ref):
    tt = xs_ref.shape[0]
    n_state = xs_ref.shape[1] // 2
    seg_len = tt // SUBLANES
    lw = S5_LANE_CHUNK
    blk = lambda j: pl.ds(j * SUBLANES, SUBLANES)
    for c in range(n_state // lw):
        re_sl, im_sl = pl.ds(c * lw, lw), pl.ds(n_state + c * lw, lw)
        ar, ai = s5["a1_re"][:, re_sl], s5["a1_im"][:, re_sl]
        xr = jnp.zeros((SUBLANES, lw), F32)
        xi = jnp.zeros((SUBLANES, lw), F32)
        for j in range(seg_len):
            br = xs_ref[blk(j), re_sl]
            bi = xs_ref[blk(j), im_sl]
            xr, xi = ar * xr - ai * xi + br, ar * xi + ai * xr + bi
            xs_ref[blk(j), re_sl] = xr
            xs_ref[blk(j), im_sl] = xi
        er, ei = xr, xi
        for k, sh in enumerate((1, 2, 4)):
            pr, pi = s5["shl_re"][k, :, re_sl], s5["shl_im"][k, :, re_sl]
            sr, si = pltpu.roll(er, sh, 0), pltpu.roll(ei, sh, 0)
            er, ei = er + (pr * sr - pi * si), ei + (pr * si + pi * sr)
        tr, ti = carry_ref[:, re_sl], carry_ref[:, im_sl]
        pr, pi = s5["alp_re"][:, re_sl], s5["alp_im"][:, re_sl]
        fr = er + (pr * tr - pi * ti)
        fi = ei + (pr * ti + pi * tr)
        row = lax.broadcasted_iota(jnp.int32, fr.shape, 0)
        dr = jnp.where(row == 0, tr, pltpu.roll(fr, 1, 0))
        di = jnp.where(row == 0, ti, pltpu.roll(fi, 1, 0))
        last = SUBLANES - 1
        carry_ref[:, re_sl] = jnp.broadcast_to(fr[last:last + 1], fr.shape)
        carry_ref[:, im_sl] = jnp.broadcast_to(fi[last:last + 1], fi.shape)
        a8r, a8i = s5["a8_re"][:, re_sl], s5["a8_im"][:, re_sl]
        both = lambda x: jnp.concatenate([x, x], axis=0).astype(BF16)
        for a in range(seg_len // SUBLANES):
            dr2, di2 = both(dr), both(di)
            for q in range(SUBLANES // 2):
                rows = pl.ds((a * SUBLANES + 2 * q) * SUBLANES, 2 * SUBLANES)
                pr, pi = s5["apb_re"][q, :, re_sl], s5["apb_im"][q, :, re_sl]
                xb_ref[rows, re_sl] = xs_ref[rows, re_sl].astype(BF16) + (pr * dr2 - pi * di2)
                xb_ref[rows, im_sl] = xs_ref[rows, im_sl].astype(BF16) + (pr * di2 + pi * dr2)
            dr, di = a8r * dr - a8i * di, a8r * di + a8i * dr


def _s5_readout(u, s5, xb_ref):
    n_state = xb_ref.shape[1] // 2
    half = n_state // 2
    lw = S5_LANE_CHUNK
    n_chunks = n_state // lw
    width = u.shape[1] // n_chunks
    blocks = []
    for c in range(n_chunks):
        h, cc = divmod(c, half // lw)
        cols = pl.ds(cc * width, width)
        blocks.append(sum(_dot(xb_ref[:, pl.ds(part * n_state + c * lw, lw)],
                               s5["cmat"][h, pl.ds(part * half + cc * lw, lw), cols])
                          for part in range(2)))
    cx_hi, cx_lo = _split2(jnp.concatenate(blocks, axis=1))
    permt = s5["permt"][...]
    y = _dot(permt, cx_hi) + _dot(permt, cx_lo) + s5["d_skip"][...] * u
    z = jax.nn.gelu(y)
    gate = jax.nn.sigmoid(_dot(z.astype(BF16), s5["w_glu"][...]) + s5["b_glu"][...])
    return z * gate


def _s5_constants(tt, lam_re, lam_im, log_dt, b_re, b_im, c_re, c_im, d_skip, w_glu, b_glu):
    d_ssm = d_skip.shape[0]
    n_grp, n_state, grp = b_re.shape
    seg_len = tt // SUBLANES
    *scan_consts, q = _s5_prep(lam_re, lam_im, log_dt, seg_len)
    q_re, q_im = q[0][:, :, None], q[1][:, :, None]
    bb = jnp.stack([q_re * b_re - q_im * b_im, q_re * b_im + q_im * b_re])
    hg = n_grp // 2
    same = np.eye(hg, dtype=bool)
    bb = bb.reshape(2, 2, hg, n_state, grp).transpose(1, 2, 4, 0, 3)
    bmat = jnp.where(same[:, None, None, :, None], bb.astype(BF16)[:, :, :, :, None, :], 0.0)
    bmat = bmat.reshape(2, hg * grp, 2 * hg * n_state)
    cc = jnp.stack([c_re, -c_im])
    cc = cc.reshape(2, 2, hg, grp, n_state).transpose(1, 0, 2, 4, 3)
    cmat = jnp.where(same[:, None, :, None], cc.astype(BF16)[:, :, :, :, None, :], 0.0)
    cmat = cmat.reshape(2, 2 * hg * n_state, hg * grp)
    rho = np.arange(tt)
    src = (rho % SUBLANES) * seg_len + rho // SUBLANES
    perm = src[:, None] == np.arange(tt)[None, :]
    return [jnp.asarray(perm, BF16), jnp.asarray(perm.T, BF16), bmat, cmat, *scan_consts,
            d_skip.reshape(1, d_ssm), w_glu.astype(BF16), b_glu.reshape(1, d_ssm)]


def _pair_blocks(x):
    lane = lax.broadcasted_iota(jnp.int32, x.shape, 1)
    first = lane < RWKV_HEAD
    return jnp.concatenate([jnp.where(first, x, 0.0), jnp.where(first, 0.0, x)], axis=0)


def _pair_transpose(x):
    t = _pair_blocks(x).T
    return t[:CHUNK] + t[CHUNK:]


def _rwkv_chunks_scores(items):
    shape = (CHUNK, LANES)
    row = lax.broadcasted_iota(jnp.int32, shape, 0)
    col = lax.broadcasted_iota(jnp.int32, shape, 1) % RWKV_HEAD
    strict = row > col
    incl = row >= col
    eye = row == col
    blocks = lambda x: _pair_blocks(x).astype(BF16)

    for it in items:
        lhs = jnp.concatenate([it["at"], it["rt"]], axis=0).astype(BF16)
        rhs = jnp.concatenate([blocks(it["bt"]), blocks(it["kt"])], axis=0)
        sc = _dot_nt(lhs, rhs)
        it["s_ab"] = jnp.where(strict, sc[:CHUNK, :LANES], 0.0)
        it["s_ak"] = jnp.where(strict, sc[:CHUNK, LANES:], 0.0).astype(BF16)
        it["s_r"] = jnp.concatenate([jnp.where(incl, sc[CHUNK:, :LANES], 0.0),
                                     jnp.where(incl, sc[CHUNK:, LANES:], 0.0)], axis=1).astype(BF16)
        it["bd_v"] = blocks(it["v"])
        it["bk_t"] = jnp.concatenate([_pair_transpose(it["bh"]), _pair_transpose(it["kh"])],
                                     axis=1).astype(BF16)

    return strict, incl, eye, blocks


def _rwkv_chunks_rest(items, h_ref, eye, blocks, filler):
    shape = (CHUNK, LANES)
    for it in items:
        it["npow"] = _dot(it["s_ab"].astype(BF16), blocks(it["s_ab"]))
        it["inv"] = jnp.where(eye, 1.0, 0.0) + it["s_ab"]
    filler()
    for step in range(4):
        for it in items:
            both = _dot(jnp.concatenate([it["inv"], it["npow"]], axis=0).astype(BF16),
                        blocks(it["npow"]))
            it["inv"] = it["inv"] + both[:CHUNK]
            it["npow"] = both[CHUNK:]
        filler()
    for it in items:
        it["inv"] = it["inv"] + _dot(it["inv"].astype(BF16), blocks(it["npow"]))
    filler()

    for it in items:
        it["av"] = _dot(it["s_ak"], it["bd_v"])
    for it in items:
        wu = _dot(it["inv"].astype(BF16),
                  jnp.concatenate([blocks(it["at"]), blocks(it["av"])], axis=1))
        it["z"] = jnp.concatenate(
            [jnp.concatenate([blocks(wu[:, :LANES]), blocks(wu[:, LANES:])], axis=1),
             jnp.concatenate([jnp.zeros((LANES, LANES), BF16), it["bd_v"]], axis=1)], axis=0)
    filler()
    for it in items:
        both = _dot(jnp.concatenate([it["s_r"], it["bk_t"]], axis=0), it["z"])
        qy, mg = both[:CHUNK], both[CHUNK:]
        m = jnp.where(eye, jnp.broadcast_to(it["decay"], shape), 0.0) + mg[:, :LANES]
        it["qm"] = jnp.concatenate([it["rt"] + qy[:, :LANES], m], axis=0).astype(BF16)
        it["yp"] = qy[:, LANES:]
        it["g"] = mg[:, LANES:]
    filler()

    for it in items:
        both = _dot(it["qm"], blocks(h_ref[it["pair"]]))
        it["y"] = both[:CHUNK] + it["yp"]
        h_ref[it["pair"]] = both[CHUNK:] + it["g"]


OPERANDS = ("at", "rt", "bt", "kt", "bh", "kh", "v")


def _rwkv_kernel(r_ref, k2_ref, v_ref, av_ref, bv_ref, logw_ref, gate_ref, bonus_ref, hn_ref,
                 win_ref, lng_ref, lnb_ref, ones_ref, ltri_ref, *refs, n_cast):
    cast_in, (o_ref, zg_ref) = refs[:n_cast], refs[n_cast:n_cast + 2]
    cast_out = refs[n_cast + 2:2 * n_cast + 2]
    h_ref, decay_ref, *op_refs = refs[2 * n_cast + 2:]
    for src, dst in zip(cast_in, cast_out):
        dst[...] = src[...].astype(BF16)
    tt, d = r_ref.shape
    n_pairs = d // LANES
    n_chunks = tt // CHUNK
    ops = dict(zip(OPERANDS, op_refs))

    @pl.when(pl.program_id(1) == 0)
    def _():
        for ref in (h_ref, decay_ref) + tuple(op_refs):
            ref[...] = jnp.zeros_like(ref)

    items = []
    for c in range(n_chunks):
        rows = pl.ds(c * CHUNK, CHUNK)
        for j in range(n_pairs):
            ln = pl.ds(j * LANES, LANES)
            item = {name: ref[rows, ln] for name, ref in ops.items()}
            item["decay"] = decay_ref[c, :, ln]
            item["pair"] = j
            items.append(item)
    _, _, eye, blocks = _rwkv_chunks_scores(items)

    for c in range(n_chunks):
        rows = pl.ds(c * CHUNK, CHUNK)
        load = lambda ref: ref[rows, :].astype(F32)
        r, k2, v, av, bv, logw = map(load, (r_ref, k2_ref, v_ref, av_ref, bv_ref, logw_ref))
        l_hi, l_lo = _split2(logw)
        lc = _dot(ltri_ref[...], l_hi) + _dot(ltri_ref[...], l_lo)
        lc_end = lc[CHUNK - 1:CHUNK]
        e_in = jnp.exp(lc - logw)
        e_inc = jnp.exp(lc)
        e_neg = jnp.exp(-lc)
        e_out = jnp.exp(lc_end - lc)
        new = dict(at=av * e_in, rt=r * e_inc, bt=bv * e_neg, kt=k2 * e_neg, bh=bv * e_out,
                   kh=k2 * e_out, v=v)
        for name, val in new.items():
            ops[name][rows, :] = val
        decay_ref[c] = jnp.exp(lc_end)

    n_fill = 8
    piece = zg_ref.shape[1] // n_fill
    gate_col0 = win_ref.shape[1] - zg_ref.shape[1]
    pending = list(range(n_fill))

    def filler():
        if pending:
            q = pending.pop(0)
            zg_ref[:, pl.ds(q * piece, piece)] = _dot(
                hn_ref[...], win_ref[:, pl.ds(gate_col0 + q * piece, piece)]).astype(BF16)

    _rwkv_chunks_rest(items, h_ref, eye, blocks, filler)
    while pending:
        filler()
    ones = ones_ref[...]
    inv_n = 1.0 / RWKV_HEAD
    for c in range(n_chunks):
        rows = pl.ds(c * CHUNK, CHUNK)
        y = jnp.concatenate([it["y"] for it in items[c * n_pairs:(c + 1) * n_pairs]], axis=1)
        mean = _head_sum(y, ones) * inv_n
        yc = y - mean
        var = _head_sum(yc * yc, ones) * inv_n
        yn = yc * lax.rsqrt(var + GN_EPS) * lng_ref[...] + lnb_ref[...]
        o_ref[rows, :] = (yn + bonus_ref[rows, :]) * gate_ref[rows, :]


def _rwkv_branch(mix, w_in, n_gate, bsz, lnx_g, lnx_b, later_weights):
    *step_in, gate, bonus, hn = mix
    t, d = gate.shape
    seq = t // bsz
    tt = min(TT_RWKV, seq)
    nt = seq // tt
    n_steps = bsz * (nt + 1)
    parts = max(p for p in range(1, n_steps + 1)
                if all(w.shape[0] % (p * BF16_TILE_ROWS) == 0 for w in later_weights))
    cast_block = lambda b, i: (jnp.minimum(b * (nt + 1) + i, parts - 1), 0)
    cast_specs = [pl.BlockSpec((w.shape[0] // parts, w.shape[1]), cast_block) for w in later_weights]
    ones = _head_ones()
    ltri = jnp.asarray(np.tril(np.ones((CHUNK, CHUNK), bool)), BF16)
    this_tile = lambda b, i: (b * nt + jnp.minimum(i, nt - 1), 0)
    prev_tile = lambda b, i: (b * nt + jnp.maximum(i - 1, 0), 0)
    c2 = lambda b, i: (0, 0)
    consts = [lnx_g.reshape(1, d), lnx_b.reshape(1, d), ones, ltri]
    return pl.pallas_call(
        functools.partial(_rwkv_kernel, n_cast=len(later_weights)),
        grid=(bsz, nt + 1),
        in_specs=[pl.BlockSpec((tt, d), this_tile) for _ in step_in]
        + [pl.BlockSpec((tt, d), prev_tile)] * 2 + [pl.BlockSpec((tt, hn.shape[1]), prev_tile)]
        + [pl.BlockSpec(w_in.shape, c2, pipeline_mode=pl.Buffered(1))]
        + [pl.BlockSpec(c.shape, c2) for c in consts] + cast_specs,
        out_specs=[pl.BlockSpec((tt, d), prev_tile), pl.BlockSpec((tt, n_gate), prev_tile)]
        + cast_specs,
        out_shape=[jax.ShapeDtypeStruct((t, d), F32), jax.ShapeDtypeStruct((t, n_gate), BF16)]
        + [jax.ShapeDtypeStruct(w.shape, BF16) for w in later_weights],
        scratch_shapes=[pltpu.VMEM((d // LANES, RWKV_HEAD, LANES), F32),
                        pltpu.VMEM((tt // CHUNK, 1, d), F32)]
        + [pltpu.VMEM((tt, d), F32)] * len(OPERANDS),
        compiler_params=pltpu.CompilerParams(dimension_semantics=("arbitrary", "arbitrary"),
                                             vmem_limit_bytes=VMEM_LIMIT),
        name="rwkv",
    )(*step_in, gate, bonus, hn, w_in, *consts, *later_weights)


def _tail_kernel(x_ref, oa_ref, ob_ref, zg_ref, wb_ref, wout_ref, g2_ref,
                 w1_ref, w2_ref, gf_ref, o_ref, *, ff_block):
    x = x_ref[...]
    d = x.shape[1]
    d_mix = oa_ref.shape[1]
    gates = jax.nn.sigmoid(zg_ref[...].astype(F32))
    ma = _dot(oa_ref[...].astype(BF16), wb_ref[:d_mix, :])
    mb = _dot(ob_ref[...].astype(BF16), wb_ref[d_mix:, :])
    merged = gates[:, :d] * ma + gates[:, d:] * mb
    x1 = x + _dot(merged.astype(BF16), wout_ref[...])
    h2 = _rmsnorm(x1, g2_ref[...]).astype(BF16)
    acc = x1
    for j in range(w1_ref.shape[1] // ff_block):
        cols = slice(j * ff_block, (j + 1) * ff_block)
        hid = jnp.square(jnp.maximum(_dot(h2, w1_ref[:, cols]), 0.0))
        acc = acc + _dot(hid.astype(BF16), w2_ref[cols, :])
    o_ref[...] = _rmsnorm(acc, gf_ref[...])


def _tail(x2, o_a, o_b, zg, w_branch, w_out, norm2_g, w_ff1, w_ff2, norm_f_g):
    t, d = x2.shape
    d_mix = o_a.shape[1]
    tm = min(TM_PROJ, t)
    tok = lambda i: (i, 0)
    const = lambda i: (0, 0)
    resident = lambda a: pl.BlockSpec(a.shape, const, pipeline_mode=pl.Buffered(1))
    vec = lambda a: a.reshape(1, d)
    return pl.pallas_call(
        functools.partial(_tail_kernel, ff_block=d),
        grid=(t // tm,),
        in_specs=[pl.BlockSpec((tm, d), tok), pl.BlockSpec((tm, d_mix), tok),
                  pl.BlockSpec((tm, d_mix), tok), pl.BlockSpec((tm, zg.shape[1]), tok),
                  resident(w_branch), resident(w_out),
                  pl.BlockSpec((1, d), const), resident(w_ff1), resident(w_ff2),
                  pl.BlockSpec((1, d), const)],
        out_specs=pl.BlockSpec((tm, d), tok),
        out_shape=jax.ShapeDtypeStruct((t, d), F32),
        compiler_params=pltpu.CompilerParams(dimension_semantics=("arbitrary",),
                                             vmem_limit_bytes=VMEM_LIMIT),
        name="tail",
    )(x2, o_a, o_b, zg, w_branch, w_out, vec(norm2_g), w_ff1, w_ff2, vec(norm_f_g))


def _layer(x2, bsz, norm1_g, w_in, lam_re, lam_im, log_dt, b_re, b_im, c_re, c_im, d_skip, w_glu,
           b_glu, mu_rwkv, w0, w2, a0, a2, g2, k_k, k_a, r_k, lnx_g, lnx_b, w_branch, w_out,
           norm2_g, w_ff1, w_ff2, out_g):
    d_ssm = d_skip.shape[0]
    w_in = w_in.astype(BF16)
    o_a, *mix = _mix_in(x2, x2.shape[0] // bsz, norm1_g, w_in, d_ssm, mu_rwkv, w0, w2, a0, a2, g2,
                        k_k, k_a, r_k.reshape(-1),
                        (lam_re, lam_im, log_dt, b_re, b_im, c_re, c_im, d_skip, w_glu, b_glu))
    o_b, zg, *tail_weights = _rwkv_branch(mix, w_in, w_in.shape[1] - d_ssm - mu_rwkv.shape[0], bsz,
                                          lnx_g, lnx_b, (w_branch, w_out, w_ff1, w_ff2))
    w_branch, w_out, w_ff1, w_ff2 = tail_weights
    return _tail(x2, o_a, o_b, zg, w_branch, w_out, norm2_g, w_ff1, w_ff2, out_g)


def kernel(x, norm1_g, w_in, lam_re, lam_im, log_dt, b_re, b_im, c_re, c_im, d_skip, w_glu, b_glu,
           mu_rwkv, w0, w2, a0, a2, g2, k_k, k_a, r_k, lnx_g, lnx_b, w_branch, w_out, norm2_g,
           w_ff1, w_ff2, norm_f_g):
    bsz, seq, d = x.shape
    depth = norm1_g.shape[0]
    assert depth == 1, "the fused tail applies the final norm, so exactly one layer is supported"
    x2 = x.reshape(bsz * seq, d)
    l = 0
    out = _layer(x2, bsz, norm1_g[l], w_in[l], lam_re[l], lam_im[l], log_dt[l], b_re[l], b_im[l],
                 c_re[l], c_im[l], d_skip[l], w_glu[l], b_glu[l], mu_rwkv[l], w0[l], w2[l], a0[l],
                 a2[l], g2[l], k_k[l], k_a[l], r_k[l], lnx_g[l], lnx_b[l], w_branch[l], w_out[l],
                 norm2_g[l], w_ff1[l], w_ff2[l], norm_f_g)
    return out.reshape(bsz, seq, d)
```

```python
import functools

import jax
import jax.numpy as jnp
import numpy as np
from jax import lax
from jax.experimental import pallas as pl
from jax.experimental.pallas import tpu as pltpu

F32 = jnp.float32
BF16 = jnp.bfloat16

RMS_EPS = 1e-6
GN_EPS = 64e-5
L2_EPS = 1e-12
MIN_NEG_REAL = -1e-4
EXP_NEG_HALF = 0.6065306597126334

SSM_GROUP = 16
SSM_STATE = 64
RWKV_HEAD = 64
GATE_LORA = 128
DECAY_LORA = 64
AAA_LORA = 64
CHUNK = 64

SUBLANES = 8
LANES = 128
BF16_TILE_ROWS = 2 * SUBLANES
VMEM_LIMIT = 56 * 1024 * 1024

TM_PROJ = 512
TT_S5 = 256
TT_RWKV = 256
S5_LANE_CHUNK = 512


def _dot(a, b):
    return jnp.dot(a, b, preferred_element_type=F32)


def _dot_nt(a, b):
    return lax.dot_general(a, b, (((1,), (1,)), ((), ())), preferred_element_type=F32)


def _split2(x):
    hi = x.astype(BF16)
    lo = (x - hi.astype(F32)).astype(BF16)
    return hi, lo


def _split3(x):
    hi = x.astype(BF16)
    r1 = x - hi.astype(F32)
    lo = r1.astype(BF16)
    lo2 = (r1 - lo.astype(F32)).astype(BF16)
    return hi, lo, lo2


def _rmsnorm(x, g):
    ms = jnp.mean(x * x, axis=-1, keepdims=True)
    return x * lax.rsqrt(ms + RMS_EPS) * g


def _cmul(ar, ai, br, bi):
    return ar * br - ai * bi, ar * bi + ai * br


def _s5_prep_kernel(rows_ref, grid_ref, a1_re_ref, a1_im_ref, a8_re_ref, a8_im_ref, apb_re_ref,
                    apb_im_ref, alp_re_ref, alp_im_ref, shl_re_ref, shl_im_ref, q_ref, *, seg_len):
    def discretise(lre, lim, ldt):
        lr = jnp.minimum(lre, MIN_NEG_REAL)
        dt = jnp.exp(ldt)
        mag = jnp.exp(lr * dt)
        ab_re = mag * jnp.cos(lim * dt)
        ab_im = mag * jnp.sin(lim * dt)
        den = lr * lr + lim * lim
        xm1 = ab_re - 1.0
        q_re = (xm1 * lr + ab_im * lim) / den
        q_im = (ab_im * lr - xm1 * lim) / den
        return ab_re, ab_im, q_re, q_im

    shape = a1_re_ref.shape
    a_re, a_im, _, _ = discretise(rows_ref[0], rows_ref[1], rows_ref[2])
    bcast = lambda x: jnp.broadcast_to(x, shape)
    pows = [(a_re, a_im)]
    for _ in range(SUBLANES - 1):
        pows.append(_cmul(*pows[-1], a_re, a_im))
    a1_re_ref[...] = bcast(a_re)
    a1_im_ref[...] = bcast(a_im)
    a8_re_ref[...] = bcast(pows[-1][0])
    a8_im_ref[...] = bcast(pows[-1][1])
    shape16 = (2 * SUBLANES,) + shape[1:]
    first = lax.broadcasted_iota(jnp.int32, shape16, 0) < SUBLANES
    for q in range(SUBLANES // 2):
        (r0, i0), (r1, i1) = pows[2 * q], pows[2 * q + 1]
        bc16 = lambda x: jnp.broadcast_to(x, shape16)
        apb_re_ref[q] = jnp.where(first, bc16(r0), bc16(r1)).astype(BF16)
        apb_im_ref[q] = jnp.where(first, bc16(i0), bc16(i1)).astype(BF16)
    lr_, li_ = pows[-1]
    n = SUBLANES
    while n < seg_len:
        lr_, li_ = _cmul(lr_, li_, lr_, li_)
        n *= 2
    lpows = [(lr_, li_)]
    for _ in range(SUBLANES - 1):
        lpows.append(_cmul(*lpows[-1], lr_, li_))
    row = lax.broadcasted_iota(jnp.int32, shape, 0)
    alp_re = jnp.zeros(shape, F32)
    alp_im = jnp.zeros(shape, F32)
    for i, (pr, pi) in enumerate(lpows):
        alp_re = jnp.where(row == i, bcast(pr), alp_re)
        alp_im = jnp.where(row == i, bcast(pi), alp_im)
    alp_re_ref[...] = alp_re
    alp_im_ref[...] = alp_im
    for i, s in enumerate((1, 2, 4)):
        pr, pi = lpows[s - 1]
        shl_re_ref[i] = jnp.where(row >= s, bcast(pr), 0.0)
        shl_im_ref[i] = jnp.where(row >= s, bcast(pi), 0.0)

    _, _, q_re, q_im = discretise(grid_ref[0], grid_ref[1], grid_ref[2])
    q_ref[0] = q_re
    q_ref[1] = q_im


def _s5_prep(lam_re, lam_im, log_dt, seg_len):
    g, p = lam_re.shape
    n = g * p
    lam3 = jnp.stack([lam_re, lam_im, jnp.broadcast_to(log_dt[:, None], (g, p))])
    s8 = jax.ShapeDtypeStruct((SUBLANES, n), F32)
    s88 = jax.ShapeDtypeStruct((SUBLANES // 2, 2 * SUBLANES, n), BF16)
    s38 = jax.ShapeDtypeStruct((3, SUBLANES, n), F32)
    sq = jax.ShapeDtypeStruct((2, g, p), F32)
    out_shape = (s8, s8, s8, s8, s88, s88, s8, s8, s38, s38, sq)
    return pl.pallas_call(functools.partial(_s5_prep_kernel, seg_len=seg_len),
                          out_shape=out_shape, name="s5_prep")(lam3.reshape(3, 1, n), lam3)


def _head_ones():
    head = np.arange(LANES) // RWKV_HEAD
    return jnp.asarray(head[:, None] == head[None, :], BF16)


def _head_sum(x, ones):
    rows, width = x.shape
    groups = width // LANES
    stacked = jnp.concatenate([x[:, j * LANES:(j + 1) * LANES] for j in range(groups)], axis=0)
    sums = _dot(stacked.astype(BF16), ones)
    return jnp.concatenate([sums[j * rows:(j + 1) * rows] for j in range(groups)], axis=1)


def _mix_in_kernel(*refs, d_ssm, d, tiles_per_seq, tt_s5):
    (x_ref, g_ref, w_ref, mu_ref, wwa_ref, g2_ref, w0_ref, a0_ref, kk_ref, ka_ref, rk_ref,
     ones_ref) = refs[:12]
    s5 = dict(zip(S5_CONSTS, refs[12:12 + len(S5_CONSTS)]))
    (oa_ref, r_ref, k2_ref, v_ref, av_ref, bv_ref, logw_ref, gate_ref, bonus_ref, h_ref,
     prev_ref, carry_ref, *state_refs) = refs[12 + len(S5_CONSTS):]
    xs_refs, xb_refs = state_refs[:len(state_refs) // 2], state_refs[len(state_refs) // 2:]
    tm = x_ref.shape[0]

    @pl.when(pl.program_id(0) % tiles_per_seq == 0)
    def _():
        prev_ref[...] = jnp.zeros_like(prev_ref)
        carry_ref[...] = jnp.zeros_like(carry_ref)

    h = _rmsnorm(x_ref[...], g_ref[...]).astype(BF16)
    h_ref[...] = h

    def mixed(c0, c1):
        p = _dot(h, w_ref[:, d_ssm + c0:d_ssm + c1])
        shape8 = (SUBLANES, c1 - c0)
        row = lax.broadcasted_iota(jnp.int32, shape8, 0)
        last = prev_ref[SUBLANES - 1:SUBLANES, c0:c1]
        prev = pltpu.roll(p, 1, 0)
        head = jnp.where(row == 0, jnp.broadcast_to(last, shape8), prev[:SUBLANES])
        prev = jnp.concatenate([head, prev[SUBLANES:]], axis=0)
        prev_ref[:, c0:c1] = p[tm - SUBLANES:, :]
        return p + (prev - p) * mu_ref[:, c0:c1]

    sub = lambda i: slice(i * tt_s5, (i + 1) * tt_s5)
    n_sub = tm // tt_s5
    us = [_dot(h[sub(i)], w_ref[:, :d_ssm]) for i in range(n_sub)]
    for i in range(n_sub):
        _s5_project(us[i], s5, xs_refs[i])
    _s5_scan(s5, xs_refs[0], xb_refs[0], carry_ref)
    lora_in = mixed(3 * d, 3 * d + GATE_LORA + DECAY_LORA + AAA_LORA)
    k = mixed(d, 2 * d)
    xg = lora_in[:, :GATE_LORA]
    xwa = lora_in[:, GATE_LORA:]
    lane = lax.broadcasted_iota(jnp.int32, xwa.shape, 1)
    zwa = jnp.where(lane < DECAY_LORA, jnp.tanh(xwa), xwa)
    lora = _dot(zwa.astype(BF16), wwa_ref[...])
    gate_ref[...] = _dot(jax.nn.sigmoid(xg).astype(BF16), g2_ref[...])
    logw_ref[...] = -EXP_NEG_HALF * jax.nn.sigmoid(w0_ref[...] + lora[:, :d])
    a = jax.nn.sigmoid(a0_ref[...] + lora[:, d:])
    for i in range(1, n_sub):
        _s5_scan(s5, xs_refs[i], xb_refs[i], carry_ref)
    oa_ref[sub(0), :] = _s5_readout(us[0], s5, xb_refs[0])

    ones = ones_ref[...]
    r = mixed(0, d)
    kk = k * kk_ref[...]
    kk = kk * lax.rsqrt(jnp.maximum(_head_sum(kk * kk, ones), L2_EPS * L2_EPS))
    v = mixed(2 * d, 3 * d)
    k2 = k * (1.0 + (a - 1.0) * ka_ref[...])
    k2_ref[...] = k2.astype(BF16)
    av_ref[...] = (-kk).astype(BF16)
    bv_ref[...] = (kk * a).astype(BF16)
    r_ref[...] = r.astype(BF16)
    v_ref[...] = v.astype(BF16)
    bonus_ref[...] = _head_sum(r * k2 * rk_ref[...], ones) * v
    for i in range(1, n_sub):
        oa_ref[sub(i), :] = _s5_readout(us[i], s5, xb_refs[i])


def _mix_in(x2, seq, norm_g, w_in, d_ssm, mu, w0, w2, a0, a2, g2, k_k, k_a, r_k, s5_params):
    t, dm = x2.shape
    d = w0.shape[0]
    tm = min(TM_PROJ, seq)
    tt_s5 = min(TT_S5, tm)
    s5_consts = _s5_constants(tt_s5, *s5_params)
    n_state2 = s5_consts[2].shape[2]
    wwa = jnp.concatenate([jnp.concatenate([w2, jnp.zeros_like(w2)], axis=1),
                           jnp.concatenate([jnp.zeros_like(a2), a2], axis=1)], axis=0)
    ones = _head_ones()
    vec = lambda a: a.reshape(1, -1)
    consts = [vec(norm_g), w_in, vec(mu), wwa.astype(BF16), g2.astype(BF16), vec(w0), vec(a0),
              vec(k_k), vec(k_a), vec(r_k), ones, *s5_consts]
    whole = lambda a: pl.BlockSpec(a.shape, lambda i: (0,) * a.ndim)
    tok = lambda i: (i, 0)
    widths = [(d_ssm, F32)] + [(d, BF16)] * 5 + [(d, F32)] * 3 + [(dm, BF16)]
    return pl.pallas_call(
        functools.partial(_mix_in_kernel, d_ssm=d_ssm, d=d, tiles_per_seq=seq // tm, tt_s5=tt_s5),
        grid=(t // tm,),
        in_specs=[pl.BlockSpec((tm, dm), tok)] + [whole(c) for c in consts],
        out_specs=[pl.BlockSpec((tm, w), tok) for w, _ in widths],
        out_shape=[jax.ShapeDtypeStruct((t, w), dt) for w, dt in widths],
        scratch_shapes=[pltpu.VMEM((SUBLANES, mu.shape[0]), F32),
                        pltpu.VMEM((SUBLANES, 2 * n_state2), F32)]
        + [pltpu.VMEM((tt_s5, 2 * n_state2), F32)] * (tm // tt_s5)
        + [pltpu.VMEM((tt_s5, 2 * n_state2), BF16)] * (tm // tt_s5),
        compiler_params=pltpu.CompilerParams(dimension_semantics=("arbitrary",),
                                             vmem_limit_bytes=VMEM_LIMIT),
        name="mix_in",
    )(x2, *consts)


S5_CONSTS = ("perm", "permt", "bmat", "cmat", "a1_re", "a1_im", "a8_re", "a8_im", "apb_re", "apb_im",
             "alp_re", "alp_im", "shl_re", "shl_im", "d_skip", "w_glu", "b_glu")


def _s5_project(u, s5, xs_ref):
    n_state = xs_ref.shape[1] // 2
    half = n_state // 2
    lw = S5_LANE_CHUNK
    ub = _dot(s5["perm"][...], u.astype(BF16)).astype(BF16)
    d_half = ub.shape[1] // 2
    for c in range(n_state // lw):
        h, cc = divmod(c, half // lw)
        for part in range(2):
            xs_ref[:, pl.ds(part * n_state + c * lw, lw)] = _dot(
                ub[:, h * d_half:(h + 1) * d_half],
                s5["bmat"][h, :, pl.ds(part * half + cc * lw, lw)])


def _s5_scan(s5, xs_ref, xb_ref, carry_ref):
    tt = xs_ref.shape[0]
    n_state = xs_ref.shape[1] // 2
    seg_len = tt // SUBLANES
    lw = S5_LANE_CHUNK
    blk = lambda j: pl.ds(j * SUBLANES, SUBLANES)
    for c in range(n_state // lw):
        re_sl, im_sl = pl.ds(c * lw, lw), pl.ds(n_state + c * lw, lw)
        ar, ai = s5["a1_re"][:, re_sl], s5["a1_im"][:, re_sl]
        xr = jnp.zeros((SUBLANES, lw), F32)
        xi = jnp.zeros((SUBLANES, lw), F32)
        for j in range(seg_len):
            br = xs_ref[blk(j), re_sl]
            bi = xs_ref[blk(j), im_sl]
            xr, xi = ar * xr - ai * xi + br, ar * xi + ai * xr + bi
            xs_ref[blk(j), re_sl] = xr
            xs_ref[blk(j), im_sl] = xi
        er, ei = xr, xi
        for k, sh in enumerate((1, 2, 4)):
            pr, pi = s5["shl_re"][k, :, re_sl], s5["shl_im"][k, :, re_sl]
            sr, si = pltpu.roll(er, sh, 0), pltpu.roll(ei, sh, 0)
            er, ei = er + (pr * sr - pi * si), ei + (pr * si + pi * sr)
        tr, ti = carry_ref[:, re_sl], carry_ref[:, im_sl]
        pr, pi = s5["alp_re"][:, re_sl], s5["alp_im"][:, re_sl]
        fr = er + (pr * tr - pi * ti)
        fi = ei + (pr * ti + pi * tr)
        row = lax.broadcasted_iota(jnp.int32, fr.shape, 0)
        dr = jnp.where(row == 0, tr, pltpu.roll(fr, 1, 0))
        di = jnp.where(row == 0, ti, pltpu.roll(fi, 1, 0))
        last = SUBLANES - 1
        carry_ref[:, re_sl] = jnp.broadcast_to(fr[last:last + 1], fr.shape)
        carry_ref[:, im_sl] = jnp.broadcast_to(fi[last:last + 1], fi.shape)
        a8r, a8i = s5["a8_re"][:, re_sl], s5["a8_im"][:, re_sl]
        both = lambda x: jnp.concatenate([x, x], axis=0).astype(BF16)
        for a in range(seg_len // SUBLANES):
            dr2, di2 = both(dr), both(di)
            for q in range(SUBLANES // 2):
                rows = pl.ds((a * SUBLANES + 2 * q) * SUBLANES, 2 * SUBLANES)
                pr, pi = s5["apb_re"][q, :, re_sl], s5["apb_im"][q, :, re_sl]
                xb_ref[rows, re_sl] = xs_ref[rows, re_sl].astype(BF16) + (pr * dr2 - pi * di2)
                xb_ref[rows, im_sl] = xs_ref[rows, im_sl].astype(BF16) + (pr * di2 + pi * dr2)
            dr, di = a8r * dr - a8i * di, a8r * di + a8i * dr


def _s5_readout(u, s5, xb_ref):
    n_state = xb_ref.shape[1] // 2
    half = n_state // 2
    lw = S5_LANE_CHUNK
    n_chunks = n_state // lw
    width = u.shape[1] // n_chunks
    blocks = []
    for c in range(n_chunks):
        h, cc = divmod(c, half // lw)
        cols = pl.ds(cc * width, width)
        blocks.append(sum(_dot(xb_ref[:, pl.ds(part * n_state + c * lw, lw)],
                               s5["cmat"][h, pl.ds(part * half + cc * lw, lw), cols])
                          for part in range(2)))
    cx_hi, cx_lo = _split2(jnp.concatenate(blocks, axis=1))
    permt = s5["permt"][...]
    y = _dot(permt, cx_hi) + _dot(permt, cx_lo) + s5["d_skip"][...] * u
    z = jax.nn.gelu(y)
    gate = jax.nn.sigmoid(_dot(z.astype(BF16), s5["w_glu"][...]) + s5["b_glu"][...])
    return z * gate


def _s5_constants(tt, lam_re, lam_im, log_dt, b_re, b_im, c_re, c_im, d_skip, w_glu, b_glu):
    d_ssm = d_skip.shape[0]
    n_grp, n_state, grp = b_re.shape
    seg_len = tt // SUBLANES
    *scan_consts, q = _s5_prep(lam_re, lam_im, log_dt, seg_len)
    q_re, q_im = q[0][:, :, None], q[1][:, :, None]
    bb = jnp.stack([q_re * b_re - q_im * b_im, q_re * b_im + q_im * b_re])
    cc = jnp.stack([c_re, -c_im])
    hg = n_grp // 2
    hc, hs2 = hg * grp, 2 * hg * n_state
    bb = bb.reshape(2, 2, hg, n_state, grp).transpose(1, 2, 4, 0, 3).reshape(2, hc, 2 * n_state)
    cc = cc.reshape(2, 2, hg, grp, n_state).transpose(1, 0, 4, 2, 3).reshape(2, 2 * n_state, hc)
    col = np.arange(hs2)
    repeat = (col[None, :] // (hg * n_state) == np.arange(2 * n_state)[:, None] // n_state) \
        & (col[None, :] % n_state == np.arange(2 * n_state)[:, None] % n_state)
    same = np.arange(hc)[:, None] // grp == (col[None, :] // n_state) % hg
    repeat = jnp.asarray(repeat, BF16)
    bmat = jnp.einsum("hck,kn->hcn", bb.astype(BF16), repeat, preferred_element_type=BF16)
    bmat = jnp.where(same, bmat, 0.0)
    cmat = jnp.einsum("nk,hkc->hnc", repeat.T, cc.astype(BF16), preferred_element_type=BF16)
    cmat = jnp.where(same.T, cmat, 0.0)
    rho = np.arange(tt)
    src = (rho % SUBLANES) * seg_len + rho // SUBLANES
    perm = src[:, None] == np.arange(tt)[None, :]
    return [jnp.asarray(perm, BF16), jnp.asarray(perm.T, BF16), bmat, cmat, *scan_consts,
            d_skip.reshape(1, d_ssm), w_glu.astype(BF16), b_glu.reshape(1, d_ssm)]


def _pair_blocks(x):
    lane = lax.broadcasted_iota(jnp.int32, x.shape, 1)
    first = lane < RWKV_HEAD
    return jnp.concatenate([jnp.where(first, x, 0.0), jnp.where(first, 0.0, x)], axis=0)


def _pair_transpose(x):
    t = _pair_blocks(x).T
    return t[:CHUNK] + t[CHUNK:]


def _rwkv_chunks_scores(items):
    shape = (CHUNK, LANES)
    row = lax.broadcasted_iota(jnp.int32, shape, 0)
    col = lax.broadcasted_iota(jnp.int32, shape, 1) % RWKV_HEAD
    strict = row > col
    incl = row >= col
    eye = row == col
    blocks = lambda x: _pair_blocks(x).astype(BF16)

    for it in items:
        lhs = jnp.concatenate([it["at"], it["rt"]], axis=0).astype(BF16)
        rhs = jnp.concatenate([blocks(it["bt"]), blocks(it["kt"])], axis=0)
        sc = _dot_nt(lhs, rhs)
        it["s_ab"] = jnp.where(strict, sc[:CHUNK, :LANES], 0.0)
        it["s_ak"] = jnp.where(strict, sc[:CHUNK, LANES:], 0.0).astype(BF16)
        it["s_r"] = jnp.concatenate([jnp.where(incl, sc[CHUNK:, :LANES], 0.0),
                                     jnp.where(incl, sc[CHUNK:, LANES:], 0.0)], axis=1).astype(BF16)
        it["bd_v"] = blocks(it["v"])
        it["bk_t"] = jnp.concatenate([_pair_transpose(it["bh"]), _pair_transpose(it["kh"])],
                                     axis=1).astype(BF16)

    return strict, incl, eye, blocks


def _rwkv_chunks_rest(items, h_ref, eye, blocks, filler):
    shape = (CHUNK, LANES)
    for it in items:
        it["npow"] = _dot(it["s_ab"].astype(BF16), blocks(it["s_ab"]))
        it["inv"] = jnp.where(eye, 1.0, 0.0) + it["s_ab"]
    filler()
    for step in range(4):
        for it in items:
            both = _dot(jnp.concatenate([it["inv"], it["npow"]], axis=0).astype(BF16),
                        blocks(it["npow"]))
            it["inv"] = it["inv"] + both[:CHUNK]
            it["npow"] = both[CHUNK:]
        filler()
    for it in items:
        it["inv"] = it["inv"] + _dot(it["inv"].astype(BF16), blocks(it["npow"]))
    filler()

    for it in items:
        it["av"] = _dot(it["s_ak"], it["bd_v"])
    for it in items:
        wu = _dot(it["inv"].astype(BF16),
                  jnp.concatenate([blocks(it["at"]), blocks(it["av"])], axis=1))
        it["z"] = jnp.concatenate(
            [jnp.concatenate([blocks(wu[:, :LANES]), blocks(wu[:, LANES:])], axis=1),
             jnp.concatenate([jnp.zeros((LANES, LANES), BF16), it["bd_v"]], axis=1)], axis=0)
    filler()
    for it in items:
        both = _dot(jnp.concatenate([it["s_r"], it["bk_t"]], axis=0), it["z"])
        qy, mg = both[:CHUNK], both[CHUNK:]
        m = jnp.where(eye, jnp.broadcast_to(it["decay"], shape), 0.0) + mg[:, :LANES]
        it["qm"] = jnp.concatenate([it["rt"] + qy[:, :LANES], m], axis=0).astype(BF16)
        it["yp"] = qy[:, LANES:]
        it["g"] = mg[:, LANES:]
    filler()

    for it in items:
        both = _dot(it["qm"], blocks(h_ref[it["pair"]]))
        it["y"] = both[:CHUNK] + it["yp"]
        h_ref[it["pair"]] = both[CHUNK:] + it["g"]


OPERANDS = ("at", "rt", "bt", "kt", "bh", "kh", "v")


def _rwkv_kernel(r_ref, k2_ref, v_ref, av_ref, bv_ref, logw_ref, gate_ref, bonus_ref, hn_ref,
                 win_ref, lng_ref, lnb_ref, ones_ref, ltri_ref, *refs, n_cast):
    cast_in, (o_ref, zg_ref) = refs[:n_cast], refs[n_cast:n_cast + 2]
    cast_out = refs[n_cast + 2:2 * n_cast + 2]
    h_ref, decay_ref, *op_refs = refs[2 * n_cast + 2:]
    for src, dst in zip(cast_in, cast_out):
        dst[...] = src[...].astype(BF16)
    tt, d = r_ref.shape
    n_pairs = d // LANES
    n_chunks = tt // CHUNK
    ops = dict(zip(OPERANDS, op_refs))

    @pl.when(pl.program_id(1) == 0)
    def _():
        for ref in (h_ref, decay_ref) + tuple(op_refs):
            ref[...] = jnp.zeros_like(ref)

    items = []
    for c in range(n_chunks):
        rows = pl.ds(c * CHUNK, CHUNK)
        for j in range(n_pairs):
            ln = pl.ds(j * LANES, LANES)
            item = {name: ref[rows, ln] for name, ref in ops.items()}
            item["decay"] = decay_ref[c, :, ln]
            item["pair"] = j
            items.append(item)
    _, _, eye, blocks = _rwkv_chunks_scores(items)

    for c in range(n_chunks):
        rows = pl.ds(c * CHUNK, CHUNK)
        load = lambda ref: ref[rows, :].astype(F32)
        r, k2, v, av, bv, logw = map(load, (r_ref, k2_ref, v_ref, av_ref, bv_ref, logw_ref))
        l_hi, l_lo = _split2(logw)
        lc = _dot(ltri_ref[...], l_hi) + _dot(ltri_ref[...], l_lo)
        lc_end = lc[CHUNK - 1:CHUNK]
        e_in = jnp.exp(lc - logw)
        e_inc = jnp.exp(lc)
        e_neg = jnp.exp(-lc)
        e_out = jnp.exp(lc_end - lc)
        new = dict(at=av * e_in, rt=r * e_inc, bt=bv * e_neg, kt=k2 * e_neg, bh=bv * e_out,
                   kh=k2 * e_out, v=v)
        for name, val in new.items():
            ops[name][rows, :] = val
        decay_ref[c] = jnp.exp(lc_end)

    n_fill = 8
    piece = zg_ref.shape[1] // n_fill
    gate_col0 = win_ref.shape[1] - zg_ref.shape[1]
    pending = list(range(n_fill))

    def filler():
        if pending:
            q = pending.pop(0)
            zg_ref[:, pl.ds(q * piece, piece)] = _dot(
                hn_ref[...], win_ref[:, pl.ds(gate_col0 + q * piece, piece)]).astype(BF16)

    _rwkv_chunks_rest(items, h_ref, eye, blocks, filler)
    while pending:
        filler()
    ones = ones_ref[...]
    inv_n = 1.0 / RWKV_HEAD
    for c in range(n_chunks):
        rows = pl.ds(c * CHUNK, CHUNK)
        y = jnp.concatenate([it["y"] for it in items[c * n_pairs:(c + 1) * n_pairs]], axis=1)
        mean = _head_sum(y, ones) * inv_n
        yc = y - mean
        var = _head_sum(yc * yc, ones) * inv_n
        yn = yc * lax.rsqrt(var + GN_EPS) * lng_ref[...] + lnb_ref[...]
        o_ref[rows, :] = (yn + bonus_ref[rows, :]) * gate_ref[rows, :]


def _rwkv_branch(mix, w_in, n_gate, bsz, lnx_g, lnx_b, later_weights):
    *step_in, gate, bonus, hn = mix
    t, d = gate.shape
    seq = t // bsz
    tt = min(TT_RWKV, seq)
    nt = seq // tt
    n_steps = bsz * (nt + 1)
    parts = max(p for p in range(1, n_steps + 1)
                if all(w.shape[0] % (p * BF16_TILE_ROWS) == 0 for w in later_weights))
    cast_block = lambda b, i: (jnp.minimum(b * (nt + 1) + i, parts - 1), 0)
    cast_specs = [pl.BlockSpec((w.shape[0] // parts, w.shape[1]), cast_block) for w in later_weights]
    ones = _head_ones()
    ltri = jnp.asarray(np.tril(np.ones((CHUNK, CHUNK), bool)), BF16)
    this_tile = lambda b, i: (b * nt + jnp.minimum(i, nt - 1), 0)
    prev_tile = lambda b, i: (b * nt + jnp.maximum(i - 1, 0), 0)
    c2 = lambda b, i: (0, 0)
    consts = [lnx_g.reshape(1, d), lnx_b.reshape(1, d), ones, ltri]
    return pl.pallas_call(
        functools.partial(_rwkv_kernel, n_cast=len(later_weights)),
        grid=(bsz, nt + 1),
        in_specs=[pl.BlockSpec((tt, d), this_tile) for _ in step_in]
        + [pl.BlockSpec((tt, d), prev_tile)] * 2 + [pl.BlockSpec((tt, hn.shape[1]), prev_tile)]
        + [pl.BlockSpec(w_in.shape, c2, pipeline_mode=pl.Buffered(1))]
        + [pl.BlockSpec(c.shape, c2) for c in consts] + cast_specs,
        out_specs=[pl.BlockSpec((tt, d), prev_tile), pl.BlockSpec((tt, n_gate), prev_tile)]
        + cast_specs,
        out_shape=[jax.ShapeDtypeStruct((t, d), F32), jax.ShapeDtypeStruct((t, n_gate), BF16)]
        + [jax.ShapeDtypeStruct(w.shape, BF16) for w in later_weights],
        scratch_shapes=[pltpu.VMEM((d // LANES, RWKV_HEAD, LANES), F32),
                        pltpu.VMEM((tt // CHUNK, 1, d), F32)]
        + [pltpu.VMEM((tt, d), F32)] * len(OPERANDS),
        compiler_params=pltpu.CompilerParams(dimension_semantics=("arbitrary", "arbitrary"),
                                             vmem_limit_bytes=VMEM_LIMIT),
        name="rwkv",
    )(*step_in, gate, bonus, hn, w_in, *consts, *later_weights)


def _tail_kernel(x_ref, oa_ref, ob_ref, zg_ref, wb_ref, wout_ref, g2_ref,
                 w1_ref, w2_ref, gf_ref, o_ref, *, ff_block):
    x = x_ref[...]
    d = x.shape[1]
    d_mix = oa_ref.shape[1]
    gates = jax.nn.sigmoid(zg_ref[...].astype(F32))
    ma = _dot(oa_ref[...].astype(BF16), wb_ref[:d_mix, :])
    mb = _dot(ob_ref[...].astype(BF16), wb_ref[d_mix:, :])
    merged = gates[:, :d] * ma + gates[:, d:] * mb
    x1 = x + _dot(merged.astype(BF16), wout_ref[...])
    h2 = _rmsnorm(x1, g2_ref[...]).astype(BF16)
    acc = x1
    for j in range(w1_ref.shape[1] // ff_block):
        cols = slice(j * ff_block, (j + 1) * ff_block)
        hid = jnp.square(jnp.maximum(_dot(h2, w1_ref[:, cols]), 0.0))
        acc = acc + _dot(hid.astype(BF16), w2_ref[cols, :])
    o_ref[...] = _rmsnorm(acc, gf_ref[...])


def _tail(x2, o_a, o_b, zg, w_branch, w_out, norm2_g, w_ff1, w_ff2, norm_f_g):
    t, d = x2.shape
    d_mix = o_a.shape[1]
    tm = min(TM_PROJ, t)
    tok = lambda i: (i, 0)
    const = lambda i: (0, 0)
    resident = lambda a: pl.BlockSpec(a.shape, const, pipeline_mode=pl.Buffered(1))
    vec = lambda a: a.reshape(1, d)
    return pl.pallas_call(
        functools.partial(_tail_kernel, ff_block=d),
        grid=(t // tm,),
        in_specs=[pl.BlockSpec((tm, d), tok), pl.BlockSpec((tm, d_mix), tok),
                  pl.BlockSpec((tm, d_mix), tok), pl.BlockSpec((tm, zg.shape[1]), tok),
                  resident(w_branch), resident(w_out),
                  pl.BlockSpec((1, d), const), resident(w_ff1), resident(w_ff2),
                  pl.BlockSpec((1, d), const)],
        out_specs=pl.BlockSpec((tm, d), tok),
        out_shape=jax.ShapeDtypeStruct((t, d), F32),
        compiler_params=pltpu.CompilerParams(dimension_semantics=("arbitrary",),
                                             vmem_limit_bytes=VMEM_LIMIT),
        name="tail",
    )(x2, o_a, o_b, zg, w_branch, w_out, vec(norm2_g), w_ff1, w_ff2, vec(norm_f_g))


def _layer(x2, bsz, norm1_g, w_in, lam_re, lam_im, log_dt, b_re, b_im, c_re, c_im, d_skip, w_glu,
           b_glu, mu_rwkv, w0, w2, a0, a2, g2, k_k, k_a, r_k, lnx_g, lnx_b, w_branch, w_out,
           norm2_g, w_ff1, w_ff2, out_g):
    d_ssm = d_skip.shape[0]
    w_in = w_in.astype(BF16)
    o_a, *mix = _mix_in(x2, x2.shape[0] // bsz, norm1_g, w_in, d_ssm, mu_rwkv, w0, w2, a0, a2, g2,
                        k_k, k_a, r_k.reshape(-1),
                        (lam_re, lam_im, log_dt, b_re, b_im, c_re, c_im, d_skip, w_glu, b_glu))
    o_b, zg, *tail_weights = _rwkv_branch(mix, w_in, w_in.shape[1] - d_ssm - mu_rwkv.shape[0], bsz,
                                          lnx_g, lnx_b, (w_branch, w_out, w_ff1, w_ff2))
    w_branch, w_out, w_ff1, w_ff2 = tail_weights
    return _tail(x2, o_a, o_b, zg, w_branch, w_out, norm2_g, w_ff1, w_ff2, out_g)


def kernel(x, norm1_g, w_in, lam_re, lam_im, log_dt, b_re, b_im, c_re, c_im, d_skip, w_glu, b_glu,
           mu_rwkv, w0, w2, a0, a2, g2, k_k, k_a, r_k, lnx_g, lnx_b, w_branch, w_out, norm2_g,
           w_ff1, w_ff2, norm_f_g):
    bsz, seq, d = x.shape
    depth = norm1_g.shape[0]
    assert depth == 1, "the fused tail applies the final norm, so exactly one layer is supported"
    x2 = x.reshape(bsz * seq, d)
    l = 0
    out = _layer(x2, bsz, norm1_g[l], w_in[l], lam_re[l], lam_im[l], log_dt[l], b_re[l], b_im[l],
                 c_re[l], c_im[l], d_skip[l], w_glu[l], b_glu[l], mu_rwkv[l], w0[l], w2[l], a0[l],
                 a2[l], g2[l], k_k[l], k_a[l], r_k[l], lnx_g[l], lnx_b[l], w_branch[l], w_out[l],
                 norm2_g[l], w_ff1[l], w_ff2[l], norm_f_g)
    return out.reshape(bsz, seq, d)
```

```python
import functools

import jax
import jax.numpy as jnp
import numpy as np
from jax import lax
from jax.experimental import pallas as pl
from jax.experimental.pallas import tpu as pltpu

F32 = jnp.float32
BF16 = jnp.bfloat16

RMS_EPS = 1e-6
GN_EPS = 64e-5
L2_EPS = 1e-12
MIN_NEG_REAL = -1e-4
EXP_NEG_HALF = 0.6065306597126334

SSM_GROUP = 16
SSM_STATE = 64
RWKV_HEAD = 64
GATE_LORA = 128
DECAY_LORA = 64
AAA_LORA = 64
CHUNK = 64

SUBLANES = 8
LANES = 128
BF16_TILE_ROWS = 2 * SUBLANES
VMEM_LIMIT = 56 * 1024 * 1024

TM_PROJ = 512
TT_S5 = 256
TT_RWKV = 256
S5_LANE_CHUNK = 512


def _dot(a, b):
    return jnp.dot(a, b, preferred_element_type=F32)


def _dot_nt(a, b):
    return lax.dot_general(a, b, (((1,), (1,)), ((), ())), preferred_element_type=F32)


def _split2(x):
    hi = x.astype(BF16)
    lo = (x - hi.astype(F32)).astype(BF16)
    return hi, lo


def _split3(x):
    hi = x.astype(BF16)
    r1 = x - hi.astype(F32)
    lo = r1.astype(BF16)
    lo2 = (r1 - lo.astype(F32)).astype(BF16)
    return hi, lo, lo2


def _rmsnorm(x, g):
    ms = jnp.mean(x * x, axis=-1, keepdims=True)
    return x * lax.rsqrt(ms + RMS_EPS) * g


def _cmul(ar, ai, br, bi):
    return ar * br - ai * bi, ar * bi + ai * br


def _s5_prep_kernel(rows_ref, grid_ref, a1_re_ref, a1_im_ref, a8_re_ref, a8_im_ref, apb_re_ref,
                    apb_im_ref, alp_re_ref, alp_im_ref, shl_re_ref, shl_im_ref, q_ref, *, seg_len):
    def discretise(lre, lim, ldt):
        lr = jnp.minimum(lre, MIN_NEG_REAL)
        dt = jnp.exp(ldt)
        mag = jnp.exp(lr * dt)
        ab_re = mag * jnp.cos(lim * dt)
        ab_im = mag * jnp.sin(lim * dt)
        den = lr * lr + lim * lim
        xm1 = ab_re - 1.0
        q_re = (xm1 * lr + ab_im * lim) / den
        q_im = (ab_im * lr - xm1 * lim) / den
        return ab_re, ab_im, q_re, q_im

    shape = a1_re_ref.shape
    a_re, a_im, _, _ = discretise(rows_ref[0], rows_ref[1], rows_ref[2])
    bcast = lambda x: jnp.broadcast_to(x, shape)
    pows = [(a_re, a_im)]
    for _ in range(SUBLANES - 1):
        pows.append(_cmul(*pows[-1], a_re, a_im))
    a1_re_ref[...] = bcast(a_re)
    a1_im_ref[...] = bcast(a_im)
    a8_re_ref[...] = bcast(pows[-1][0])
    a8_im_ref[...] = bcast(pows[-1][1])
    shape16 = (2 * SUBLANES,) + shape[1:]
    first = lax.broadcasted_iota(jnp.int32, shape16, 0) < SUBLANES
    for q in range(SUBLANES // 2):
        (r0, i0), (r1, i1) = pows[2 * q], pows[2 * q + 1]
        bc16 = lambda x: jnp.broadcast_to(x, shape16)
        apb_re_ref[q] = jnp.where(first, bc16(r0), bc16(r1)).astype(BF16)
        apb_im_ref[q] = jnp.where(first, bc16(i0), bc16(i1)).astype(BF16)
    lr_, li_ = pows[-1]
    n = SUBLANES
    while n < seg_len:
        lr_, li_ = _cmul(lr_, li_, lr_, li_)
        n *= 2
    lpows = [(lr_, li_)]
    for _ in range(SUBLANES - 1):
        lpows.append(_cmul(*lpows[-1], lr_, li_))
    row = lax.broadcasted_iota(jnp.int32, shape, 0)
    alp_re = jnp.zeros(shape, F32)
    alp_im = jnp.zeros(shape, F32)
    for i, (pr, pi) in enumerate(lpows):
        alp_re = jnp.where(row == i, bcast(pr), alp_re)
        alp_im = jnp.where(row == i, bcast(pi), alp_im)
    alp_re_ref[...] = alp_re
    alp_im_ref[...] = alp_im
    for i, s in enumerate((1, 2, 4)):
        pr, pi = lpows[s - 1]
        shl_re_ref[i] = jnp.where(row >= s, bcast(pr), 0.0)
        shl_im_ref[i] = jnp.where(row >= s, bcast(pi), 0.0)

    _, _, q_re, q_im = discretise(grid_ref[0], grid_ref[1], grid_ref[2])
    q_ref[0] = q_re
    q_ref[1] = q_im


def _s5_prep(lam_re, lam_im, log_dt, seg_len):
    g, p = lam_re.shape
    n = g * p
    lam3 = jnp.stack([lam_re, lam_im, jnp.broadcast_to(log_dt[:, None], (g, p))])
    s8 = jax.ShapeDtypeStruct((SUBLANES, n), F32)
    s88 = jax.ShapeDtypeStruct((SUBLANES // 2, 2 * SUBLANES, n), BF16)
    s38 = jax.ShapeDtypeStruct((3, SUBLANES, n), F32)
    sq = jax.ShapeDtypeStruct((2, g, p), F32)
    out_shape = (s8, s8, s8, s8, s88, s88, s8, s8, s38, s38, sq)
    return pl.pallas_call(functools.partial(_s5_prep_kernel, seg_len=seg_len),
                          out_shape=out_shape, name="s5_prep")(lam3.reshape(3, 1, n), lam3)


def _head_ones():
    head = np.arange(LANES) // RWKV_HEAD
    return jnp.asarray(head[:, None] == head[None, :], BF16)


def _head_sum(x, ones):
    rows, width = x.shape
    groups = width // LANES
    stacked = jnp.concatenate([x[:, j * LANES:(j + 1) * LANES] for j in range(groups)], axis=0)
    sums = _dot(stacked.astype(BF16), ones)
    return jnp.concatenate([sums[j * rows:(j + 1) * rows] for j in range(groups)], axis=1)


def _mix_in_kernel(*refs, d_ssm, d, tiles_per_seq, tt_s5):
    (x_ref, g_ref, w_ref, mu_ref, wwa_ref, g2_ref, w0_ref, a0_ref, kk_ref, ka_ref, rk_ref,
     ones_ref) = refs[:12]
    s5 = dict(zip(S5_CONSTS, refs[12:12 + len(S5_CONSTS)]))
    (oa_ref, r_ref, k2_ref, v_ref, av_ref, bv_ref, logw_ref, gate_ref, bonus_ref, h_ref,
     prev_ref, carry_ref, *state_refs) = refs[12 + len(S5_CONSTS):]
    xs_refs, xb_refs = state_refs[:len(state_refs) // 2], state_refs[len(state_refs) // 2:]
    tm = x_ref.shape[0]

    @pl.when(pl.program_id(0) % tiles_per_seq == 0)
    def _():
        prev_ref[...] = jnp.zeros_like(prev_ref)
        carry_ref[...] = jnp.zeros_like(carry_ref)

    h = _rmsnorm(x_ref[...], g_ref[...]).astype(BF16)
    h_ref[...] = h

    def mixed(c0, c1):
        p = _dot(h, w_ref[:, d_ssm + c0:d_ssm + c1])
        shape8 = (SUBLANES, c1 - c0)
        row = lax.broadcasted_iota(jnp.int32, shape8, 0)
        last = prev_ref[SUBLANES - 1:SUBLANES, c0:c1]
        prev = pltpu.roll(p, 1, 0)
        head = jnp.where(row == 0, jnp.broadcast_to(last, shape8), prev[:SUBLANES])
        prev = jnp.concatenate([head, prev[SUBLANES:]], axis=0)
        prev_ref[:, c0:c1] = p[tm - SUBLANES:, :]
        return p + (prev - p) * mu_ref[:, c0:c1]

    sub = lambda i: slice(i * tt_s5, (i + 1) * tt_s5)
    n_sub = tm // tt_s5
    us = [_dot(h[sub(i)], w_ref[:, :d_ssm]) for i in range(n_sub)]
    for i in range(n_sub):
        _s5_project(us[i], s5, xs_refs[i])
    _s5_scan(s5, xs_refs[0], xb_refs[0], carry_ref)
    lora_in = mixed(3 * d, 3 * d + GATE_LORA + DECAY_LORA + AAA_LORA)
    k = mixed(d, 2 * d)
    xg = lora_in[:, :GATE_LORA]
    xwa = lora_in[:, GATE_LORA:]
    lane = lax.broadcasted_iota(jnp.int32, xwa.shape, 1)
    zwa = jnp.where(lane < DECAY_LORA, jnp.tanh(xwa), xwa)
    lora = _dot(zwa.astype(BF16), wwa_ref[...])
    gate_ref[...] = _dot(jax.nn.sigmoid(xg).astype(BF16), g2_ref[...])
    logw_ref[...] = -EXP_NEG_HALF * jax.nn.sigmoid(w0_ref[...] + lora[:, :d])
    a = jax.nn.sigmoid(a0_ref[...] + lora[:, d:])
    for i in range(1, n_sub):
        _s5_scan(s5, xs_refs[i], xb_refs[i], carry_ref)
    oa_ref[sub(0), :] = _s5_readout(us[0], s5, xb_refs[0])

    ones = ones_ref[...]
    r = mixed(0, d)
    kk = k * kk_ref[...]
    kk = kk * lax.rsqrt(jnp.maximum(_head_sum(kk * kk, ones), L2_EPS * L2_EPS))
    v = mixed(2 * d, 3 * d)
    k2 = k * (1.0 + (a - 1.0) * ka_ref[...])
    k2_ref[...] = k2.astype(BF16)
    av_ref[...] = (-kk).astype(BF16)
    bv_ref[...] = (kk * a).astype(BF16)
    r_ref[...] = r.astype(BF16)
    v_ref[...] = v.astype(BF16)
    bonus_ref[...] = _head_sum(r * k2 * rk_ref[...], ones) * v
    for i in range(1, n_sub):
        oa_ref[sub(i), :] = _s5_readout(us[i], s5, xb_refs[i])


def _mix_in(x2, seq, norm_g, w_in, d_ssm, mu, w0, w2, a0, a2, g2, k_k, k_a, r_k, s5_params):
    t, dm = x2.shape
    d = w0.shape[0]
    tm = min(TM_PROJ, seq)
    tt_s5 = min(TT_S5, tm)
    s5_consts = _s5_constants(tt_s5, *s5_params)
    n_state2 = s5_consts[2].shape[2]
    wwa = jnp.concatenate([jnp.concatenate([w2, jnp.zeros_like(w2)], axis=1),
                           jnp.concatenate([jnp.zeros_like(a2), a2], axis=1)], axis=0)
    ones = _head_ones()
    vec = lambda a: a.reshape(1, -1)
    consts = [vec(norm_g), w_in, vec(mu), wwa.astype(BF16), g2.astype(BF16), vec(w0), vec(a0),
              vec(k_k), vec(k_a), vec(r_k), ones, *s5_consts]
    whole = lambda a: pl.BlockSpec(a.shape, lambda i: (0,) * a.ndim)
    tok = lambda i: (i, 0)
    widths = [(d_ssm, F32)] + [(d, BF16)] * 5 + [(d, F32)] * 3 + [(dm, BF16)]
    return pl.pallas_call(
        functools.partial(_mix_in_kernel, d_ssm=d_ssm, d=d, tiles_per_seq=seq // tm, tt_s5=tt_s5),
        grid=(t // tm,),
        in_specs=[pl.BlockSpec((tm, dm), tok)] + [whole(c) for c in consts],
        out_specs=[pl.BlockSpec((tm, w), tok) for w, _ in widths],
        out_shape=[jax.ShapeDtypeStruct((t, w), dt) for w, dt in widths],
        scratch_shapes=[pltpu.VMEM((SUBLANES, mu.shape[0]), F32),
                        pltpu.VMEM((SUBLANES, 2 * n_state2), F32)]
        + [pltpu.VMEM((tt_s5, 2 * n_state2), F32)] * (tm // tt_s5)
        + [pltpu.VMEM((tt_s5, 2 * n_state2), BF16)] * (tm // tt_s5),
        compiler_params=pltpu.CompilerParams(dimension_semantics=("arbitrary",),
                                             vmem_limit_bytes=VMEM_LIMIT),
        name="mix_in",
    )(x2, *consts)


S5_CONSTS = ("perm", "permt", "bmat", "cmat", "a1_re", "a1_im", "a8_re", "a8_im", "apb_re", "apb_im",
             "alp_re", "alp_im", "shl_re", "shl_im", "d_skip", "w_glu", "b_glu")


def _s5_project(u, s5, xs_ref):
    n_state = xs_ref.shape[1] // 2
    half = n_state // 2
    lw = S5_LANE_CHUNK
    ub = _dot(s5["perm"][...], u.astype(BF16)).astype(BF16)
    d_half = ub.shape[1] // 2
    for c in range(n_state // lw):
        h, cc = divmod(c, half // lw)
        for part in range(2):
            xs_ref[:, pl.ds(part * n_state + c * lw, lw)] = _dot(
                ub[:, h * d_half:(h + 1) * d_half],
                s5["bmat"][h, :, pl.ds(part * half + cc * lw, lw)])


def _s5_scan(s5, xs_ref, xb_ref, carry_ref):
    tt = xs_ref.shape[0]
    n_state = xs_ref.shape[1] // 2
    seg_len = tt // SUBLANES
    lw = S5_LANE_CHUNK
    blk = lambda j: pl.ds(j * SUBLANES, SUBLANES)
    for c in range(n_state // lw):
        re_sl, im_sl = pl.ds(c * lw, lw), pl.ds(n_state + c * lw, lw)
        ar, ai = s5["a1_re"][:, re_sl], s5["a1_im"][:, re_sl]
        xr = jnp.zeros((SUBLANES, lw), F32)
        xi = jnp.zeros((SUBLANES, lw), F32)
        for j in range(seg_len):
            br = xs_ref[blk(j), re_sl]
            bi = xs_ref[blk(j), im_sl]
            xr, xi = ar * xr - ai * xi + br, ar * xi + ai * xr + bi
            xs_ref[blk(j), re_sl] = xr
            xs_ref[blk(j), im_sl] = xi
        er, ei = xr, xi
        for k, sh in enumerate((1, 2, 4)):
            pr, pi = s5["shl_re"][k, :, re_sl], s5["shl_im"][k, :, re_sl]
            sr, si = pltpu.roll(er, sh, 0), pltpu.roll(ei, sh, 0)
            er, ei = er + (pr * sr - pi * si), ei + (pr * si + pi * sr)
        tr, ti = carry_ref[:, re_sl], carry_ref[:, im_sl]
        pr, pi = s5["alp_re"][:, re_sl], s5["alp_im"][:, re_sl]
        fr = er + (pr * tr - pi * ti)
        fi = ei + (pr * ti + pi * tr)
        row = lax.broadcasted_iota(jnp.int32, fr.shape, 0)
        dr = jnp.where(row == 0, tr, pltpu.roll(fr, 1, 0))
        di = jnp.where(row == 0, ti, pltpu.roll(fi, 1, 0))
        last = SUBLANES - 1
        carry_ref[:, re_sl] = jnp.broadcast_to(fr[last:last + 1], fr.shape)
        carry_ref[:, im_sl] = jnp.broadcast_to(fi[last:last + 1], fi.shape)
        a8r, a8i = s5["a8_re"][:, re_sl], s5["a8_im"][:, re_sl]
        both = lambda x: jnp.concatenate([x, x], axis=0).astype(BF16)
        for a in range(seg_len // SUBLANES):
            dr2, di2 = both(dr), both(di)
            for q in range(SUBLANES // 2):
                rows = pl.ds((a * SUBLANES + 2 * q) * SUBLANES, 2 * SUBLANES)
                pr, pi = s5["apb_re"][q, :, re_sl], s5["apb_im"][q, :, re_sl]
                xb_ref[rows, re_sl] = xs_ref[rows, re_sl].astype(BF16) + (pr * dr2 - pi * di2)
                xb_ref[rows, im_sl] = xs_ref[rows, im_sl].astype(BF16) + (pr * di2 + pi * dr2)
            dr, di = a8r * dr - a8i * di, a8r * di + a8i * dr


def _s5_readout(u, s5, xb_ref):
    n_state = xb_ref.shape[1] // 2
    half = n_state // 2
    lw = S5_LANE_CHUNK
    n_chunks = n_state // lw
    width = u.shape[1] // n_chunks
    blocks = []
    for c in range(n_chunks):
        h, cc = divmod(c, half // lw)
        cols = pl.ds(cc * width, width)
        blocks.append(sum(_dot(xb_ref[:, pl.ds(part * n_state + c * lw, lw)],
                               s5["cmat"][h, pl.ds(part * half + cc * lw, lw), cols])
                          for part in range(2)))
    cx_hi, cx_lo = _split2(jnp.concatenate(blocks, axis=1))
    permt = s5["permt"][...]
    y = _dot(permt, cx_hi) + _dot(permt, cx_lo) + s5["d_skip"][...] * u
    z = jax.nn.gelu(y)
    gate = jax.nn.sigmoid(_dot(z.astype(BF16), s5["w_glu"][...]) + s5["b_glu"][...])
    return z * gate


def _s5_constants(tt, lam_re, lam_im, log_dt, b_re, b_im, c_re, c_im, d_skip, w_glu, b_glu):
    d_ssm = d_skip.shape[0]
    n_grp, n_state, grp = b_re.shape
    seg_len = tt // SUBLANES
    *scan_consts, q = _s5_prep(lam_re, lam_im, log_dt, seg_len)
    q_re, q_im = q[0][:, :, None], q[1][:, :, None]
    bb = jnp.stack([q_re * b_re - q_im * b_im, q_re * b_im + q_im * b_re])
    cc = jnp.stack([c_re, -c_im])
    hg = n_grp // 2
    hc, hs2 = hg * grp, 2 * hg * n_state
    bb = bb.reshape(2, 2, hg, n_state, grp).transpose(1, 2, 4, 0, 3).reshape(2, hc, 2 * n_state)
    cc = cc.reshape(2, 2, hg, grp, n_state).transpose(1, 0, 4, 2, 3).reshape(2, 2 * n_state, hc)
    col = np.arange(hs2)
    repeat = (col[None, :] // (hg * n_state) == np.arange(2 * n_state)[:, None] // n_state) \
        & (col[None, :] % n_state == np.arange(2 * n_state)[:, None] % n_state)
    same = np.arange(hc)[:, None] // grp == (col[None, :] // n_state) % hg
    repeat = jnp.asarray(repeat, BF16)
    bmat = jnp.einsum("hck,kn->hcn", bb.astype(BF16), repeat, preferred_element_type=BF16)
    bmat = jnp.where(same, bmat, 0.0)
    cmat = jnp.einsum("nk,hkc->hnc", repeat.T, cc.astype(BF16), preferred_element_type=BF16)
    cmat = jnp.where(same.T, cmat, 0.0)
    rho = np.arange(tt)
    src = (rho % SUBLANES) * seg_len + rho // SUBLANES
    perm = src[:, None] == np.arange(tt)[None, :]
    return [jnp.asarray(perm, BF16), jnp.asarray(perm.T, BF16), bmat, cmat, *scan_consts,
            d_skip.reshape(1, d_ssm), w_glu.astype(BF16), b_glu.reshape(1, d_ssm)]


def _pair_blocks(x):
    lane = lax.broadcasted_iota(jnp.int32, x.shape, 1)
    first = lane < RWKV_HEAD
    return jnp.concatenate([jnp.where(first, x, 0.0), jnp.where(first, 0.0, x)], axis=0)


def _pair_transpose(x):
    t = _pair_blocks(x).T
    return t[:CHUNK] + t[CHUNK:]


def _rwkv_chunks_scores(items):
    shape = (CHUNK, LANES)
    row = lax.broadcasted_iota(jnp.int32, shape, 0)
    col = lax.broadcasted_iota(jnp.int32, shape, 1) % RWKV_HEAD
    strict = row > col
    incl = row >= col
    eye = row == col
    blocks = lambda x: _pair_blocks(x).astype(BF16)

    for it in items:
        lhs = jnp.concatenate([it["at"], it["rt"]], axis=0).astype(BF16)
        rhs = jnp.concatenate([blocks(it["bt"]), blocks(it["kt"])], axis=0)
        sc = _dot_nt(lhs, rhs)
        it["s_ab"] = jnp.where(strict, sc[:CHUNK, :LANES], 0.0)
        it["s_ak"] = jnp.where(strict, sc[:CHUNK, LANES:], 0.0).astype(BF16)
        it["s_r"] = jnp.concatenate([jnp.where(incl, sc[CHUNK:, :LANES], 0.0),
                                     jnp.where(incl, sc[CHUNK:, LANES:], 0.0)], axis=1).astype(BF16)
        it["bd_v"] = blocks(it["v"])
        it["bk_t"] = jnp.concatenate([_pair_transpose(it["bh"]), _pair_transpose(it["kh"])],
                                     axis=1).astype(BF16)

    return strict, incl, eye, blocks


def _rwkv_chunks_rest(items, h_ref, eye, blocks, filler):
    shape = (CHUNK, LANES)
    for it in items:
        it["npow"] = _dot(it["s_ab"].astype(BF16), blocks(it["s_ab"]))
        it["inv"] = jnp.where(eye, 1.0, 0.0) + it["s_ab"]
    filler()
    for step in range(4):
        for it in items:
            both = _dot(jnp.concatenate([it["inv"], it["npow"]], axis=0).astype(BF16),
                        blocks(it["npow"]))
            it["inv"] = it["inv"] + both[:CHUNK]
            it["npow"] = both[CHUNK:]
        filler()
    for it in items:
        it["inv"] = it["inv"] + _dot(it["inv"].astype(BF16), blocks(it["npow"]))
    filler()

    for it in items:
        it["av"] = _dot(it["s_ak"], it["bd_v"])
    for it in items:
        wu = _dot(it["inv"].astype(BF16),
                  jnp.concatenate([blocks(it["at"]), blocks(it["av"])], axis=1))
        it["z"] = jnp.concatenate(
            [jnp.concatenate([blocks(wu[:, :LANES]), blocks(wu[:, LANES:])], axis=1),
             jnp.concatenate([jnp.zeros((LANES, LANES), BF16), it["bd_v"]], axis=1)], axis=0)
    filler()
    for it in items:
        both = _dot(jnp.concatenate([it["s_r"], it["bk_t"]], axis=0), it["z"])
        qy, mg = both[:CHUNK], both[CHUNK:]
        m = jnp.where(eye, jnp.broadcast_to(it["decay"], shape), 0.0) + mg[:, :LANES]
        it["qm"] = jnp.concatenate([it["rt"] + qy[:, :LANES], m], axis=0).astype(BF16)
        it["yp"] = qy[:, LANES:]
        it["g"] = mg[:, LANES:]
    filler()

    for it in items:
        both = _dot(it["qm"], blocks(h_ref[it["pair"]]))
        it["y"] = both[:CHUNK] + it["yp"]
        h_ref[it["pair"]] = both[CHUNK:] + it["g"]


OPERANDS = ("at", "rt", "bt", "kt", "bh", "kh", "v")


def _rwkv_kernel(r_ref, k2_ref, v_ref, av_ref, bv_ref, logw_ref, gate_ref, bonus_ref, hn_ref,
                 win_ref, lng_ref, lnb_ref, ones_ref, ltri_ref, *refs, n_cast, tiles_per_seq):
    cast_in, (o_ref, zg_ref) = refs[:n_cast], refs[n_cast:n_cast + 2]
    cast_out = refs[n_cast + 2:2 * n_cast + 2]
    h_ref, decay_ref, *op_refs = refs[2 * n_cast + 2:]
    for src, dst in zip(cast_in, cast_out):
        dst[...] = src[...].astype(BF16)
    tt, d = r_ref.shape
    n_pairs = d // LANES
    n_chunks = tt // CHUNK
    ops = dict(zip(OPERANDS, op_refs))

    step = pl.program_id(0)

    @pl.when(step == 0)
    def _():
        for ref in (decay_ref,) + tuple(op_refs):
            ref[...] = jnp.zeros_like(ref)

    @pl.when(lax.rem(jnp.maximum(step - 1, 0), tiles_per_seq) == 0)
    def _():
        h_ref[...] = jnp.zeros_like(h_ref)

    items = []
    for c in range(n_chunks):
        rows = pl.ds(c * CHUNK, CHUNK)
        for j in range(n_pairs):
            ln = pl.ds(j * LANES, LANES)
            item = {name: ref[rows, ln] for name, ref in ops.items()}
            item["decay"] = decay_ref[c, :, ln]
            item["pair"] = j
            items.append(item)
    _, _, eye, blocks = _rwkv_chunks_scores(items)

    for c in range(n_chunks):
        rows = pl.ds(c * CHUNK, CHUNK)
        load = lambda ref: ref[rows, :].astype(F32)
        r, k2, v, av, bv, logw = map(load, (r_ref, k2_ref, v_ref, av_ref, bv_ref, logw_ref))
        l_hi, l_lo = _split2(logw)
        lc = _dot(ltri_ref[...], l_hi) + _dot(ltri_ref[...], l_lo)
        lc_end = lc[CHUNK - 1:CHUNK]
        e_in = jnp.exp(lc - logw)
        e_inc = jnp.exp(lc)
        e_neg = jnp.exp(-lc)
        e_out = jnp.exp(lc_end - lc)
        new = dict(at=av * e_in, rt=r * e_inc, bt=bv * e_neg, kt=k2 * e_neg, bh=bv * e_out,
                   kh=k2 * e_out, v=v)
        for name, val in new.items():
            ops[name][rows, :] = val
        decay_ref[c] = jnp.exp(lc_end)

    n_fill = 8
    piece = zg_ref.shape[1] // n_fill
    gate_col0 = win_ref.shape[1] - zg_ref.shape[1]
    pending = list(range(n_fill))

    def filler():
        if pending:
            q = pending.pop(0)
            zg_ref[:, pl.ds(q * piece, piece)] = _dot(
                hn_ref[...], win_ref[:, pl.ds(gate_col0 + q * piece, piece)]).astype(BF16)

    _rwkv_chunks_rest(items, h_ref, eye, blocks, filler)
    while pending:
        filler()
    ones = ones_ref[...]
    inv_n = 1.0 / RWKV_HEAD
    for c in range(n_chunks):
        rows = pl.ds(c * CHUNK, CHUNK)
        y = jnp.concatenate([it["y"] for it in items[c * n_pairs:(c + 1) * n_pairs]], axis=1)
        mean = _head_sum(y, ones) * inv_n
        yc = y - mean
        var = _head_sum(yc * yc, ones) * inv_n
        yn = yc * lax.rsqrt(var + GN_EPS) * lng_ref[...] + lnb_ref[...]
        o_ref[rows, :] = (yn + bonus_ref[rows, :]) * gate_ref[rows, :]


def _rwkv_branch(mix, w_in, n_gate, bsz, lnx_g, lnx_b, later_weights):
    *step_in, gate, bonus, hn = mix
    t, d = gate.shape
    seq = t // bsz
    tt = min(TT_RWKV, seq)
    nt = seq // tt
    n_tiles = bsz * nt
    n_steps = n_tiles + 1
    parts = max(p for p in range(1, n_steps + 1)
                if all(w.shape[0] % (p * BF16_TILE_ROWS) == 0 for w in later_weights))
    cast_block = lambda i: (jnp.minimum(i, parts - 1), 0)
    cast_specs = [pl.BlockSpec((w.shape[0] // parts, w.shape[1]), cast_block) for w in later_weights]
    ones = _head_ones()
    ltri = jnp.asarray(np.tril(np.ones((CHUNK, CHUNK), bool)), BF16)
    this_tile = lambda i: (jnp.minimum(i, n_tiles - 1), 0)
    prev_tile = lambda i: (jnp.maximum(i - 1, 0), 0)
    c2 = lambda i: (0, 0)
    consts = [lnx_g.reshape(1, d), lnx_b.reshape(1, d), ones, ltri]
    return pl.pallas_call(
        functools.partial(_rwkv_kernel, n_cast=len(later_weights), tiles_per_seq=nt),
        grid=(n_steps,),
        in_specs=[pl.BlockSpec((tt, d), this_tile) for _ in step_in]
        + [pl.BlockSpec((tt, d), prev_tile)] * 2 + [pl.BlockSpec((tt, hn.shape[1]), prev_tile)]
        + [pl.BlockSpec(w_in.shape, c2, pipeline_mode=pl.Buffered(1))]
        + [pl.BlockSpec(c.shape, c2) for c in consts] + cast_specs,
        out_specs=[pl.BlockSpec((tt, d), prev_tile), pl.BlockSpec((tt, n_gate), prev_tile)]
        + cast_specs,
        out_shape=[jax.ShapeDtypeStruct((t, d), F32), jax.ShapeDtypeStruct((t, n_gate), BF16)]
        + [jax.ShapeDtypeStruct(w.shape, BF16) for w in later_weights],
        scratch_shapes=[pltpu.VMEM((d // LANES, RWKV_HEAD, LANES), F32),
                        pltpu.VMEM((tt // CHUNK, 1, d), F32)]
        + [pltpu.VMEM((tt, d), F32)] * len(OPERANDS),
        compiler_params=pltpu.CompilerParams(dimension_semantics=("arbitrary",),
                                             vmem_limit_bytes=VMEM_LIMIT),
        name="rwkv",
    )(*step_in, gate, bonus, hn, w_in, *consts, *later_weights)


def _tail_kernel(x_ref, oa_ref, ob_ref, zg_ref, wb_ref, wout_ref, g2_ref,
                 w1_ref, w2_ref, gf_ref, o_ref, *, ff_block):
    x = x_ref[...]
    d = x.shape[1]
    d_mix = oa_ref.shape[1]
    gates = jax.nn.sigmoid(zg_ref[...].astype(F32))
    ma = _dot(oa_ref[...].astype(BF16), wb_ref[:d_mix, :])
    mb = _dot(ob_ref[...].astype(BF16), wb_ref[d_mix:, :])
    merged = gates[:, :d] * ma + gates[:, d:] * mb
    x1 = x + _dot(merged.astype(BF16), wout_ref[...])
    h2 = _rmsnorm(x1, g2_ref[...]).astype(BF16)
    acc = x1
    for j in range(w1_ref.shape[1] // ff_block):
        cols = slice(j * ff_block, (j + 1) * ff_block)
        hid = jnp.square(jnp.maximum(_dot(h2, w1_ref[:, cols]), 0.0))
        acc = acc + _dot(hid.astype(BF16), w2_ref[cols, :])
    o_ref[...] = _rmsnorm(acc, gf_ref[...])


def _tail(x2, o_a, o_b, zg, w_branch, w_out, norm2_g, w_ff1, w_ff2, norm_f_g):
    t, d = x2.shape
    d_mix = o_a.shape[1]
    tm = min(TM_PROJ, t)
    tok = lambda i: (i, 0)
    const = lambda i: (0, 0)
    resident = lambda a: pl.BlockSpec(a.shape, const, pipeline_mode=pl.Buffered(1))
    vec = lambda a: a.reshape(1, d)
    return pl.pallas_call(
        functools.partial(_tail_kernel, ff_block=d),
        grid=(t // tm,),
        in_specs=[pl.BlockSpec((tm, d), tok), pl.BlockSpec((tm, d_mix), tok),
                  pl.BlockSpec((tm, d_mix), tok), pl.BlockSpec((tm, zg.shape[1]), tok),
                  resident(w_branch), resident(w_out),
                  pl.BlockSpec((1, d), const), resident(w_ff1), resident(w_ff2),
                  pl.BlockSpec((1, d), const)],
        out_specs=pl.BlockSpec((tm, d), tok),
        out_shape=jax.ShapeDtypeStruct((t, d), F32),
        compiler_params=pltpu.CompilerParams(dimension_semantics=("arbitrary",),
                                             vmem_limit_bytes=VMEM_LIMIT),
        name="tail",
    )(x2, o_a, o_b, zg, w_branch, w_out, vec(norm2_g), w_ff1, w_ff2, vec(norm_f_g))


def _layer(x2, bsz, norm1_g, w_in, lam_re, lam_im, log_dt, b_re, b_im, c_re, c_im, d_skip, w_glu,
           b_glu, mu_rwkv, w0, w2, a0, a2, g2, k_k, k_a, r_k, lnx_g, lnx_b, w_branch, w_out,
           norm2_g, w_ff1, w_ff2, out_g):
    d_ssm = d_skip.shape[0]
    w_in = w_in.astype(BF16)
    o_a, *mix = _mix_in(x2, x2.shape[0] // bsz, norm1_g, w_in, d_ssm, mu_rwkv, w0, w2, a0, a2, g2,
                        k_k, k_a, r_k.reshape(-1),
                        (lam_re, lam_im, log_dt, b_re, b_im, c_re, c_im, d_skip, w_glu, b_glu))
    o_b, zg, *tail_weights = _rwkv_branch(mix, w_in, w_in.shape[1] - d_ssm - mu_rwkv.shape[0], bsz,
                                          lnx_g, lnx_b, (w_branch, w_out, w_ff1, w_ff2))
    w_branch, w_out, w_ff1, w_ff2 = tail_weights
    return _tail(x2, o_a, o_b, zg, w_branch, w_out, norm2_g, w_ff1, w_ff2, out_g)


def kernel(x, norm1_g, w_in, lam_re, lam_im, log_dt, b_re, b_im, c_re, c_im, d_skip, w_glu, b_glu,
           mu_rwkv, w0, w2, a0, a2, g2, k_k, k_a, r_k, lnx_g, lnx_b, w_branch, w_out, norm2_g,
           w_ff1, w_ff2, norm_f_g):
    bsz, seq, d = x.shape
    depth = norm1_g.shape[0]
    assert depth == 1, "the fused tail applies the final norm, so exactly one layer is supported"
    x2 = x.reshape(bsz * seq, d)
    l = 0
    out = _layer(x2, bsz, norm1_g[l], w_in[l], lam_re[l], lam_im[l], log_dt[l], b_re[l], b_im[l],
                 c_re[l], c_im[l], d_skip[l], w_glu[l], b_glu[l], mu_rwkv[l], w0[l], w2[l], a0[l],
                 a2[l], g2[l], k_k[l], k_a[l], r_k[l], lnx_g[l], lnx_b[l], w_branch[l], w_out[l],
                 norm2_g[l], w_ff1[l], w_ff2[l], norm_f_g)
    return out.reshape(bsz, seq, d)
```

```python
import functools

import jax
import jax.numpy as jnp
import numpy as np
from jax import lax
from jax.experimental import pallas as pl
from jax.experimental.pallas import tpu as pltpu

F32 = jnp.float32
BF16 = jnp.bfloat16

RMS_EPS = 1e-6
GN_EPS = 64e-5
L2_EPS = 1e-12
MIN_NEG_REAL = -1e-4
EXP_NEG_HALF = 0.6065306597126334

SSM_GROUP = 16
SSM_STATE = 64
RWKV_HEAD = 64
GATE_LORA = 128
DECAY_LORA = 64
AAA_LORA = 64
CHUNK = 64

SUBLANES = 8
LANES = 128
BF16_TILE_ROWS = 2 * SUBLANES
VMEM_LIMIT = 56 * 1024 * 1024

TM_PROJ = 512
TT_S5 = 256
TT_RWKV = 256
S5_LANE_CHUNK = 512


def _dot(a, b):
    return jnp.dot(a, b, preferred_element_type=F32)


def _dot_nt(a, b):
    return lax.dot_general(a, b, (((1,), (1,)), ((), ())), preferred_element_type=F32)


def _split2(x):
    hi = x.astype(BF16)
    lo = (x - hi.astype(F32)).astype(BF16)
    return hi, lo


def _split3(x):
    hi = x.astype(BF16)
    r1 = x - hi.astype(F32)
    lo = r1.astype(BF16)
    lo2 = (r1 - lo.astype(F32)).astype(BF16)
    return hi, lo, lo2


def _rmsnorm(x, g):
    ms = jnp.mean(x * x, axis=-1, keepdims=True)
    return x * lax.rsqrt(ms + RMS_EPS) * g


def _cmul(ar, ai, br, bi):
    return ar * br - ai * bi, ar * bi + ai * br


def _s5_prep_kernel(rows_ref, grid_ref, a1_re_ref, a1_im_ref, a8_re_ref, a8_im_ref, apb_re_ref,
                    apb_im_ref, alp_re_ref, alp_im_ref, shl_re_ref, shl_im_ref, q_ref, *, seg_len):
    def discretise(lre, lim, ldt):
        lr = jnp.minimum(lre, MIN_NEG_REAL)
        dt = jnp.exp(ldt)
        mag = jnp.exp(lr * dt)
        ab_re = mag * jnp.cos(lim * dt)
        ab_im = mag * jnp.sin(lim * dt)
        den = lr * lr + lim * lim
        xm1 = ab_re - 1.0
        q_re = (xm1 * lr + ab_im * lim) / den
        q_im = (ab_im * lr - xm1 * lim) / den
        return ab_re, ab_im, q_re, q_im

    shape = a1_re_ref.shape
    a_re, a_im, _, _ = discretise(rows_ref[0], rows_ref[1], rows_ref[2])
    bcast = lambda x: jnp.broadcast_to(x, shape)
    pows = [(a_re, a_im)]
    for _ in range(SUBLANES - 1):
        pows.append(_cmul(*pows[-1], a_re, a_im))
    a1_re_ref[...] = bcast(a_re)
    a1_im_ref[...] = bcast(a_im)
    a8_re_ref[...] = bcast(pows[-1][0])
    a8_im_ref[...] = bcast(pows[-1][1])
    shape16 = (2 * SUBLANES,) + shape[1:]
    first = lax.broadcasted_iota(jnp.int32, shape16, 0) < SUBLANES
    for q in range(SUBLANES // 2):
        (r0, i0), (r1, i1) = pows[2 * q], pows[2 * q + 1]
        bc16 = lambda x: jnp.broadcast_to(x, shape16)
        apb_re_ref[q] = jnp.where(first, bc16(r0), bc16(r1)).astype(BF16)
        apb_im_ref[q] = jnp.where(first, bc16(i0), bc16(i1)).astype(BF16)
    lr_, li_ = pows[-1]
    n = SUBLANES
    while n < seg_len:
        lr_, li_ = _cmul(lr_, li_, lr_, li_)
        n *= 2
    lpows = [(lr_, li_)]
    for _ in range(SUBLANES - 1):
        lpows.append(_cmul(*lpows[-1], lr_, li_))
    row = lax.broadcasted_iota(jnp.int32, shape, 0)
    alp_re = jnp.zeros(shape, F32)
    alp_im = jnp.zeros(shape, F32)
    for i, (pr, pi) in enumerate(lpows):
        alp_re = jnp.where(row == i, bcast(pr), alp_re)
        alp_im = jnp.where(row == i, bcast(pi), alp_im)
    alp_re_ref[...] = alp_re
    alp_im_ref[...] = alp_im
    for i, s in enumerate((1, 2, 4)):
        pr, pi = lpows[s - 1]
        shl_re_ref[i] = jnp.where(row >= s, bcast(pr), 0.0)
        shl_im_ref[i] = jnp.where(row >= s, bcast(pi), 0.0)

    _, _, q_re, q_im = discretise(grid_ref[0], grid_ref[1], grid_ref[2])
    q_ref[0] = q_re
    q_ref[1] = q_im


def _s5_prep(lam_re, lam_im, log_dt, seg_len):
    g, p = lam_re.shape
    n = g * p
    lam3 = jnp.stack([lam_re, lam_im, jnp.broadcast_to(log_dt[:, None], (g, p))])
    s8 = jax.ShapeDtypeStruct((SUBLANES, n), F32)
    s88 = jax.ShapeDtypeStruct((SUBLANES // 2, 2 * SUBLANES, n), BF16)
    s38 = jax.ShapeDtypeStruct((3, SUBLANES, n), F32)
    sq = jax.ShapeDtypeStruct((2, g, p), F32)
    out_shape = (s8, s8, s8, s8, s88, s88, s8, s8, s38, s38, sq)
    return pl.pallas_call(functools.partial(_s5_prep_kernel, seg_len=seg_len),
                          out_shape=out_shape, name="s5_prep")(lam3.reshape(3, 1, n), lam3)


def _head_ones():
    head = np.arange(LANES) // RWKV_HEAD
    return jnp.asarray(head[:, None] == head[None, :], BF16)


def _head_sum(x, ones):
    rows, width = x.shape
    groups = width // LANES
    stacked = jnp.concatenate([x[:, j * LANES:(j + 1) * LANES] for j in range(groups)], axis=0)
    sums = _dot(stacked.astype(BF16), ones)
    return jnp.concatenate([sums[j * rows:(j + 1) * rows] for j in range(groups)], axis=1)


def _mix_in_kernel(*refs, d_ssm, d, tiles_per_seq, tt_s5):
    (x_ref, g_ref, w_ref, mu_ref, wwa_ref, g2_ref, w0_ref, a0_ref, kk_ref, ka_ref, rk_ref,
     ones_ref) = refs[:12]
    s5 = dict(zip(S5_CONSTS, refs[12:12 + len(S5_CONSTS)]))
    (w_rows_ref, oa_ref, r_ref, k2_ref, v_ref, av_ref, bv_ref, logw_ref, gate_ref, bonus_ref, h_ref,
     wg_rows_ref, prev_ref, carry_ref, *state_refs) = refs[12 + len(S5_CONSTS):]
    xs_refs, xb_refs = state_refs[:len(state_refs) // 2], state_refs[len(state_refs) // 2:]
    tm = x_ref.shape[0]

    @pl.when(pl.program_id(0) % tiles_per_seq == 0)
    def _():
        prev_ref[...] = jnp.zeros_like(prev_ref)
        carry_ref[...] = jnp.zeros_like(carry_ref)

    wg_rows_ref[...] = w_rows_ref[:, w_rows_ref.shape[1] - wg_rows_ref.shape[1]:].astype(BF16)

    h = _rmsnorm(x_ref[...], g_ref[...]).astype(BF16)
    h_ref[...] = h

    def mixed(c0, c1):
        p = _dot(h, w_ref[:, d_ssm + c0:d_ssm + c1])
        shape8 = (SUBLANES, c1 - c0)
        row = lax.broadcasted_iota(jnp.int32, shape8, 0)
        last = prev_ref[SUBLANES - 1:SUBLANES, c0:c1]
        prev = pltpu.roll(p, 1, 0)
        head = jnp.where(row == 0, jnp.broadcast_to(last, shape8), prev[:SUBLANES])
        prev = jnp.concatenate([head, prev[SUBLANES:]], axis=0)
        prev_ref[:, c0:c1] = p[tm - SUBLANES:, :]
        return p + (prev - p) * mu_ref[:, c0:c1]

    sub = lambda i: slice(i * tt_s5, (i + 1) * tt_s5)
    n_sub = tm // tt_s5
    us = [_dot(h[sub(i)], w_ref[:, :d_ssm]) for i in range(n_sub)]
    for i in range(n_sub):
        _s5_project(us[i], s5, xs_refs[i])
    _s5_scan(s5, xs_refs[0], xb_refs[0], carry_ref)
    lora_in = mixed(3 * d, 3 * d + GATE_LORA + DECAY_LORA + AAA_LORA)
    k = mixed(d, 2 * d)
    xg = lora_in[:, :GATE_LORA]
    xwa = lora_in[:, GATE_LORA:]
    lane = lax.broadcasted_iota(jnp.int32, xwa.shape, 1)
    zwa = jnp.where(lane < DECAY_LORA, jnp.tanh(xwa), xwa)
    lora = _dot(zwa.astype(BF16), wwa_ref[...])
    gate_ref[...] = _dot(jax.nn.sigmoid(xg).astype(BF16), g2_ref[...])
    logw_ref[...] = -EXP_NEG_HALF * jax.nn.sigmoid(w0_ref[...] + lora[:, :d])
    a = jax.nn.sigmoid(a0_ref[...] + lora[:, d:])
    for i in range(1, n_sub):
        _s5_scan(s5, xs_refs[i], xb_refs[i], carry_ref)
    oa_ref[sub(0), :] = _s5_readout(us[0], s5, xb_refs[0])

    ones = ones_ref[...]
    r = mixed(0, d)
    kk = k * kk_ref[...]
    kk = kk * lax.rsqrt(jnp.maximum(_head_sum(kk * kk, ones), L2_EPS * L2_EPS))
    v = mixed(2 * d, 3 * d)
    k2 = k * (1.0 + (a - 1.0) * ka_ref[...])
    k2_ref[...] = k2.astype(BF16)
    av_ref[...] = (-kk).astype(BF16)
    bv_ref[...] = (kk * a).astype(BF16)
    r_ref[...] = r.astype(BF16)
    v_ref[...] = v.astype(BF16)
    bonus_ref[...] = _head_sum(r * k2 * rk_ref[...], ones) * v
    for i in range(1, n_sub):
        oa_ref[sub(i), :] = _s5_readout(us[i], s5, xb_refs[i])


def _mix_in(x2, seq, norm_g, w_in, d_ssm, mu, w0, w2, a0, a2, g2, k_k, k_a, r_k, s5_params):
    t, dm = x2.shape
    n_mix = d_ssm + mu.shape[0]
    n_gate = w_in.shape[1] - n_mix
    w_mix = w_in[:, :n_mix].astype(BF16)
    d = w0.shape[0]
    tm = min(TM_PROJ, seq)
    tt_s5 = min(TT_S5, tm)
    s5_consts = _s5_constants(tt_s5, *s5_params)
    n_state2 = s5_consts[2].shape[2]
    wwa = jnp.concatenate([jnp.concatenate([w2, jnp.zeros_like(w2)], axis=1),
                           jnp.concatenate([jnp.zeros_like(a2), a2], axis=1)], axis=0)
    ones = _head_ones()
    vec = lambda a: a.reshape(1, -1)
    consts = [vec(norm_g), w_mix, vec(mu), wwa.astype(BF16), g2.astype(BF16), vec(w0), vec(a0),
              vec(k_k), vec(k_a), vec(r_k), ones, *s5_consts]
    whole = lambda a: pl.BlockSpec(a.shape, lambda i: (0,) * a.ndim)
    tok = lambda i: (i, 0)
    widths = [(d_ssm, F32)] + [(d, BF16)] * 5 + [(d, F32)] * 3 + [(dm, BF16)]
    n_steps = t // tm
    parts = max(p for p in range(1, n_steps + 1) if dm % (p * BF16_TILE_ROWS) == 0)
    cast_block = lambda i: (jnp.minimum(i, parts - 1), 0)
    return pl.pallas_call(
        functools.partial(_mix_in_kernel, d_ssm=d_ssm, d=d, tiles_per_seq=seq // tm, tt_s5=tt_s5),
        grid=(n_steps,),
        in_specs=[pl.BlockSpec((tm, dm), tok)] + [whole(c) for c in consts]
        + [pl.BlockSpec((dm // parts, w_in.shape[1]), cast_block)],
        out_specs=[pl.BlockSpec((tm, w), tok) for w, _ in widths]
        + [pl.BlockSpec((dm // parts, n_gate), cast_block)],
        out_shape=[jax.ShapeDtypeStruct((t, w), dt) for w, dt in widths]
        + [jax.ShapeDtypeStruct((dm, n_gate), BF16)],
        scratch_shapes=[pltpu.VMEM((SUBLANES, mu.shape[0]), F32),
                        pltpu.VMEM((SUBLANES, 2 * n_state2), F32)]
        + [pltpu.VMEM((tt_s5, 2 * n_state2), F32)] * (tm // tt_s5)
        + [pltpu.VMEM((tt_s5, 2 * n_state2), BF16)] * (tm // tt_s5),
        compiler_params=pltpu.CompilerParams(dimension_semantics=("arbitrary",),
                                             vmem_limit_bytes=VMEM_LIMIT),
        name="mix_in",
    )(x2, *consts, w_in)


S5_CONSTS = ("perm", "permt", "bmat", "cmat", "a1_re", "a1_im", "a8_re", "a8_im", "apb_re", "apb_im",
             "alp_re", "alp_im", "shl_re", "shl_im", "d_skip", "w_glu", "b_glu")


def _s5_project(u, s5, xs_ref):
    n_state = xs_ref.shape[1] // 2
    half = n_state // 2
    lw = S5_LANE_CHUNK
    ub = _dot(s5["perm"][...], u.astype(BF16)).astype(BF16)
    d_half = ub.shape[1] // 2
    for c in range(n_state // lw):
        h, cc = divmod(c, half // lw)
        for part in range(2):
            xs_ref[:, pl.ds(part * n_state + c * lw, lw)] = _dot(
                ub[:, h * d_half:(h + 1) * d_half],
                s5["bmat"][h, :, pl.ds(part * half + cc * lw, lw)])


def _s5_scan(s5, xs_ref, xb_ref, carry_ref):
    tt = xs_ref.shape[0]
    n_state = xs_ref.shape[1] // 2
    seg_len = tt // SUBLANES
    lw = S5_LANE_CHUNK
    blk = lambda j: pl.ds(j * SUBLANES, SUBLANES)
    for c in range(n_state // lw):
        re_sl, im_sl = pl.ds(c * lw, lw), pl.ds(n_state + c * lw, lw)
        ar, ai = s5["a1_re"][:, re_sl], s5["a1_im"][:, re_sl]
        xr = jnp.zeros((SUBLANES, lw), F32)
        xi = jnp.zeros((SUBLANES, lw), F32)
        for j in range(seg_len):
            br = xs_ref[blk(j), re_sl]
            bi = xs_ref[blk(j), im_sl]
            xr, xi = ar * xr - ai * xi + br, ar * xi + ai * xr + bi
            xs_ref[blk(j), re_sl] = xr
            xs_ref[blk(j), im_sl] = xi
        er, ei = xr, xi
        for k, sh in enumerate((1, 2, 4)):
            pr, pi = s5["shl_re"][k, :, re_sl], s5["shl_im"][k, :, re_sl]
            sr, si = pltpu.roll(er, sh, 0), pltpu.roll(ei, sh, 0)
            er, ei = er + (pr * sr - pi * si), ei + (pr * si + pi * sr)
        tr, ti = carry_ref[:, re_sl], carry_ref[:, im_sl]
        pr, pi = s5["alp_re"][:, re_sl], s5["alp_im"][:, re_sl]
        fr = er + (pr * tr - pi * ti)
        fi = ei + (pr * ti + pi * tr)
        row = lax.broadcasted_iota(jnp.int32, fr.shape, 0)
        dr = jnp.where(row == 0, tr, pltpu.roll(fr, 1, 0))
        di = jnp.where(row == 0, ti, pltpu.roll(fi, 1, 0))
        last = SUBLANES - 1
        carry_ref[:, re_sl] = jnp.broadcast_to(fr[last:last + 1], fr.shape)
        carry_ref[:, im_sl] = jnp.broadcast_to(fi[last:last + 1], fi.shape)
        a8r, a8i = s5["a8_re"][:, re_sl], s5["a8_im"][:, re_sl]
        both = lambda x: jnp.concatenate([x, x], axis=0).astype(BF16)
        for a in range(seg_len // SUBLANES):
            dr2, di2 = both(dr), both(di)
            for q in range(SUBLANES // 2):
                rows = pl.ds((a * SUBLANES + 2 * q) * SUBLANES, 2 * SUBLANES)
                pr, pi = s5["apb_re"][q, :, re_sl], s5["apb_im"][q, :, re_sl]
                xb_ref[rows, re_sl] = xs_ref[rows, re_sl].astype(BF16) + (pr * dr2 - pi * di2)
                xb_ref[rows, im_sl] = xs_ref[rows, im_sl].astype(BF16) + (pr * di2 + pi * dr2)
            dr, di = a8r * dr - a8i * di, a8r * di + a8i * dr


def _s5_readout(u, s5, xb_ref):
    n_state = xb_ref.shape[1] // 2
    half = n_state // 2
    lw = S5_LANE_CHUNK
    n_chunks = n_state // lw
    width = u.shape[1] // n_chunks
    blocks = []
    for c in range(n_chunks):
        h, cc = divmod(c, half // lw)
        cols = pl.ds(cc * width, width)
        blocks.append(sum(_dot(xb_ref[:, pl.ds(part * n_state + c * lw, lw)],
                               s5["cmat"][h, pl.ds(part * half + cc * lw, lw), cols])
                          for part in range(2)))
    cx_hi, cx_lo = _split2(jnp.concatenate(blocks, axis=1))
    permt = s5["permt"][...]
    y = _dot(permt, cx_hi) + _dot(permt, cx_lo) + s5["d_skip"][...] * u
    z = jax.nn.gelu(y)
    gate = jax.nn.sigmoid(_dot(z.astype(BF16), s5["w_glu"][...]) + s5["b_glu"][...])
    return z * gate


def _s5_constants(tt, lam_re, lam_im, log_dt, b_re, b_im, c_re, c_im, d_skip, w_glu, b_glu):
    d_ssm = d_skip.shape[0]
    n_grp, n_state, grp = b_re.shape
    seg_len = tt // SUBLANES
    *scan_consts, q = _s5_prep(lam_re, lam_im, log_dt, seg_len)
    q_re, q_im = q[0][:, :, None], q[1][:, :, None]
    bb = jnp.stack([q_re * b_re - q_im * b_im, q_re * b_im + q_im * b_re])
    cc = jnp.stack([c_re, -c_im])
    hg = n_grp // 2
    hc, hs2 = hg * grp, 2 * hg * n_state
    bb = bb.reshape(2, 2, hg, n_state, grp).transpose(1, 2, 4, 0, 3).reshape(2, hc, 2 * n_state)
    cc = cc.reshape(2, 2, hg, grp, n_state).transpose(1, 0, 4, 2, 3).reshape(2, 2 * n_state, hc)
    col = np.arange(hs2)
    repeat = (col[None, :] // (hg * n_state) == np.arange(2 * n_state)[:, None] // n_state) \
        & (col[None, :] % n_state == np.arange(2 * n_state)[:, None] % n_state)
    same = np.arange(hc)[:, None] // grp == (col[None, :] // n_state) % hg
    repeat = jnp.asarray(repeat, BF16)
    bmat = jnp.einsum("hck,kn->hcn", bb.astype(BF16), repeat, preferred_element_type=BF16)
    bmat = jnp.where(same, bmat, 0.0)
    cmat = jnp.einsum("nk,hkc->hnc", repeat.T, cc.astype(BF16), preferred_element_type=BF16)
    cmat = jnp.where(same.T, cmat, 0.0)
    rho = np.arange(tt)
    src = (rho % SUBLANES) * seg_len + rho // SUBLANES
    perm = src[:, None] == np.arange(tt)[None, :]
    return [jnp.asarray(perm, BF16), jnp.asarray(perm.T, BF16), bmat, cmat, *scan_consts,
            d_skip.reshape(1, d_ssm), w_glu.astype(BF16), b_glu.reshape(1, d_ssm)]


def _pair_blocks(x):
    lane = lax.broadcasted_iota(jnp.int32, x.shape, 1)
    first = lane < RWKV_HEAD
    return jnp.concatenate([jnp.where(first, x, 0.0), jnp.where(first, 0.0, x)], axis=0)


def _pair_transpose(x):
    t = _pair_blocks(x).T
    return t[:CHUNK] + t[CHUNK:]


def _rwkv_chunks_scores(items):
    shape = (CHUNK, LANES)
    row = lax.broadcasted_iota(jnp.int32, shape, 0)
    col = lax.broadcasted_iota(jnp.int32, shape, 1) % RWKV_HEAD
    strict = row > col
    incl = row >= col
    eye = row == col
    blocks = lambda x: _pair_blocks(x).astype(BF16)

    for it in items:
        lhs = jnp.concatenate([it["at"], it["rt"]], axis=0).astype(BF16)
        rhs = jnp.concatenate([blocks(it["bt"]), blocks(it["kt"])], axis=0)
        sc = _dot_nt(lhs, rhs)
        it["s_ab"] = jnp.where(strict, sc[:CHUNK, :LANES], 0.0)
        it["s_ak"] = jnp.where(strict, sc[:CHUNK, LANES:], 0.0).astype(BF16)
        it["s_r"] = jnp.concatenate([jnp.where(incl, sc[CHUNK:, :LANES], 0.0),
                                     jnp.where(incl, sc[CHUNK:, LANES:], 0.0)], axis=1).astype(BF16)
        it["bd_v"] = blocks(it["v"])
        it["bk_t"] = jnp.concatenate([_pair_transpose(it["bh"]), _pair_transpose(it["kh"])],
                                     axis=1).astype(BF16)

    return strict, incl, eye, blocks


def _rwkv_chunks_rest(items, h_ref, eye, blocks, filler):
    shape = (CHUNK, LANES)
    for it in items:
        it["npow"] = _dot(it["s_ab"].astype(BF16), blocks(it["s_ab"]))
        it["inv"] = jnp.where(eye, 1.0, 0.0) + it["s_ab"]
    filler()
    for step in range(4):
        for it in items:
            both = _dot(jnp.concatenate([it["inv"], it["npow"]], axis=0).astype(BF16),
                        blocks(it["npow"]))
            it["inv"] = it["inv"] + both[:CHUNK]
            it["npow"] = both[CHUNK:]
        filler()
    for it in items:
        it["inv"] = it["inv"] + _dot(it["inv"].astype(BF16), blocks(it["npow"]))
    filler()

    for it in items:
        it["av"] = _dot(it["s_ak"], it["bd_v"])
    for it in items:
        wu = _dot(it["inv"].astype(BF16),
                  jnp.concatenate([blocks(it["at"]), blocks(it["av"])], axis=1))
        it["z"] = jnp.concatenate(
            [jnp.concatenate([blocks(wu[:, :LANES]), blocks(wu[:, LANES:])], axis=1),
             jnp.concatenate([jnp.zeros((LANES, LANES), BF16), it["bd_v"]], axis=1)], axis=0)
    filler()
    for it in items:
        both = _dot(jnp.concatenate([it["s_r"], it["bk_t"]], axis=0), it["z"])
        qy, mg = both[:CHUNK], both[CHUNK:]
        m = jnp.where(eye, jnp.broadcast_to(it["decay"], shape), 0.0) + mg[:, :LANES]
        it["qm"] = jnp.concatenate([it["rt"] + qy[:, :LANES], m], axis=0).astype(BF16)
        it["yp"] = qy[:, LANES:]
        it["g"] = mg[:, LANES:]
    filler()

    for it in items:
        both = _dot(it["qm"], blocks(h_ref[it["pair"]]))
        it["y"] = both[:CHUNK] + it["yp"]
        h_ref[it["pair"]] = both[CHUNK:] + it["g"]


OPERANDS = ("at", "rt", "bt", "kt", "bh", "kh", "v")


def _rwkv_kernel(r_ref, k2_ref, v_ref, av_ref, bv_ref, logw_ref, gate_ref, bonus_ref, hn_ref,
                 win_ref, lng_ref, lnb_ref, ones_ref, ltri_ref, *refs, n_cast, tiles_per_seq):
    cast_in, (o_ref, zg_ref) = refs[:n_cast], refs[n_cast:n_cast + 2]
    cast_out = refs[n_cast + 2:2 * n_cast + 2]
    h_ref, decay_ref, *op_refs = refs[2 * n_cast + 2:]
    for src, dst in zip(cast_in, cast_out):
        dst[...] = src[...].astype(BF16)
    tt, d = r_ref.shape
    n_pairs = d // LANES
    n_chunks = tt // CHUNK
    ops = dict(zip(OPERANDS, op_refs))

    step = pl.program_id(0)

    @pl.when(step == 0)
    def _():
        for ref in (decay_ref,) + tuple(op_refs):
            ref[...] = jnp.zeros_like(ref)

    @pl.when(lax.rem(jnp.maximum(step - 1, 0), tiles_per_seq) == 0)
    def _():
        h_ref[...] = jnp.zeros_like(h_ref)

    items = []
    for c in range(n_chunks):
        rows = pl.ds(c * CHUNK, CHUNK)
        for j in range(n_pairs):
            ln = pl.ds(j * LANES, LANES)
            item = {name: ref[rows, ln] for name, ref in ops.items()}
            item["decay"] = decay_ref[c, :, ln]
            item["pair"] = j
            items.append(item)
    _, _, eye, blocks = _rwkv_chunks_scores(items)

    for c in range(n_chunks):
        rows = pl.ds(c * CHUNK, CHUNK)
        load = lambda ref: ref[rows, :].astype(F32)
        r, k2, v, av, bv, logw = map(load, (r_ref, k2_ref, v_ref, av_ref, bv_ref, logw_ref))
        l_hi, l_lo = _split2(logw)
        lc = _dot(ltri_ref[...], l_hi) + _dot(ltri_ref[...], l_lo)
        lc_end = lc[CHUNK - 1:CHUNK]
        e_in = jnp.exp(lc - logw)
        e_inc = jnp.exp(lc)
        e_neg = jnp.exp(-lc)
        e_out = jnp.exp(lc_end - lc)
        new = dict(at=av * e_in, rt=r * e_inc, bt=bv * e_neg, kt=k2 * e_neg, bh=bv * e_out,
                   kh=k2 * e_out, v=v)
        for name, val in new.items():
            ops[name][rows, :] = val
        decay_ref[c] = jnp.exp(lc_end)

    n_fill = 8
    piece = zg_ref.shape[1] // n_fill
    gate_col0 = win_ref.shape[1] - zg_ref.shape[1]
    pending = list(range(n_fill))

    def filler():
        if pending:
            q = pending.pop(0)
            zg_ref[:, pl.ds(q * piece, piece)] = _dot(
                hn_ref[...], win_ref[:, pl.ds(gate_col0 + q * piece, piece)]).astype(BF16)

    _rwkv_chunks_rest(items, h_ref, eye, blocks, filler)
    while pending:
        filler()
    ones = ones_ref[...]
    inv_n = 1.0 / RWKV_HEAD
    for c in range(n_chunks):
        rows = pl.ds(c * CHUNK, CHUNK)
        y = jnp.concatenate([it["y"] for it in items[c * n_pairs:(c + 1) * n_pairs]], axis=1)
        mean = _head_sum(y, ones) * inv_n
        yc = y - mean
        var = _head_sum(yc * yc, ones) * inv_n
        yn = yc * lax.rsqrt(var + GN_EPS) * lng_ref[...] + lnb_ref[...]
        o_ref[rows, :] = (yn + bonus_ref[rows, :]) * gate_ref[rows, :]


def _rwkv_branch(mix, w_in, n_gate, bsz, lnx_g, lnx_b, later_weights):
    *step_in, gate, bonus, hn = mix
    t, d = gate.shape
    seq = t // bsz
    tt = min(TT_RWKV, seq)
    nt = seq // tt
    n_tiles = bsz * nt
    n_steps = n_tiles + 1
    parts = max(p for p in range(1, n_steps + 1)
                if all(w.shape[0] % (p * BF16_TILE_ROWS) == 0 for w in later_weights))
    cast_block = lambda i: (jnp.minimum(i, parts - 1), 0)
    cast_specs = [pl.BlockSpec((w.shape[0] // parts, w.shape[1]), cast_block) for w in later_weights]
    ones = _head_ones()
    ltri = jnp.asarray(np.tril(np.ones((CHUNK, CHUNK), bool)), BF16)
    this_tile = lambda i: (jnp.minimum(i, n_tiles - 1), 0)
    prev_tile = lambda i: (jnp.maximum(i - 1, 0), 0)
    c2 = lambda i: (0, 0)
    consts = [lnx_g.reshape(1, d), lnx_b.reshape(1, d), ones, ltri]
    return pl.pallas_call(
        functools.partial(_rwkv_kernel, n_cast=len(later_weights), tiles_per_seq=nt),
        grid=(n_steps,),
        in_specs=[pl.BlockSpec((tt, d), this_tile) for _ in step_in]
        + [pl.BlockSpec((tt, d), prev_tile)] * 2 + [pl.BlockSpec((tt, hn.shape[1]), prev_tile)]
        + [pl.BlockSpec(w_in.shape, c2, pipeline_mode=pl.Buffered(1))]
        + [pl.BlockSpec(c.shape, c2) for c in consts] + cast_specs,
        out_specs=[pl.BlockSpec((tt, d), prev_tile), pl.BlockSpec((tt, n_gate), prev_tile)]
        + cast_specs,
        out_shape=[jax.ShapeDtypeStruct((t, d), F32), jax.ShapeDtypeStruct((t, n_gate), BF16)]
        + [jax.ShapeDtypeStruct(w.shape, BF16) for w in later_weights],
        scratch_shapes=[pltpu.VMEM((d // LANES, RWKV_HEAD, LANES), F32),
                        pltpu.VMEM((tt // CHUNK, 1, d), F32)]
        + [pltpu.VMEM((tt, d), F32)] * len(OPERANDS),
        compiler_params=pltpu.CompilerParams(dimension_semantics=("arbitrary",),
                                             vmem_limit_bytes=VMEM_LIMIT),
        name="rwkv",
    )(*step_in, gate, bonus, hn, w_in, *consts, *later_weights)


def _tail_kernel(x_ref, oa_ref, ob_ref, zg_ref, wb_ref, wout_ref, g2_ref,
                 w1_ref, w2_ref, gf_ref, o_ref, *, ff_block):
    x = x_ref[...]
    d = x.shape[1]
    d_mix = oa_ref.shape[1]
    gates = jax.nn.sigmoid(zg_ref[...].astype(F32))
    ma = _dot(oa_ref[...].astype(BF16), wb_ref[:d_mix, :])
    mb = _dot(ob_ref[...].astype(BF16), wb_ref[d_mix:, :])
    merged = gates[:, :d] * ma + gates[:, d:] * mb
    x1 = x + _dot(merged.astype(BF16), wout_ref[...])
    h2 = _rmsnorm(x1, g2_ref[...]).astype(BF16)
    acc = x1
    for j in range(w1_ref.shape[1] // ff_block):
        cols = slice(j * ff_block, (j + 1) * ff_block)
        hid = jnp.square(jnp.maximum(_dot(h2, w1_ref[:, cols]), 0.0))
        acc = acc + _dot(hid.astype(BF16), w2_ref[cols, :])
    o_ref[...] = _rmsnorm(acc, gf_ref[...])


def _tail(x2, o_a, o_b, zg, w_branch, w_out, norm2_g, w_ff1, w_ff2, norm_f_g):
    t, d = x2.shape
    d_mix = o_a.shape[1]
    tm = min(TM_PROJ, t)
    tok = lambda i: (i, 0)
    const = lambda i: (0, 0)
    resident = lambda a: pl.BlockSpec(a.shape, const, pipeline_mode=pl.Buffered(1))
    vec = lambda a: a.reshape(1, d)
    return pl.pallas_call(
        functools.partial(_tail_kernel, ff_block=d),
        grid=(t // tm,),
        in_specs=[pl.BlockSpec((tm, d), tok), pl.BlockSpec((tm, d_mix), tok),
                  pl.BlockSpec((tm, d_mix), tok), pl.BlockSpec((tm, zg.shape[1]), tok),
                  resident(w_branch), resident(w_out),
                  pl.BlockSpec((1, d), const), resident(w_ff1), resident(w_ff2),
                  pl.BlockSpec((1, d), const)],
        out_specs=pl.BlockSpec((tm, d), tok),
        out_shape=jax.ShapeDtypeStruct((t, d), F32),
        compiler_params=pltpu.CompilerParams(dimension_semantics=("arbitrary",),
                                             vmem_limit_bytes=VMEM_LIMIT),
        name="tail",
    )(x2, o_a, o_b, zg, w_branch, w_out, vec(norm2_g), w_ff1, w_ff2, vec(norm_f_g))


def _layer(x2, bsz, norm1_g, w_in, lam_re, lam_im, log_dt, b_re, b_im, c_re, c_im, d_skip, w_glu,
           b_glu, mu_rwkv, w0, w2, a0, a2, g2, k_k, k_a, r_k, lnx_g, lnx_b, w_branch, w_out,
           norm2_g, w_ff1, w_ff2, out_g):
    d_ssm = d_skip.shape[0]
    o_a, *mix, w_gate = _mix_in(x2, x2.shape[0] // bsz, norm1_g, w_in, d_ssm, mu_rwkv, w0, w2, a0, a2, g2,
                        k_k, k_a, r_k.reshape(-1),
                        (lam_re, lam_im, log_dt, b_re, b_im, c_re, c_im, d_skip, w_glu, b_glu))
    o_b, zg, *tail_weights = _rwkv_branch(mix, w_gate, w_gate.shape[1], bsz, lnx_g, lnx_b,
                                          (w_branch, w_out, w_ff1, w_ff2))
    w_branch, w_out, w_ff1, w_ff2 = tail_weights
    return _tail(x2, o_a, o_b, zg, w_branch, w_out, norm2_g, w_ff1, w_ff2, out_g)


def kernel(x, norm1_g, w_in, lam_re, lam_im, log_dt, b_re, b_im, c_re, c_im, d_skip, w_glu, b_glu,
           mu_rwkv, w0, w2, a0, a2, g2, k_k, k_a, r_k, lnx_g, lnx_b, w_branch, w_out, norm2_g,
           w_ff1, w_ff2, norm_f_g):
    bsz, seq, d = x.shape
    depth = norm1_g.shape[0]
    assert depth == 1, "the fused tail applies the final norm, so exactly one layer is supported"
    x2 = x.reshape(bsz * seq, d)
    l = 0
    out = _layer(x2, bsz, norm1_g[l], w_in[l], lam_re[l], lam_im[l], log_dt[l], b_re[l], b_im[l],
                 c_re[l], c_im[l], d_skip[l], w_glu[l], b_glu[l], mu_rwkv[l], w0[l], w2[l], a0[l],
                 a2[l], g2[l], k_k[l], k_a[l], r_k[l], lnx_g[l], lnx_b[l], w_branch[l], w_out[l],
                 norm2_g[l], w_ff1[l], w_ff2[l], norm_f_g)
    return out.reshape(bsz, seq, d)
```

```python
import functools

import jax
import jax.numpy as jnp
import numpy as np
from jax import lax
from jax.experimental import pallas as pl
from jax.experimental.pallas import tpu as pltpu

F32 = jnp.float32
BF16 = jnp.bfloat16

RMS_EPS = 1e-6
GN_EPS = 64e-5
L2_EPS = 1e-12
MIN_NEG_REAL = -1e-4
EXP_NEG_HALF = 0.6065306597126334

SSM_GROUP = 16
SSM_STATE = 64
RWKV_HEAD = 64
GATE_LORA = 128
DECAY_LORA = 64
AAA_LORA = 64
CHUNK = 64

SUBLANES = 8
LANES = 128
BF16_TILE_ROWS = 2 * SUBLANES
VMEM_LIMIT = 56 * 1024 * 1024

TM_PROJ = 512
TT_S5 = 256
TT_RWKV = 256
S5_LANE_CHUNK = 512


def _dot(a, b):
    return jnp.dot(a, b, preferred_element_type=F32)


def _dot_nt(a, b):
    return lax.dot_general(a, b, (((1,), (1,)), ((), ())), preferred_element_type=F32)


def _split2(x):
    hi = x.astype(BF16)
    lo = (x - hi.astype(F32)).astype(BF16)
    return hi, lo


def _split3(x):
    hi = x.astype(BF16)
    r1 = x - hi.astype(F32)
    lo = r1.astype(BF16)
    lo2 = (r1 - lo.astype(F32)).astype(BF16)
    return hi, lo, lo2


def _rmsnorm(x, g):
    ms = jnp.mean(x * x, axis=-1, keepdims=True)
    return x * lax.rsqrt(ms + RMS_EPS) * g


def _cmul(ar, ai, br, bi):
    return ar * br - ai * bi, ar * bi + ai * br


def _s5_prep_kernel(rows_ref, grid_ref, a1_re_ref, a1_im_ref, a8_re_ref, a8_im_ref, apb_re_ref,
                    apb_im_ref, alp_re_ref, alp_im_ref, shl_re_ref, shl_im_ref, q_ref, *, seg_len):
    def discretise(lre, lim, ldt):
        lr = jnp.minimum(lre, MIN_NEG_REAL)
        dt = jnp.exp(ldt)
        mag = jnp.exp(lr * dt)
        ab_re = mag * jnp.cos(lim * dt)
        ab_im = mag * jnp.sin(lim * dt)
        den = lr * lr + lim * lim
        xm1 = ab_re - 1.0
        q_re = (xm1 * lr + ab_im * lim) / den
        q_im = (ab_im * lr - xm1 * lim) / den
        return ab_re, ab_im, q_re, q_im

    shape = a1_re_ref.shape
    a_re, a_im, _, _ = discretise(rows_ref[0], rows_ref[1], rows_ref[2])
    bcast = lambda x: jnp.broadcast_to(x, shape)
    pows = [(a_re, a_im)]
    for _ in range(SUBLANES - 1):
        pows.append(_cmul(*pows[-1], a_re, a_im))
    a1_re_ref[...] = bcast(a_re)
    a1_im_ref[...] = bcast(a_im)
    a8_re_ref[...] = bcast(pows[-1][0])
    a8_im_ref[...] = bcast(pows[-1][1])
    shape16 = (2 * SUBLANES,) + shape[1:]
    first = lax.broadcasted_iota(jnp.int32, shape16, 0) < SUBLANES
    for q in range(SUBLANES // 2):
        (r0, i0), (r1, i1) = pows[2 * q], pows[2 * q + 1]
        bc16 = lambda x: jnp.broadcast_to(x, shape16)
        apb_re_ref[q] = jnp.where(first, bc16(r0), bc16(r1)).astype(BF16)
        apb_im_ref[q] = jnp.where(first, bc16(i0), bc16(i1)).astype(BF16)
    lr_, li_ = pows[-1]
    n = SUBLANES
    while n < seg_len:
        lr_, li_ = _cmul(lr_, li_, lr_, li_)
        n *= 2
    lpows = [(lr_, li_)]
    for _ in range(SUBLANES - 1):
        lpows.append(_cmul(*lpows[-1], lr_, li_))
    row = lax.broadcasted_iota(jnp.int32, shape, 0)
    alp_re = jnp.zeros(shape, F32)
    alp_im = jnp.zeros(shape, F32)
    for i, (pr, pi) in enumerate(lpows):
        alp_re = jnp.where(row == i, bcast(pr), alp_re)
        alp_im = jnp.where(row == i, bcast(pi), alp_im)
    alp_re_ref[...] = alp_re
    alp_im_ref[...] = alp_im
    for i, s in enumerate((1, 2, 4)):
        pr, pi = lpows[s - 1]
        shl_re_ref[i] = jnp.where(row >= s, bcast(pr), 0.0)
        shl_im_ref[i] = jnp.where(row >= s, bcast(pi), 0.0)

    _, _, q_re, q_im = discretise(grid_ref[0], grid_ref[1], grid_ref[2])
    q_ref[0] = q_re
    q_ref[1] = q_im


def _s5_prep(lam_re, lam_im, log_dt, seg_len):
    g, p = lam_re.shape
    n = g * p
    lam3 = jnp.stack([lam_re, lam_im, jnp.broadcast_to(log_dt[:, None], (g, p))])
    s8 = jax.ShapeDtypeStruct((SUBLANES, n), F32)
    s88 = jax.ShapeDtypeStruct((SUBLANES // 2, 2 * SUBLANES, n), BF16)
    s38 = jax.ShapeDtypeStruct((3, SUBLANES, n), F32)
    sq = jax.ShapeDtypeStruct((2, g, p), F32)
    out_shape = (s8, s8, s8, s8, s88, s88, s8, s8, s38, s38, sq)
    return pl.pallas_call(functools.partial(_s5_prep_kernel, seg_len=seg_len),
                          out_shape=out_shape, name="s5_prep")(lam3.reshape(3, 1, n), lam3)


def _head_ones():
    head = np.arange(LANES) // RWKV_HEAD
    return jnp.asarray(head[:, None] == head[None, :], BF16)


def _head_sum(x, ones):
    rows, width = x.shape
    groups = width // LANES
    stacked = jnp.concatenate([x[:, j * LANES:(j + 1) * LANES] for j in range(groups)], axis=0)
    sums = _dot(stacked.astype(BF16), ones)
    return jnp.concatenate([sums[j * rows:(j + 1) * rows] for j in range(groups)], axis=1)


def _mix_in_kernel(*refs, d_ssm, d, tiles_per_seq, tt_s5):
    (x_ref, g_ref, w_f32_ref, mu_ref, wwa_ref, g2_ref, w0_ref, a0_ref, kk_ref, ka_ref, rk_ref,
     ones_ref) = refs[:12]
    s5 = dict(zip(S5_CONSTS, refs[12:12 + len(S5_CONSTS)]))
    (w_rows_ref, oa_ref, r_ref, k2_ref, v_ref, av_ref, bv_ref, logw_ref, gate_ref, bonus_ref, h_ref,
     wg_rows_ref, w_ref, prev_ref, carry_ref, *state_refs) = refs[12 + len(S5_CONSTS):]
    xs_refs, xb_refs = state_refs[:len(state_refs) // 2], state_refs[len(state_refs) // 2:]
    tm = x_ref.shape[0]

    @pl.when(pl.program_id(0) == 0)
    def _():
        rows_per_iter = 4 * BF16_TILE_ROWS

        def cast_rows(i, _):
            rows = pl.ds(pl.multiple_of(i * rows_per_iter, rows_per_iter), rows_per_iter)
            w_ref[rows, :] = w_f32_ref[rows, :].astype(BF16)
            return 0

        lax.fori_loop(0, w_ref.shape[0] // rows_per_iter, cast_rows, 0)

    @pl.when(pl.program_id(0) % tiles_per_seq == 0)
    def _():
        prev_ref[...] = jnp.zeros_like(prev_ref)
        carry_ref[...] = jnp.zeros_like(carry_ref)

    wg_rows_ref[...] = w_rows_ref[:, w_rows_ref.shape[1] - wg_rows_ref.shape[1]:].astype(BF16)

    h = _rmsnorm(x_ref[...], g_ref[...]).astype(BF16)
    h_ref[...] = h

    def mixed(c0, c1):
        p = _dot(h, w_ref[:, d_ssm + c0:d_ssm + c1])
        shape8 = (SUBLANES, c1 - c0)
        row = lax.broadcasted_iota(jnp.int32, shape8, 0)
        last = prev_ref[SUBLANES - 1:SUBLANES, c0:c1]
        prev = pltpu.roll(p, 1, 0)
        head = jnp.where(row == 0, jnp.broadcast_to(last, shape8), prev[:SUBLANES])
        prev = jnp.concatenate([head, prev[SUBLANES:]], axis=0)
        prev_ref[:, c0:c1] = p[tm - SUBLANES:, :]
        return p + (prev - p) * mu_ref[:, c0:c1]

    sub = lambda i: slice(i * tt_s5, (i + 1) * tt_s5)
    n_sub = tm // tt_s5
    us = [_dot(h[sub(i)], w_ref[:, :d_ssm]) for i in range(n_sub)]
    for i in range(n_sub):
        _s5_project(us[i], s5, xs_refs[i])
    _s5_scan(s5, xs_refs[0], xb_refs[0], carry_ref)
    lora_in = mixed(3 * d, 3 * d + GATE_LORA + DECAY_LORA + AAA_LORA)
    k = mixed(d, 2 * d)
    xg = lora_in[:, :GATE_LORA]
    xwa = lora_in[:, GATE_LORA:]
    lane = lax.broadcasted_iota(jnp.int32, xwa.shape, 1)
    zwa = jnp.where(lane < DECAY_LORA, jnp.tanh(xwa), xwa)
    lora = _dot(zwa.astype(BF16), wwa_ref[...])
    gate_ref[...] = _dot(jax.nn.sigmoid(xg).astype(BF16), g2_ref[...])
    logw_ref[...] = -EXP_NEG_HALF * jax.nn.sigmoid(w0_ref[...] + lora[:, :d])
    a = jax.nn.sigmoid(a0_ref[...] + lora[:, d:])
    for i in range(1, n_sub):
        _s5_scan(s5, xs_refs[i], xb_refs[i], carry_ref)
    oa_ref[sub(0), :] = _s5_readout(us[0], s5, xb_refs[0])

    ones = ones_ref[...]
    r = mixed(0, d)
    kk = k * kk_ref[...]
    kk = kk * lax.rsqrt(jnp.maximum(_head_sum(kk * kk, ones), L2_EPS * L2_EPS))
    v = mixed(2 * d, 3 * d)
    k2 = k * (1.0 + (a - 1.0) * ka_ref[...])
    k2_ref[...] = k2.astype(BF16)
    av_ref[...] = (-kk).astype(BF16)
    bv_ref[...] = (kk * a).astype(BF16)
    r_ref[...] = r.astype(BF16)
    v_ref[...] = v.astype(BF16)
    bonus_ref[...] = _head_sum(r * k2 * rk_ref[...], ones) * v
    for i in range(1, n_sub):
        oa_ref[sub(i), :] = _s5_readout(us[i], s5, xb_refs[i])


def _mix_in(x2, seq, norm_g, w_in, d_ssm, mu, w0, w2, a0, a2, g2, k_k, k_a, r_k, s5_params):
    t, dm = x2.shape
    n_mix = d_ssm + mu.shape[0]
    n_gate = w_in.shape[1] - n_mix
    d = w0.shape[0]
    tm = min(TM_PROJ, seq)
    tt_s5 = min(TT_S5, tm)
    s5_consts = _s5_constants(tt_s5, *s5_params)
    n_state2 = s5_consts[2].shape[2]
    wwa = jnp.concatenate([jnp.concatenate([w2, jnp.zeros_like(w2)], axis=1),
                           jnp.concatenate([jnp.zeros_like(a2), a2], axis=1)], axis=0)
    ones = _head_ones()
    vec = lambda a: a.reshape(1, -1)
    consts = [vec(norm_g), w_in, vec(mu), wwa.astype(BF16), g2.astype(BF16), vec(w0), vec(a0),
              vec(k_k), vec(k_a), vec(r_k), ones, *s5_consts]
    whole = lambda a: pl.BlockSpec(a.shape, lambda i: (0,) * a.ndim)
    mixer_cols = pl.BlockSpec((dm, n_mix), lambda i: (0, 0), pipeline_mode=pl.Buffered(1))
    tok = lambda i: (i, 0)
    widths = [(d_ssm, F32)] + [(d, BF16)] * 5 + [(d, F32)] * 3 + [(dm, BF16)]
    n_steps = t // tm
    parts = max(p for p in range(1, n_steps + 1) if dm % (p * BF16_TILE_ROWS) == 0)
    cast_block = lambda i: (jnp.minimum(i, parts - 1), 0)
    return pl.pallas_call(
        functools.partial(_mix_in_kernel, d_ssm=d_ssm, d=d, tiles_per_seq=seq // tm, tt_s5=tt_s5),
        grid=(n_steps,),
        in_specs=[pl.BlockSpec((tm, dm), tok)]
        + [mixer_cols if c is w_in else whole(c) for c in consts]
        + [pl.BlockSpec((dm // parts, w_in.shape[1]), cast_block)],
        out_specs=[pl.BlockSpec((tm, w), tok) for w, _ in widths]
        + [pl.BlockSpec((dm // parts, n_gate), cast_block)],
        out_shape=[jax.ShapeDtypeStruct((t, w), dt) for w, dt in widths]
        + [jax.ShapeDtypeStruct((dm, n_gate), BF16)],
        scratch_shapes=[pltpu.VMEM((dm, n_mix), BF16), pltpu.VMEM((SUBLANES, mu.shape[0]), F32),
                        pltpu.VMEM((SUBLANES, 2 * n_state2), F32)]
        + [pltpu.VMEM((tt_s5, 2 * n_state2), F32)] * (tm // tt_s5)
        + [pltpu.VMEM((tt_s5, 2 * n_state2), BF16)] * (tm // tt_s5),
        compiler_params=pltpu.CompilerParams(dimension_semantics=("arbitrary",),
                                             vmem_limit_bytes=VMEM_LIMIT),
        name="mix_in",
    )(x2, *consts, w_in)


S5_CONSTS = ("perm", "permt", "bmat", "cmat", "a1_re", "a1_im", "a8_re", "a8_im", "apb_re", "apb_im",
             "alp_re", "alp_im", "shl_re", "shl_im", "d_skip", "w_glu", "b_glu")


def _s5_project(u, s5, xs_ref):
    n_state = xs_ref.shape[1] // 2
    half = n_state // 2
    lw = S5_LANE_CHUNK
    ub = _dot(s5["perm"][...], u.astype(BF16)).astype(BF16)
    d_half = ub.shape[1] // 2
    for c in range(n_state // lw):
        h, cc = divmod(c, half // lw)
        for part in range(2):
            xs_ref[:, pl.ds(part * n_state + c * lw, lw)] = _dot(
                ub[:, h * d_half:(h + 1) * d_half],
                s5["bmat"][h, :, pl.ds(part * half + cc * lw, lw)])


def _s5_scan(s5, xs_ref, xb_ref, carry_ref):
    tt = xs_ref.shape[0]
    n_state = xs_ref.shape[1] // 2
    seg_len = tt // SUBLANES
    lw = S5_LANE_CHUNK
    blk = lambda j: pl.ds(j * SUBLANES, SUBLANES)
    for c in range(n_state // lw):
        re_sl, im_sl = pl.ds(c * lw, lw), pl.ds(n_state + c * lw, lw)
        ar, ai = s5["a1_re"][:, re_sl], s5["a1_im"][:, re_sl]
        xr = jnp.zeros((SUBLANES, lw), F32)
        xi = jnp.zeros((SUBLANES, lw), F32)
        for j in range(seg_len):
            br = xs_ref[blk(j), re_sl]
            bi = xs_ref[blk(j), im_sl]
            xr, xi = ar * xr - ai * xi + br, ar * xi + ai * xr + bi
            xs_ref[blk(j), re_sl] = xr
            xs_ref[blk(j), im_sl] = xi
        er, ei = xr, xi
        for k, sh in enumerate((1, 2, 4)):
            pr, pi = s5["shl_re"][k, :, re_sl], s5["shl_im"][k, :, re_sl]
            sr, si = pltpu.roll(er, sh, 0), pltpu.roll(ei, sh, 0)
            er, ei = er + (pr * sr - pi * si), ei + (pr * si + pi * sr)
        tr, ti = carry_ref[:, re_sl], carry_ref[:, im_sl]
        pr, pi = s5["alp_re"][:, re_sl], s5["alp_im"][:, re_sl]
        fr = er + (pr * tr - pi * ti)
        fi = ei + (pr * ti + pi * tr)
        row = lax.broadcasted_iota(jnp.int32, fr.shape, 0)
        dr = jnp.where(row == 0, tr, pltpu.roll(fr, 1, 0))
        di = jnp.where(row == 0, ti, pltpu.roll(fi, 1, 0))
        last = SUBLANES - 1
        carry_ref[:, re_sl] = jnp.broadcast_to(fr[last:last + 1], fr.shape)
        carry_ref[:, im_sl] = jnp.broadcast_to(fi[last:last + 1], fi.shape)
        a8r, a8i = s5["a8_re"][:, re_sl], s5["a8_im"][:, re_sl]
        both = lambda x: jnp.concatenate([x, x], axis=0).astype(BF16)
        for a in range(seg_len // SUBLANES):
            dr2, di2 = both(dr), both(di)
            for q in range(SUBLANES // 2):
                rows = pl.ds((a * SUBLANES + 2 * q) * SUBLANES, 2 * SUBLANES)
                pr, pi = s5["apb_re"][q, :, re_sl], s5["apb_im"][q, :, re_sl]
                xb_ref[rows, re_sl] = xs_ref[rows, re_sl].astype(BF16) + (pr * dr2 - pi * di2)
                xb_ref[rows, im_sl] = xs_ref[rows, im_sl].astype(BF16) + (pr * di2 + pi * dr2)
            dr, di = a8r * dr - a8i * di, a8r * di + a8i * dr


def _s5_readout(u, s5, xb_ref):
    n_state = xb_ref.shape[1] // 2
    half = n_state // 2
    lw = S5_LANE_CHUNK
    n_chunks = n_state // lw
    width = u.shape[1] // n_chunks
    blocks = []
    for c in range(n_chunks):
        h, cc = divmod(c, half // lw)
        cols = pl.ds(cc * width, width)
        blocks.append(sum(_dot(xb_ref[:, pl.ds(part * n_state + c * lw, lw)],
                               s5["cmat"][h, pl.ds(part * half + cc * lw, lw), cols])
                          for part in range(2)))
    cx_hi, cx_lo = _split2(jnp.concatenate(blocks, axis=1))
    permt = s5["permt"][...]
    y = _dot(permt, cx_hi) + _dot(permt, cx_lo) + s5["d_skip"][...] * u
    z = jax.nn.gelu(y)
    gate = jax.nn.sigmoid(_dot(z.astype(BF16), s5["w_glu"][...]) + s5["b_glu"][...])
    return z * gate


def _s5_constants(tt, lam_re, lam_im, log_dt, b_re, b_im, c_re, c_im, d_skip, w_glu, b_glu):
    d_ssm = d_skip.shape[0]
    n_grp, n_state, grp = b_re.shape
    seg_len = tt // SUBLANES
    *scan_consts, q = _s5_prep(lam_re, lam_im, log_dt, seg_len)
    q_re, q_im = q[0][:, :, None], q[1][:, :, None]
    bb = jnp.stack([q_re * b_re - q_im * b_im, q_re * b_im + q_im * b_re])
    cc = jnp.stack([c_re, -c_im])
    hg = n_grp // 2
    hc, hs2 = hg * grp, 2 * hg * n_state
    bb = bb.reshape(2, 2, hg, n_state, grp).transpose(1, 2, 4, 0, 3).reshape(2, hc, 2 * n_state)
    cc = cc.reshape(2, 2, hg, grp, n_state).transpose(1, 0, 4, 2, 3).reshape(2, 2 * n_state, hc)
    col = np.arange(hs2)
    repeat = (col[None, :] // (hg * n_state) == np.arange(2 * n_state)[:, None] // n_state) \
        & (col[None, :] % n_state == np.arange(2 * n_state)[:, None] % n_state)
    same = np.arange(hc)[:, None] // grp == (col[None, :] // n_state) % hg
    repeat = jnp.asarray(repeat, BF16)
    bmat = jnp.einsum("hck,kn->hcn", bb.astype(BF16), repeat, preferred_element_type=BF16)
    bmat = jnp.where(same, bmat, 0.0)
    cmat = jnp.einsum("nk,hkc->hnc", repeat.T, cc.astype(BF16), preferred_element_type=BF16)
    cmat = jnp.where(same.T, cmat, 0.0)
    rho = np.arange(tt)
    src = (rho % SUBLANES) * seg_len + rho // SUBLANES
    perm = src[:, None] == np.arange(tt)[None, :]
    return [jnp.asarray(perm, BF16), jnp.asarray(perm.T, BF16), bmat, cmat, *scan_consts,
            d_skip.reshape(1, d_ssm), w_glu.astype(BF16), b_glu.reshape(1, d_ssm)]


def _pair_blocks(x):
    lane = lax.broadcasted_iota(jnp.int32, x.shape, 1)
    first = lane < RWKV_HEAD
    return jnp.concatenate([jnp.where(first, x, 0.0), jnp.where(first, 0.0, x)], axis=0)


def _pair_transpose(x):
    t = _pair_blocks(x).T
    return t[:CHUNK] + t[CHUNK:]


def _rwkv_chunks_scores(items):
    shape = (CHUNK, LANES)
    row = lax.broadcasted_iota(jnp.int32, shape, 0)
    col = lax.broadcasted_iota(jnp.int32, shape, 1) % RWKV_HEAD
    strict = row > col
    incl = row >= col
    eye = row == col
    blocks = lambda x: _pair_blocks(x).astype(BF16)

    for it in items:
        lhs = jnp.concatenate([it["at"], it["rt"]], axis=0).astype(BF16)
        rhs = jnp.concatenate([blocks(it["bt"]), blocks(it["kt"])], axis=0)
        sc = _dot_nt(lhs, rhs)
        it["s_ab"] = jnp.where(strict, sc[:CHUNK, :LANES], 0.0)
        it["s_ak"] = jnp.where(strict, sc[:CHUNK, LANES:], 0.0).astype(BF16)
        it["s_r"] = jnp.concatenate([jnp.where(incl, sc[CHUNK:, :LANES], 0.0),
                                     jnp.where(incl, sc[CHUNK:, LANES:], 0.0)], axis=1).astype(BF16)
        it["bd_v"] = blocks(it["v"])
        it["bk_t"] = jnp.concatenate([_pair_transpose(it["bh"]), _pair_transpose(it["kh"])],
                                     axis=1).astype(BF16)

    return strict, incl, eye, blocks


def _rwkv_chunks_rest(items, h_ref, eye, blocks, filler):
    shape = (CHUNK, LANES)
    for it in items:
        it["npow"] = _dot(it["s_ab"].astype(BF16), blocks(it["s_ab"]))
        it["inv"] = jnp.where(eye, 1.0, 0.0) + it["s_ab"]
    filler()
    for step in range(4):
        for it in items:
            both = _dot(jnp.concatenate([it["inv"], it["npow"]], axis=0).astype(BF16),
                        blocks(it["npow"]))
            it["inv"] = it["inv"] + both[:CHUNK]
            it["npow"] = both[CHUNK:]
        filler()
    for it in items:
        it["inv"] = it["inv"] + _dot(it["inv"].astype(BF16), blocks(it["npow"]))
    filler()

    for it in items:
        it["av"] = _dot(it["s_ak"], it["bd_v"])
    for it in items:
        wu = _dot(it["inv"].astype(BF16),
                  jnp.concatenate([blocks(it["at"]), blocks(it["av"])], axis=1))
        it["z"] = jnp.concatenate(
            [jnp.concatenate([blocks(wu[:, :LANES]), blocks(wu[:, LANES:])], axis=1),
             jnp.concatenate([jnp.zeros((LANES, LANES), BF16), it["bd_v"]], axis=1)], axis=0)
    filler()
    for it in items:
        both = _dot(jnp.concatenate([it["s_r"], it["bk_t"]], axis=0), it["z"])
        qy, mg = both[:CHUNK], both[CHUNK:]
        m = jnp.where(eye, jnp.broadcast_to(it["decay"], shape), 0.0) + mg[:, :LANES]
        it["qm"] = jnp.concatenate([it["rt"] + qy[:, :LANES], m], axis=0).astype(BF16)
        it["yp"] = qy[:, LANES:]
        it["g"] = mg[:, LANES:]
    filler()

    for it in items:
        both = _dot(it["qm"], blocks(h_ref[it["pair"]]))
        it["y"] = both[:CHUNK] + it["yp"]
        h_ref[it["pair"]] = both[CHUNK:] + it["g"]


OPERANDS = ("at", "rt", "bt", "kt", "bh", "kh", "v")


def _rwkv_kernel(r_ref, k2_ref, v_ref, av_ref, bv_ref, logw_ref, gate_ref, bonus_ref, hn_ref,
                 win_ref, lng_ref, lnb_ref, ones_ref, ltri_ref, *refs, n_cast, tiles_per_seq):
    cast_in, (o_ref, zg_ref) = refs[:n_cast], refs[n_cast:n_cast + 2]
    cast_out = refs[n_cast + 2:2 * n_cast + 2]
    h_ref, decay_ref, *op_refs = refs[2 * n_cast + 2:]
    for src, dst in zip(cast_in, cast_out):
        dst[...] = src[...].astype(BF16)
    tt, d = r_ref.shape
    n_pairs = d // LANES
    n_chunks = tt // CHUNK
    ops = dict(zip(OPERANDS, op_refs))

    step = pl.program_id(0)

    @pl.when(step == 0)
    def _():
        for ref in (decay_ref,) + tuple(op_refs):
            ref[...] = jnp.zeros_like(ref)

    @pl.when(lax.rem(jnp.maximum(step - 1, 0), tiles_per_seq) == 0)
    def _():
        h_ref[...] = jnp.zeros_like(h_ref)

    items = []
    for c in range(n_chunks):
        rows = pl.ds(c * CHUNK, CHUNK)
        for j in range(n_pairs):
            ln = pl.ds(j * LANES, LANES)
            item = {name: ref[rows, ln] for name, ref in ops.items()}
            item["decay"] = decay_ref[c, :, ln]
            item["pair"] = j
            items.append(item)
    _, _, eye, blocks = _rwkv_chunks_scores(items)

    for c in range(n_chunks):
        rows = pl.ds(c * CHUNK, CHUNK)
        load = lambda ref: ref[rows, :].astype(F32)
        r, k2, v, av, bv, logw = map(load, (r_ref, k2_ref, v_ref, av_ref, bv_ref, logw_ref))
        l_hi, l_lo = _split2(logw)
        lc = _dot(ltri_ref[...], l_hi) + _dot(ltri_ref[...], l_lo)
        lc_end = lc[CHUNK - 1:CHUNK]
        e_in = jnp.exp(lc - logw)
        e_inc = jnp.exp(lc)
        e_neg = jnp.exp(-lc)
        e_out = jnp.exp(lc_end - lc)
        new = dict(at=av * e_in, rt=r * e_inc, bt=bv * e_neg, kt=k2 * e_neg, bh=bv * e_out,
                   kh=k2 * e_out, v=v)
        for name, val in new.items():
            ops[name][rows, :] = val
        decay_ref[c] = jnp.exp(lc_end)

    n_fill = 8
    piece = zg_ref.shape[1] // n_fill
    gate_col0 = win_ref.shape[1] - zg_ref.shape[1]
    pending = list(range(n_fill))

    def filler():
        if pending:
            q = pending.pop(0)
            zg_ref[:, pl.ds(q * piece, piece)] = _dot(
                hn_ref[...], win_ref[:, pl.ds(gate_col0 + q * piece, piece)]).astype(BF16)

    _rwkv_chunks_rest(items, h_ref, eye, blocks, filler)
    while pending:
        filler()
    ones = ones_ref[...]
    inv_n = 1.0 / RWKV_HEAD
    for c in range(n_chunks):
        rows = pl.ds(c * CHUNK, CHUNK)
        y = jnp.concatenate([it["y"] for it in items[c * n_pairs:(c + 1) * n_pairs]], axis=1)
        mean = _head_sum(y, ones) * inv_n
        yc = y - mean
        var = _head_sum(yc * yc, ones) * inv_n
        yn = yc * lax.rsqrt(var + GN_EPS) * lng_ref[...] + lnb_ref[...]
        o_ref[rows, :] = (yn + bonus_ref[rows, :]) * gate_ref[rows, :]


def _rwkv_branch(mix, w_in, n_gate, bsz, lnx_g, lnx_b, later_weights):
    *step_in, gate, bonus, hn = mix
    t, d = gate.shape
    seq = t // bsz
    tt = min(TT_RWKV, seq)
    nt = seq // tt
    n_tiles = bsz * nt
    n_steps = n_tiles + 1
    parts = max(p for p in range(1, n_steps + 1)
                if all(w.shape[0] % (p * BF16_TILE_ROWS) == 0 for w in later_weights))
    cast_block = lambda i: (jnp.minimum(i, parts - 1), 0)
    cast_specs = [pl.BlockSpec((w.shape[0] // parts, w.shape[1]), cast_block) for w in later_weights]
    ones = _head_ones()
    ltri = jnp.asarray(np.tril(np.ones((CHUNK, CHUNK), bool)), BF16)
    this_tile = lambda i: (jnp.minimum(i, n_tiles - 1), 0)
    prev_tile = lambda i: (jnp.maximum(i - 1, 0), 0)
    c2 = lambda i: (0, 0)
    consts = [lnx_g.reshape(1, d), lnx_b.reshape(1, d), ones, ltri]
    return pl.pallas_call(
        functools.partial(_rwkv_kernel, n_cast=len(later_weights), tiles_per_seq=nt),
        grid=(n_steps,),
        in_specs=[pl.BlockSpec((tt, d), this_tile) for _ in step_in]
        + [pl.BlockSpec((tt, d), prev_tile)] * 2 + [pl.BlockSpec((tt, hn.shape[1]), prev_tile)]
        + [pl.BlockSpec(w_in.shape, c2, pipeline_mode=pl.Buffered(1))]
        + [pl.BlockSpec(c.shape, c2) for c in consts] + cast_specs,
        out_specs=[pl.BlockSpec((tt, d), prev_tile), pl.BlockSpec((tt, n_gate), prev_tile)]
        + cast_specs,
        out_shape=[jax.ShapeDtypeStruct((t, d), F32), jax.ShapeDtypeStruct((t, n_gate), BF16)]
        + [jax.ShapeDtypeStruct(w.shape, BF16) for w in later_weights],
        scratch_shapes=[pltpu.VMEM((d // LANES, RWKV_HEAD, LANES), F32),
                        pltpu.VMEM((tt // CHUNK, 1, d), F32)]
        + [pltpu.VMEM((tt, d), F32)] * len(OPERANDS),
        compiler_params=pltpu.CompilerParams(dimension_semantics=("arbitrary",),
                                             vmem_limit_bytes=VMEM_LIMIT),
        name="rwkv",
    )(*step_in, gate, bonus, hn, w_in, *consts, *later_weights)


def _tail_kernel(x_ref, oa_ref, ob_ref, zg_ref, wb_ref, wout_ref, g2_ref,
                 w1_ref, w2_ref, gf_ref, o_ref, *, ff_block):
    x = x_ref[...]
    d = x.shape[1]
    d_mix = oa_ref.shape[1]
    gates = jax.nn.sigmoid(zg_ref[...].astype(F32))
    ma = _dot(oa_ref[...].astype(BF16), wb_ref[:d_mix, :])
    mb = _dot(ob_ref[...].astype(BF16), wb_ref[d_mix:, :])
    merged = gates[:, :d] * ma + gates[:, d:] * mb
    x1 = x + _dot(merged.astype(BF16), wout_ref[...])
    h2 = _rmsnorm(x1, g2_ref[...]).astype(BF16)
    acc = x1
    for j in range(w1_ref.shape[1] // ff_block):
        cols = slice(j * ff_block, (j + 1) * ff_block)
        hid = jnp.square(jnp.maximum(_dot(h2, w1_ref[:, cols]), 0.0))
        acc = acc + _dot(hid.astype(BF16), w2_ref[cols, :])
    o_ref[...] = _rmsnorm(acc, gf_ref[...])


def _tail(x2, o_a, o_b, zg, w_branch, w_out, norm2_g, w_ff1, w_ff2, norm_f_g):
    t, d = x2.shape
    d_mix = o_a.shape[1]
    tm = min(TM_PROJ, t)
    tok = lambda i: (i, 0)
    const = lambda i: (0, 0)
    resident = lambda a: pl.BlockSpec(a.shape, const, pipeline_mode=pl.Buffered(1))
    vec = lambda a: a.reshape(1, d)
    return pl.pallas_call(
        functools.partial(_tail_kernel, ff_block=d),
        grid=(t // tm,),
        in_specs=[pl.BlockSpec((tm, d), tok), pl.BlockSpec((tm, d_mix), tok),
                  pl.BlockSpec((tm, d_mix), tok), pl.BlockSpec((tm, zg.shape[1]), tok),
                  resident(w_branch), resident(w_out),
                  pl.BlockSpec((1, d), const), resident(w_ff1), resident(w_ff2),
                  pl.BlockSpec((1, d), const)],
        out_specs=pl.BlockSpec((tm, d), tok),
        out_shape=jax.ShapeDtypeStruct((t, d), F32),
        compiler_params=pltpu.CompilerParams(dimension_semantics=("arbitrary",),
                                             vmem_limit_bytes=VMEM_LIMIT),
        name="tail",
    )(x2, o_a, o_b, zg, w_branch, w_out, vec(norm2_g), w_ff1, w_ff2, vec(norm_f_g))


def _layer(x2, bsz, norm1_g, w_in, lam_re, lam_im, log_dt, b_re, b_im, c_re, c_im, d_skip, w_glu,
           b_glu, mu_rwkv, w0, w2, a0, a2, g2, k_k, k_a, r_k, lnx_g, lnx_b, w_branch, w_out,
           norm2_g, w_ff1, w_ff2, out_g):
    d_ssm = d_skip.shape[0]
    o_a, *mix, w_gate = _mix_in(x2, x2.shape[0] // bsz, norm1_g, w_in, d_ssm, mu_rwkv, w0, w2, a0, a2, g2,
                        k_k, k_a, r_k.reshape(-1),
                        (lam_re, lam_im, log_dt, b_re, b_im, c_re, c_im, d_skip, w_glu, b_glu))
    o_b, zg, *tail_weights = _rwkv_branch(mix, w_gate, w_gate.shape[1], bsz, lnx_g, lnx_b,
                                          (w_branch, w_out, w_ff1, w_ff2))
    w_branch, w_out, w_ff1, w_ff2 = tail_weights
    return _tail(x2, o_a, o_b, zg, w_branch, w_out, norm2_g, w_ff1, w_ff2, out_g)


def kernel(x, norm1_g, w_in, lam_re, lam_im, log_dt, b_re, b_im, c_re, c_im, d_skip, w_glu, b_glu,
           mu_rwkv, w0, w2, a0, a2, g2, k_k, k_a, r_k, lnx_g, lnx_b, w_branch, w_out, norm2_g,
           w_ff1, w_ff2, norm_f_g):
    bsz, seq, d = x.shape
    depth = norm1_g.shape[0]
    assert depth == 1, "the fused tail applies the final norm, so exactly one layer is supported"
    x2 = x.reshape(bsz * seq, d)
    l = 0
    out = _layer(x2, bsz, norm1_g[l], w_in[l], lam_re[l], lam_im[l], log_dt[l], b_re[l], b_im[l],
                 c_re[l], c_im[l], d_skip[l], w_glu[l], b_glu[l], mu_rwkv[l], w0[l], w2[l], a0[l],
                 a2[l], g2[l], k_k[l], k_a[l], r_k[l], lnx_g[l], lnx_b[l], w_branch[l], w_out[l],
                 norm2_g[l], w_ff1[l], w_ff2[l], norm_f_g)
    return out.reshape(bsz, seq, d)
```

```python
import functools
import math

import jax
import jax.numpy as jnp
import numpy as np
from jax import lax
from jax.experimental import pallas as pl
from jax.experimental.pallas import tpu as pltpu

F32 = jnp.float32
BF16 = jnp.bfloat16

RMS_EPS = 1e-6
GN_EPS = 64e-5
L2_EPS = 1e-12
MIN_NEG_REAL = -1e-4
EXP_NEG_HALF = 0.6065306597126334

SSM_GROUP = 16
SSM_STATE = 64
RWKV_HEAD = 64
GATE_LORA = 128
DECAY_LORA = 64
AAA_LORA = 64
CHUNK = 64

SUBLANES = 8
LANES = 128
BF16_TILE_ROWS = 2 * SUBLANES
VMEM_LIMIT = 56 * 1024 * 1024

TM_PROJ = 512
TT_S5 = 256
TT_RWKV = 256
S5_LANE_CHUNK = 512


def _dot(a, b):
    return jnp.dot(a, b, preferred_element_type=F32)


def _dot_nt(a, b):
    return lax.dot_general(a, b, (((1,), (1,)), ((), ())), preferred_element_type=F32)


def _split2(x):
    hi = x.astype(BF16)
    lo = (x - hi.astype(F32)).astype(BF16)
    return hi, lo


def _split3(x):
    hi = x.astype(BF16)
    r1 = x - hi.astype(F32)
    lo = r1.astype(BF16)
    lo2 = (r1 - lo.astype(F32)).astype(BF16)
    return hi, lo, lo2


def _rmsnorm(x, g):
    ms = jnp.mean(x * x, axis=-1, keepdims=True)
    return x * lax.rsqrt(ms + RMS_EPS) * g


def _cmul(ar, ai, br, bi):
    return ar * br - ai * bi, ar * bi + ai * br


def _s5_prep_kernel(rows_ref, grid_ref, a1_re_ref, a1_im_ref, a8_re_ref, a8_im_ref, apb_re_ref,
                    apb_im_ref, alp_re_ref, alp_im_ref, shl_re_ref, shl_im_ref, q_ref, *, seg_len):
    def discretise(lre, lim, ldt):
        lr = jnp.minimum(lre, MIN_NEG_REAL)
        dt = jnp.exp(ldt)
        mag = jnp.exp(lr * dt)
        ab_re = mag * jnp.cos(lim * dt)
        ab_im = mag * jnp.sin(lim * dt)
        den = lr * lr + lim * lim
        xm1 = ab_re - 1.0
        q_re = (xm1 * lr + ab_im * lim) / den
        q_im = (ab_im * lr - xm1 * lim) / den
        return ab_re, ab_im, q_re, q_im

    shape = a1_re_ref.shape
    a_re, a_im, _, _ = discretise(rows_ref[0], rows_ref[1], rows_ref[2])
    bcast = lambda x: jnp.broadcast_to(x, shape)
    pows = [(a_re, a_im)]
    for _ in range(SUBLANES - 1):
        pows.append(_cmul(*pows[-1], a_re, a_im))
    a1_re_ref[...] = bcast(a_re)
    a1_im_ref[...] = bcast(a_im)
    a8_re_ref[...] = bcast(pows[-1][0])
    a8_im_ref[...] = bcast(pows[-1][1])
    shape16 = (2 * SUBLANES,) + shape[1:]
    first = lax.broadcasted_iota(jnp.int32, shape16, 0) < SUBLANES
    for q in range(SUBLANES // 2):
        (r0, i0), (r1, i1) = pows[2 * q], pows[2 * q + 1]
        bc16 = lambda x: jnp.broadcast_to(x, shape16)
        apb_re_ref[q] = jnp.where(first, bc16(r0), bc16(r1)).astype(BF16)
        apb_im_ref[q] = jnp.where(first, bc16(i0), bc16(i1)).astype(BF16)
    lr_, li_ = pows[-1]
    n = SUBLANES
    while n < seg_len:
        lr_, li_ = _cmul(lr_, li_, lr_, li_)
        n *= 2
    lpows = [(lr_, li_)]
    for _ in range(SUBLANES - 1):
        lpows.append(_cmul(*lpows[-1], lr_, li_))
    row = lax.broadcasted_iota(jnp.int32, shape, 0)
    alp_re = jnp.zeros(shape, F32)
    alp_im = jnp.zeros(shape, F32)
    for i, (pr, pi) in enumerate(lpows):
        alp_re = jnp.where(row == i, bcast(pr), alp_re)
        alp_im = jnp.where(row == i, bcast(pi), alp_im)
    alp_re_ref[...] = alp_re
    alp_im_ref[...] = alp_im
    for i, s in enumerate((1, 2, 4)):
        pr, pi = lpows[s - 1]
        shl_re_ref[i] = jnp.where(row >= s, bcast(pr), 0.0)
        shl_im_ref[i] = jnp.where(row >= s, bcast(pi), 0.0)

    _, _, q_re, q_im = discretise(grid_ref[0], grid_ref[1], grid_ref[2])
    q_ref[0] = q_re
    q_ref[1] = q_im


def _s5_prep(lam_re, lam_im, log_dt, seg_len):
    g, p = lam_re.shape
    n = g * p
    lam3 = jnp.stack([lam_re, lam_im, jnp.broadcast_to(log_dt[:, None], (g, p))])
    s8 = jax.ShapeDtypeStruct((SUBLANES, n), F32)
    s88 = jax.ShapeDtypeStruct((SUBLANES // 2, 2 * SUBLANES, n), BF16)
    s38 = jax.ShapeDtypeStruct((3, SUBLANES, n), F32)
    sq = jax.ShapeDtypeStruct((2, g, p), F32)
    out_shape = (s8, s8, s8, s8, s88, s88, s8, s8, s38, s38, sq)
    return pl.pallas_call(functools.partial(_s5_prep_kernel, seg_len=seg_len),
                          out_shape=out_shape, name="s5_prep")(lam3.reshape(3, 1, n), lam3)


def _head_ones():
    head = np.arange(LANES) // RWKV_HEAD
    return jnp.asarray(head[:, None] == head[None, :], BF16)


def _head_sum(x, ones):
    rows, width = x.shape
    groups = width // LANES
    stacked = jnp.concatenate([x[:, j * LANES:(j + 1) * LANES] for j in range(groups)], axis=0)
    sums = _dot(stacked.astype(BF16), ones)
    return jnp.concatenate([sums[j * rows:(j + 1) * rows] for j in range(groups)], axis=1)


def _mix_in_kernel(*refs, d_ssm, d, tiles_per_seq, tt_s5):
    (x_ref, g_ref, w_f32_ref, mu_ref, wwa_ref, g2_ref, w0_ref, a0_ref, kk_ref, ka_ref, rk_ref,
     ones_ref) = refs[:12]
    s5 = dict(zip(S5_CONSTS, refs[12:12 + len(S5_CONSTS)]))
    (w_rows_ref, oa_ref, r_ref, k2_ref, v_ref, av_ref, bv_ref, logw_ref, gate_ref, bonus_ref, h_ref,
     wg_rows_ref, w_ref, prev_ref, carry_ref, *state_refs) = refs[12 + len(S5_CONSTS):]
    xs_refs, xb_refs = state_refs[:len(state_refs) // 2], state_refs[len(state_refs) // 2:]
    tm = x_ref.shape[0]

    @pl.when(pl.program_id(0) == 0)
    def _():
        rows_per_iter = math.gcd(w_ref.shape[0], 4 * BF16_TILE_ROWS)

        def cast_rows(i, _):
            rows = pl.ds(pl.multiple_of(i * rows_per_iter, rows_per_iter), rows_per_iter)
            w_ref[rows, :] = w_f32_ref[rows, :].astype(BF16)
            return 0

        lax.fori_loop(0, w_ref.shape[0] // rows_per_iter, cast_rows, 0)

    @pl.when(pl.program_id(0) % tiles_per_seq == 0)
    def _():
        prev_ref[...] = jnp.zeros_like(prev_ref)
        carry_ref[...] = jnp.zeros_like(carry_ref)

    wg_rows_ref[...] = w_rows_ref[:, w_rows_ref.shape[1] - wg_rows_ref.shape[1]:].astype(BF16)

    h = _rmsnorm(x_ref[...], g_ref[...]).astype(BF16)
    h_ref[...] = h

    def mixed(c0, c1):
        p = _dot(h, w_ref[:, d_ssm + c0:d_ssm + c1])
        shape8 = (SUBLANES, c1 - c0)
        row = lax.broadcasted_iota(jnp.int32, shape8, 0)
        last = prev_ref[SUBLANES - 1:SUBLANES, c0:c1]
        prev = pltpu.roll(p, 1, 0)
        head = jnp.where(row == 0, jnp.broadcast_to(last, shape8), prev[:SUBLANES])
        prev = jnp.concatenate([head, prev[SUBLANES:]], axis=0)
        prev_ref[:, c0:c1] = p[tm - SUBLANES:, :]
        return p + (prev - p) * mu_ref[:, c0:c1]

    sub = lambda i: slice(i * tt_s5, (i + 1) * tt_s5)
    n_sub = tm // tt_s5
    us = [_dot(h[sub(i)], w_ref[:, :d_ssm]) for i in range(n_sub)]
    for i in range(n_sub):
        _s5_project(us[i], s5, xs_refs[i])
    _s5_scan(s5, xs_refs[0], xb_refs[0], carry_ref)
    lora_in = mixed(3 * d, 3 * d + GATE_LORA + DECAY_LORA + AAA_LORA)
    k = mixed(d, 2 * d)
    xg = lora_in[:, :GATE_LORA]
    xwa = lora_in[:, GATE_LORA:]
    lane = lax.broadcasted_iota(jnp.int32, xwa.shape, 1)
    zwa = jnp.where(lane < DECAY_LORA, jnp.tanh(xwa), xwa)
    lora = _dot(zwa.astype(BF16), wwa_ref[...])
    gate_ref[...] = _dot(jax.nn.sigmoid(xg).astype(BF16), g2_ref[...])
    logw_ref[...] = -EXP_NEG_HALF * jax.nn.sigmoid(w0_ref[...] + lora[:, :d])
    a = jax.nn.sigmoid(a0_ref[...] + lora[:, d:])
    for i in range(1, n_sub):
        _s5_scan(s5, xs_refs[i], xb_refs[i], carry_ref)
    oa_ref[sub(0), :] = _s5_readout(us[0], s5, xb_refs[0])

    ones = ones_ref[...]
    r = mixed(0, d)
    kk = k * kk_ref[...]
    kk = kk * lax.rsqrt(jnp.maximum(_head_sum(kk * kk, ones), L2_EPS * L2_EPS))
    v = mixed(2 * d, 3 * d)
    k2 = k * (1.0 + (a - 1.0) * ka_ref[...])
    k2_ref[...] = k2.astype(BF16)
    av_ref[...] = (-kk).astype(BF16)
    bv_ref[...] = (kk * a).astype(BF16)
    r_ref[...] = r.astype(BF16)
    v_ref[...] = v.astype(BF16)
    bonus_ref[...] = _head_sum(r * k2 * rk_ref[...], ones) * v
    for i in range(1, n_sub):
        oa_ref[sub(i), :] = _s5_readout(us[i], s5, xb_refs[i])


def _mix_in(x2, seq, norm_g, w_in, d_ssm, mu, w0, w2, a0, a2, g2, k_k, k_a, r_k, s5_params):
    t, dm = x2.shape
    n_mix = d_ssm + mu.shape[0]
    n_gate = w_in.shape[1] - n_mix
    d = w0.shape[0]
    tm = min(TM_PROJ, seq)
    tt_s5 = min(TT_S5, tm)
    s5_consts = _s5_constants(tt_s5, *s5_params)
    n_state2 = s5_consts[2].shape[2]
    wwa = jnp.concatenate([jnp.concatenate([w2, jnp.zeros_like(w2)], axis=1),
                           jnp.concatenate([jnp.zeros_like(a2), a2], axis=1)], axis=0)
    ones = _head_ones()
    vec = lambda a: a.reshape(1, -1)
    consts = [vec(norm_g), w_in, vec(mu), wwa.astype(BF16), g2.astype(BF16), vec(w0), vec(a0),
              vec(k_k), vec(k_a), vec(r_k), ones, *s5_consts]
    whole = lambda a: pl.BlockSpec(a.shape, lambda i: (0,) * a.ndim)
    mixer_cols = pl.BlockSpec((dm, n_mix), lambda i: (0, 0), pipeline_mode=pl.Buffered(1))
    tok = lambda i: (i, 0)
    widths = [(d_ssm, F32)] + [(d, BF16)] * 5 + [(d, F32)] * 3 + [(dm, BF16)]
    n_steps = t // tm
    parts = max(p for p in range(1, n_steps + 1) if dm % (p * BF16_TILE_ROWS) == 0)
    cast_block = lambda i: (jnp.minimum(i, parts - 1), 0)
    return pl.pallas_call(
        functools.partial(_mix_in_kernel, d_ssm=d_ssm, d=d, tiles_per_seq=seq // tm, tt_s5=tt_s5),
        grid=(n_steps,),
        in_specs=[pl.BlockSpec((tm, dm), tok)]
        + [mixer_cols if c is w_in else whole(c) for c in consts]
        + [pl.BlockSpec((dm // parts, w_in.shape[1]), cast_block)],
        out_specs=[pl.BlockSpec((tm, w), tok) for w, _ in widths]
        + [pl.BlockSpec((dm // parts, n_gate), cast_block)],
        out_shape=[jax.ShapeDtypeStruct((t, w), dt) for w, dt in widths]
        + [jax.ShapeDtypeStruct((dm, n_gate), BF16)],
        scratch_shapes=[pltpu.VMEM((dm, n_mix), BF16), pltpu.VMEM((SUBLANES, mu.shape[0]), F32),
                        pltpu.VMEM((SUBLANES, 2 * n_state2), F32)]
        + [pltpu.VMEM((tt_s5, 2 * n_state2), F32)] * (tm // tt_s5)
        + [pltpu.VMEM((tt_s5, 2 * n_state2), BF16)] * (tm // tt_s5),
        compiler_params=pltpu.CompilerParams(dimension_semantics=("arbitrary",),
                                             vmem_limit_bytes=VMEM_LIMIT),
        name="mix_in",
    )(x2, *consts, w_in)


S5_CONSTS = ("perm", "permt", "bmat", "cmat", "a1_re", "a1_im", "a8_re", "a8_im", "apb_re", "apb_im",
             "alp_re", "alp_im", "shl_re", "shl_im", "d_skip", "w_glu", "b_glu")


def _s5_project(u, s5, xs_ref):
    n_state = xs_ref.shape[1] // 2
    half = n_state // 2
    lw = S5_LANE_CHUNK
    ub = _dot(s5["perm"][...], u.astype(BF16)).astype(BF16)
    d_half = ub.shape[1] // 2
    for c in range(n_state // lw):
        h, cc = divmod(c, half // lw)
        for part in range(2):
            xs_ref[:, pl.ds(part * n_state + c * lw, lw)] = _dot(
                ub[:, h * d_half:(h + 1) * d_half],
                s5["bmat"][h, :, pl.ds(part * half + cc * lw, lw)])


def _s5_scan(s5, xs_ref, xb_ref, carry_ref):
    tt = xs_ref.shape[0]
    n_state = xs_ref.shape[1] // 2
    seg_len = tt // SUBLANES
    lw = S5_LANE_CHUNK
    blk = lambda j: pl.ds(j * SUBLANES, SUBLANES)
    for c in range(n_state // lw):
        re_sl, im_sl = pl.ds(c * lw, lw), pl.ds(n_state + c * lw, lw)
        ar, ai = s5["a1_re"][:, re_sl], s5["a1_im"][:, re_sl]
        xr = jnp.zeros((SUBLANES, lw), F32)
        xi = jnp.zeros((SUBLANES, lw), F32)
        for j in range(seg_len):
            br = xs_ref[blk(j), re_sl]
            bi = xs_ref[blk(j), im_sl]
            xr, xi = ar * xr - ai * xi + br, ar * xi + ai * xr + bi
            xs_ref[blk(j), re_sl] = xr
            xs_ref[blk(j), im_sl] = xi
        er, ei = xr, xi
        for k, sh in enumerate((1, 2, 4)):
            pr, pi = s5["shl_re"][k, :, re_sl], s5["shl_im"][k, :, re_sl]
            sr, si = pltpu.roll(er, sh, 0), pltpu.roll(ei, sh, 0)
            er, ei = er + (pr * sr - pi * si), ei + (pr * si + pi * sr)
        tr, ti = carry_ref[:, re_sl], carry_ref[:, im_sl]
        pr, pi = s5["alp_re"][:, re_sl], s5["alp_im"][:, re_sl]
        fr = er + (pr * tr - pi * ti)
        fi = ei + (pr * ti + pi * tr)
        row = lax.broadcasted_iota(jnp.int32, fr.shape, 0)
        dr = jnp.where(row == 0, tr, pltpu.roll(fr, 1, 0))
        di = jnp.where(row == 0, ti, pltpu.roll(fi, 1, 0))
        last = SUBLANES - 1
        carry_ref[:, re_sl] = jnp.broadcast_to(fr[last:last + 1], fr.shape)
        carry_ref[:, im_sl] = jnp.broadcast_to(fi[last:last + 1], fi.shape)
        a8r, a8i = s5["a8_re"][:, re_sl], s5["a8_im"][:, re_sl]
        both = lambda x: jnp.concatenate([x, x], axis=0).astype(BF16)
        for a in range(seg_len // SUBLANES):
            dr2, di2 = both(dr), both(di)
            for q in range(SUBLANES // 2):
                rows = pl.ds((a * SUBLANES + 2 * q) * SUBLANES, 2 * SUBLANES)
                pr, pi = s5["apb_re"][q, :, re_sl], s5["apb_im"][q, :, re_sl]
                xb_ref[rows, re_sl] = xs_ref[rows, re_sl].astype(BF16) + (pr * dr2 - pi * di2)
                xb_ref[rows, im_sl] = xs_ref[rows, im_sl].astype(BF16) + (pr * di2 + pi * dr2)
            dr, di = a8r * dr - a8i * di, a8r * di + a8i * dr


def _s5_readout(u, s5, xb_ref):
    n_state = xb_ref.shape[1] // 2
    half = n_state // 2
    lw = S5_LANE_CHUNK
    n_chunks = n_state // lw
    width = u.shape[1] // n_chunks
    blocks = []
    for c in range(n_chunks):
        h, cc = divmod(c, half // lw)
        cols = pl.ds(cc * width, width)
        blocks.append(sum(_dot(xb_ref[:, pl.ds(part * n_state + c * lw, lw)],
                               s5["cmat"][h, pl.ds(part * half + cc * lw, lw), cols])
                          for part in range(2)))
    cx_hi, cx_lo = _split2(jnp.concatenate(blocks, axis=1))
    permt = s5["permt"][...]
    y = _dot(permt, cx_hi) + _dot(permt, cx_lo) + s5["d_skip"][...] * u
    z = jax.nn.gelu(y)
    gate = jax.nn.sigmoid(_dot(z.astype(BF16), s5["w_glu"][...]) + s5["b_glu"][...])
    return z * gate


def _s5_constants(tt, lam_re, lam_im, log_dt, b_re, b_im, c_re, c_im, d_skip, w_glu, b_glu):
    d_ssm = d_skip.shape[0]
    n_grp, n_state, grp = b_re.shape
    seg_len = tt // SUBLANES
    *scan_consts, q = _s5_prep(lam_re, lam_im, log_dt, seg_len)
    q_re, q_im = q[0][:, :, None], q[1][:, :, None]
    bb = jnp.stack([q_re * b_re - q_im * b_im, q_re * b_im + q_im * b_re])
    cc = jnp.stack([c_re, -c_im])
    hg = n_grp // 2
    hc, hs2 = hg * grp, 2 * hg * n_state
    bb = bb.reshape(2, 2, hg, n_state, grp).transpose(1, 2, 4, 0, 3).reshape(2, hc, 2 * n_state)
    cc = cc.reshape(2, 2, hg, grp, n_state).transpose(1, 0, 4, 2, 3).reshape(2, 2 * n_state, hc)
    col = np.arange(hs2)
    repeat = (col[None, :] // (hg * n_state) == np.arange(2 * n_state)[:, None] // n_state) \
        & (col[None, :] % n_state == np.arange(2 * n_state)[:, None] % n_state)
    same = np.arange(hc)[:, None] // grp == (col[None, :] // n_state) % hg
    repeat = jnp.asarray(repeat, BF16)
    bmat = jnp.einsum("hck,kn->hcn", bb.astype(BF16), repeat, preferred_element_type=BF16)
    bmat = jnp.where(same, bmat, 0.0)
    cmat = jnp.einsum("nk,hkc->hnc", repeat.T, cc.astype(BF16), preferred_element_type=BF16)
    cmat = jnp.where(same.T, cmat, 0.0)
    rho = np.arange(tt)
    src = (rho % SUBLANES) * seg_len + rho // SUBLANES
    perm = src[:, None] == np.arange(tt)[None, :]
    return [jnp.asarray(perm, BF16), jnp.asarray(perm.T, BF16), bmat, cmat, *scan_consts,
            d_skip.reshape(1, d_ssm), w_glu.astype(BF16), b_glu.reshape(1, d_ssm)]


def _pair_blocks(x):
    lane = lax.broadcasted_iota(jnp.int32, x.shape, 1)
    first = lane < RWKV_HEAD
    return jnp.concatenate([jnp.where(first, x, 0.0), jnp.where(first, 0.0, x)], axis=0)


def _pair_transpose(x):
    t = _pair_blocks(x).T
    return t[:CHUNK] + t[CHUNK:]


def _rwkv_chunks_scores(items):
    shape = (CHUNK, LANES)
    row = lax.broadcasted_iota(jnp.int32, shape, 0)
    col = lax.broadcasted_iota(jnp.int32, shape, 1) % RWKV_HEAD
    strict = row > col
    incl = row >= col
    eye = row == col
    blocks = lambda x: _pair_blocks(x).astype(BF16)

    for it in items:
        lhs = jnp.concatenate([it["at"], it["rt"]], axis=0).astype(BF16)
        rhs = jnp.concatenate([blocks(it["bt"]), blocks(it["kt"])], axis=0)
        sc = _dot_nt(lhs, rhs)
        it["s_ab"] = jnp.where(strict, sc[:CHUNK, :LANES], 0.0)
        it["s_ak"] = jnp.where(strict, sc[:CHUNK, LANES:], 0.0).astype(BF16)
        it["s_r"] = jnp.concatenate([jnp.where(incl, sc[CHUNK:, :LANES], 0.0),
                                     jnp.where(incl, sc[CHUNK:, LANES:], 0.0)], axis=1).astype(BF16)
        it["bd_v"] = blocks(it["v"])
        it["bk_t"] = jnp.concatenate([_pair_transpose(it["bh"]), _pair_transpose(it["kh"])],
                                     axis=1).astype(BF16)

    return strict, incl, eye, blocks


def _rwkv_chunks_rest(items, h_ref, eye, blocks, filler):
    shape = (CHUNK, LANES)
    for it in items:
        it["npow"] = _dot(it["s_ab"].astype(BF16), blocks(it["s_ab"]))
        it["inv"] = jnp.where(eye, 1.0, 0.0) + it["s_ab"]
    filler()
    for step in range(4):
        for it in items:
            both = _dot(jnp.concatenate([it["inv"], it["npow"]], axis=0).astype(BF16),
                        blocks(it["npow"]))
            it["inv"] = it["inv"] + both[:CHUNK]
            it["npow"] = both[CHUNK:]
        filler()
    for it in items:
        it["inv"] = it["inv"] + _dot(it["inv"].astype(BF16), blocks(it["npow"]))
    filler()

    for it in items:
        it["av"] = _dot(it["s_ak"], it["bd_v"])
    for it in items:
        wu = _dot(it["inv"].astype(BF16),
                  jnp.concatenate([blocks(it["at"]), blocks(it["av"])], axis=1))
        it["z"] = jnp.concatenate(
            [jnp.concatenate([blocks(wu[:, :LANES]), blocks(wu[:, LANES:])], axis=1),
             jnp.concatenate([jnp.zeros((LANES, LANES), BF16), it["bd_v"]], axis=1)], axis=0)
    filler()
    for it in items:
        both = _dot(jnp.concatenate([it["s_r"], it["bk_t"]], axis=0), it["z"])
        qy, mg = both[:CHUNK], both[CHUNK:]
        m = jnp.where(eye, jnp.broadcast_to(it["decay"], shape), 0.0) + mg[:, :LANES]
        it["qm"] = jnp.concatenate([it["rt"] + qy[:, :LANES], m], axis=0).astype(BF16)
        it["yp"] = qy[:, LANES:]
        it["g"] = mg[:, LANES:]
    filler()

    for it in items:
        both = _dot(it["qm"], blocks(h_ref[it["pair"]]))
        it["y"] = both[:CHUNK] + it["yp"]
        h_ref[it["pair"]] = both[CHUNK:] + it["g"]


OPERANDS = ("at", "rt", "bt", "kt", "bh", "kh", "v")


def _rwkv_kernel(r_ref, k2_ref, v_ref, av_ref, bv_ref, logw_ref, gate_ref, bonus_ref, hn_ref,
                 win_ref, lng_ref, lnb_ref, ones_ref, ltri_ref, *refs, n_cast, tiles_per_seq):
    cast_in, (o_ref, zg_ref) = refs[:n_cast], refs[n_cast:n_cast + 2]
    cast_out = refs[n_cast + 2:2 * n_cast + 2]
    h_ref, decay_ref, *op_refs = refs[2 * n_cast + 2:]
    for src, dst in zip(cast_in, cast_out):
        dst[...] = src[...].astype(BF16)
    tt, d = r_ref.shape
    n_pairs = d // LANES
    n_chunks = tt // CHUNK
    ops = dict(zip(OPERANDS, op_refs))

    step = pl.program_id(0)

    @pl.when(step == 0)
    def _():
        for ref in (decay_ref,) + tuple(op_refs):
            ref[...] = jnp.zeros_like(ref)

    @pl.when(lax.rem(jnp.maximum(step - 1, 0), tiles_per_seq) == 0)
    def _():
        h_ref[...] = jnp.zeros_like(h_ref)

    items = []
    for c in range(n_chunks):
        rows = pl.ds(c * CHUNK, CHUNK)
        for j in range(n_pairs):
            ln = pl.ds(j * LANES, LANES)
            item = {name: ref[rows, ln] for name, ref in ops.items()}
            item["decay"] = decay_ref[c, :, ln]
            item["pair"] = j
            items.append(item)
    _, _, eye, blocks = _rwkv_chunks_scores(items)

    for c in range(n_chunks):
        rows = pl.ds(c * CHUNK, CHUNK)
        load = lambda ref: ref[rows, :].astype(F32)
        r, k2, v, av, bv, logw = map(load, (r_ref, k2_ref, v_ref, av_ref, bv_ref, logw_ref))
        l_hi, l_lo = _split2(logw)
        lc = _dot(ltri_ref[...], l_hi) + _dot(ltri_ref[...], l_lo)
        lc_end = lc[CHUNK - 1:CHUNK]
        e_in = jnp.exp(lc - logw)
        e_inc = jnp.exp(lc)
        e_neg = jnp.exp(-lc)
        e_out = jnp.exp(lc_end - lc)
        new = dict(at=av * e_in, rt=r * e_inc, bt=bv * e_neg, kt=k2 * e_neg, bh=bv * e_out,
                   kh=k2 * e_out, v=v)
        for name, val in new.items():
            ops[name][rows, :] = val
        decay_ref[c] = jnp.exp(lc_end)

    n_fill = 8
    piece = zg_ref.shape[1] // n_fill
    gate_col0 = win_ref.shape[1] - zg_ref.shape[1]
    pending = list(range(n_fill))

    def filler():
        if pending:
            q = pending.pop(0)
            zg_ref[:, pl.ds(q * piece, piece)] = _dot(
                hn_ref[...], win_ref[:, pl.ds(gate_col0 + q * piece, piece)]).astype(BF16)

    _rwkv_chunks_rest(items, h_ref, eye, blocks, filler)
    while pending:
        filler()
    ones = ones_ref[...]
    inv_n = 1.0 / RWKV_HEAD
    for c in range(n_chunks):
        rows = pl.ds(c * CHUNK, CHUNK)
        y = jnp.concatenate([it["y"] for it in items[c * n_pairs:(c + 1) * n_pairs]], axis=1)
        mean = _head_sum(y, ones) * inv_n
        yc = y - mean
        var = _head_sum(yc * yc, ones) * inv_n
        yn = yc * lax.rsqrt(var + GN_EPS) * lng_ref[...] + lnb_ref[...]
        o_ref[rows, :] = (yn + bonus_ref[rows, :]) * gate_ref[rows, :]


def _rwkv_branch(mix, w_in, n_gate, bsz, lnx_g, lnx_b, later_weights):
    *step_in, gate, bonus, hn = mix
    t, d = gate.shape
    seq = t // bsz
    tt = min(TT_RWKV, seq)
    nt = seq // tt
    n_tiles = bsz * nt
    n_steps = n_tiles + 1
    parts = max(p for p in range(1, n_steps + 1)
                if all(w.shape[0] % (p * BF16_TILE_ROWS) == 0 for w in later_weights))
    cast_block = lambda i: (jnp.minimum(i, parts - 1), 0)
    cast_specs = [pl.BlockSpec((w.shape[0] // parts, w.shape[1]), cast_block) for w in later_weights]
    ones = _head_ones()
    ltri = jnp.asarray(np.tril(np.ones((CHUNK, CHUNK), bool)), BF16)
    this_tile = lambda i: (jnp.minimum(i, n_tiles - 1), 0)
    prev_tile = lambda i: (jnp.maximum(i - 1, 0), 0)
    c2 = lambda i: (0, 0)
    consts = [lnx_g.reshape(1, d), lnx_b.reshape(1, d), ones, ltri]
    return pl.pallas_call(
        functools.partial(_rwkv_kernel, n_cast=len(later_weights), tiles_per_seq=nt),
        grid=(n_steps,),
        in_specs=[pl.BlockSpec((tt, d), this_tile) for _ in step_in]
        + [pl.BlockSpec((tt, d), prev_tile)] * 2 + [pl.BlockSpec((tt, hn.shape[1]), prev_tile)]
        + [pl.BlockSpec(w_in.shape, c2, pipeline_mode=pl.Buffered(1))]
        + [pl.BlockSpec(c.shape, c2) for c in consts] + cast_specs,
        out_specs=[pl.BlockSpec((tt, d), prev_tile), pl.BlockSpec((tt, n_gate), prev_tile)]
        + cast_specs,
        out_shape=[jax.ShapeDtypeStruct((t, d), F32), jax.ShapeDtypeStruct((t, n_gate), BF16)]
        + [jax.ShapeDtypeStruct(w.shape, BF16) for w in later_weights],
        scratch_shapes=[pltpu.VMEM((d // LANES, RWKV_HEAD, LANES), F32),
                        pltpu.VMEM((tt // CHUNK, 1, d), F32)]
        + [pltpu.VMEM((tt, d), F32)] * len(OPERANDS),
        compiler_params=pltpu.CompilerParams(dimension_semantics=("arbitrary",),
                                             vmem_limit_bytes=VMEM_LIMIT),
        name="rwkv",
    )(*step_in, gate, bonus, hn, w_in, *consts, *later_weights)


def _tail_kernel(x_ref, oa_ref, ob_ref, zg_ref, wb_ref, wout_ref, g2_ref,
                 w1_ref, w2_ref, gf_ref, o_ref, *, ff_block):
    x = x_ref[...]
    d = x.shape[1]
    d_mix = oa_ref.shape[1]
    gates = jax.nn.sigmoid(zg_ref[...].astype(F32))
    ma = _dot(oa_ref[...].astype(BF16), wb_ref[:d_mix, :])
    mb = _dot(ob_ref[...].astype(BF16), wb_ref[d_mix:, :])
    merged = gates[:, :d] * ma + gates[:, d:] * mb
    x1 = x + _dot(merged.astype(BF16), wout_ref[...])
    h2 = _rmsnorm(x1, g2_ref[...]).astype(BF16)
    acc = x1
    for j in range(w1_ref.shape[1] // ff_block):
        cols = slice(j * ff_block, (j + 1) * ff_block)
        hid = jnp.square(jnp.maximum(_dot(h2, w1_ref[:, cols]), 0.0))
        acc = acc + _dot(hid.astype(BF16), w2_ref[cols, :])
    o_ref[...] = _rmsnorm(acc, gf_ref[...])


def _tail(x2, o_a, o_b, zg, w_branch, w_out, norm2_g, w_ff1, w_ff2, norm_f_g):
    t, d = x2.shape
    d_mix = o_a.shape[1]
    tm = min(TM_PROJ, t)
    tok = lambda i: (i, 0)
    const = lambda i: (0, 0)
    resident = lambda a: pl.BlockSpec(a.shape, const, pipeline_mode=pl.Buffered(1))
    vec = lambda a: a.reshape(1, d)
    return pl.pallas_call(
        functools.partial(_tail_kernel, ff_block=d),
        grid=(t // tm,),
        in_specs=[pl.BlockSpec((tm, d), tok), pl.BlockSpec((tm, d_mix), tok),
                  pl.BlockSpec((tm, d_mix), tok), pl.BlockSpec((tm, zg.shape[1]), tok),
                  resident(w_branch), resident(w_out),
                  pl.BlockSpec((1, d), const), resident(w_ff1), resident(w_ff2),
                  pl.BlockSpec((1, d), const)],
        out_specs=pl.BlockSpec((tm, d), tok),
        out_shape=jax.ShapeDtypeStruct((t, d), F32),
        compiler_params=pltpu.CompilerParams(dimension_semantics=("arbitrary",),
                                             vmem_limit_bytes=VMEM_LIMIT),
        name="tail",
    )(x2, o_a, o_b, zg, w_branch, w_out, vec(norm2_g), w_ff1, w_ff2, vec(norm_f_g))


def _layer(x2, bsz, norm1_g, w_in, lam_re, lam_im, log_dt, b_re, b_im, c_re, c_im, d_skip, w_glu,
           b_glu, mu_rwkv, w0, w2, a0, a2, g2, k_k, k_a, r_k, lnx_g, lnx_b, w_branch, w_out,
           norm2_g, w_ff1, w_ff2, out_g):
    d_ssm = d_skip.shape[0]
    o_a, *mix, w_gate = _mix_in(x2, x2.shape[0] // bsz, norm1_g, w_in, d_ssm, mu_rwkv, w0, w2, a0, a2, g2,
                        k_k, k_a, r_k.reshape(-1),
                        (lam_re, lam_im, log_dt, b_re, b_im, c_re, c_im, d_skip, w_glu, b_glu))
    o_b, zg, *tail_weights = _rwkv_branch(mix, w_gate, w_gate.shape[1], bsz, lnx_g, lnx_b,
                                          (w_branch, w_out, w_ff1, w_ff2))
    w_branch, w_out, w_ff1, w_ff2 = tail_weights
    return _tail(x2, o_a, o_b, zg, w_branch, w_out, norm2_g, w_ff1, w_ff2, out_g)


def kernel(x, norm1_g, w_in, lam_re, lam_im, log_dt, b_re, b_im, c_re, c_im, d_skip, w_glu, b_glu,
           mu_rwkv, w0, w2, a0, a2, g2, k_k, k_a, r_k, lnx_g, lnx_b, w_branch, w_out, norm2_g,
           w_ff1, w_ff2, norm_f_g):
    bsz, seq, d = x.shape
    depth = norm1_g.shape[0]
    assert depth == 1, "the fused tail applies the final norm, so exactly one layer is supported"
    x2 = x.reshape(bsz * seq, d)
    l = 0
    out = _layer(x2, bsz, norm1_g[l], w_in[l], lam_re[l], lam_im[l], log_dt[l], b_re[l], b_im[l],
                 c_re[l], c_im[l], d_skip[l], w_glu[l], b_glu[l], mu_rwkv[l], w0[l], w2[l], a0[l],
                 a2[l], g2[l], k_k[l], k_a[l], r_k[l], lnx_g[l], lnx_b[l], w_branch[l], w_out[l],
                 norm2_g[l], w_ff1[l], w_ff2[l], norm_f_g)
    return out.reshape(bsz, seq, d)
```
